```python
import math
import jax, jax.numpy as jnp
from jax import lax
import numpy as np

D_MODEL = 1024
BATCH = 4
SEQ = 8192
DEPTH = 2

N_A_LAYERS = DEPTH // 2
N_B_LAYERS = DEPTH - N_A_LAYERS

DN_HEADS = 8
DN_HEAD_K = 128
DN_HEAD_V = 128
DN_CONV = 4
DN_CHUNK = 64
DN_KEY_WIDTH = DN_HEADS * DN_HEAD_K
DN_VAL_WIDTH = DN_HEADS * DN_HEAD_V
DN_QKV_WIDTH = 2 * DN_KEY_WIDTH + DN_VAL_WIDTH
DN_IN_WIDTH = DN_QKV_WIDTH + DN_VAL_WIDTH + 2 * DN_HEADS

NSA_HEADS = 16
NSA_KV_GROUPS = 4
NSA_HEAD_DIM = 64
NSA_REP = NSA_HEADS // NSA_KV_GROUPS
NSA_WIDTH = NSA_HEADS * NSA_HEAD_DIM
N_BRANCH = 3
CMP_BLOCK = 32
CMP_STRIDE = 16
CMP_HIDDEN = 128
SEL_BLOCK = 64
SEL_TOPK = 16
WINDOW = 512
Q_BLOCK = 128
FORCED_SCORE = 1.0e4
NSA_IN_WIDTH = NSA_WIDTH + N_BRANCH * NSA_WIDTH + N_BRANCH * NSA_HEADS
SHARED_KV_WIDTH = N_BRANCH * 2 * NSA_KV_GROUPS * NSA_HEAD_DIM

ROPE_THETA = 10000.0
NORM_EPS = 1e-6
DEEPNORM_ALPHA = (2.0 * DEPTH) ** 0.25
DEEPNORM_BETA = (8.0 * DEPTH) ** -0.25

kernel_name = 'yoco_gdn_nsa_deepnorm_hybrid'


def layer_norm(x, w, b):
    xf = x.astype(jnp.float32)
    mu = jnp.mean(xf, axis=-1, keepdims=True)
    var = jnp.mean(jnp.square(xf - mu), axis=-1, keepdims=True)
    return ((xf - mu) * lax.rsqrt(var + NORM_EPS) * w.astype(jnp.float32) + b.astype(jnp.float32)).astype(x.dtype)


def rms_norm(x, w):
    xf = x.astype(jnp.float32)
    return xf * lax.rsqrt(jnp.mean(jnp.square(xf), axis=-1, keepdims=True) + NORM_EPS) * w.astype(jnp.float32)


def l2_normalize(x):
    xf = x.astype(jnp.float32)
    return xf * lax.rsqrt(jnp.sum(jnp.square(xf), axis=-1, keepdims=True) + NORM_EPS)


def rope(x, pos):
    half = x.shape[-1] // 2
    inv_freq = ROPE_THETA ** (-jnp.arange(half, dtype=jnp.float32) / half)
    ang = pos.astype(jnp.float32)[:, None] * inv_freq[None, :]
    cos, sin = jnp.cos(ang), jnp.sin(ang)
    x1 = x[..., :half].astype(jnp.float32)
    x2 = x[..., half:].astype(jnp.float32)
    out = jnp.concatenate([x1 * cos - x2 * sin, x2 * cos + x1 * sin], axis=-1)
    return out.astype(x.dtype)


def causal_short_conv(x, w):
    k_len = w.shape[0]
    t_len = x.shape[1]
    xp = jnp.pad(x, ((0, 0), (k_len - 1, 0), (0, 0)))
    y = xp[:, 0:t_len] * w[0]
    for i in range(1, k_len):
        y = y + xp[:, i:i + t_len] * w[i]
    return y


def gated_delta_rule_chunked(q, k, v, g, beta):
    f32 = jnp.float32
    b_sz, h_sz, t_len, dk = q.shape
    dv = v.shape[-1]
    c = DN_CHUNK
    n = t_len // c
    q = q.astype(f32).reshape(b_sz, h_sz, n, c, dk) * (dk ** -0.5)
    k = k.astype(f32).reshape(b_sz, h_sz, n, c, dk)
    v = v.astype(f32).reshape(b_sz, h_sz, n, c, dv)
    beta = beta.astype(f32).reshape(b_sz, h_sz, n, c, 1)
    gc = jnp.cumsum(g.astype(f32).reshape(b_sz, h_sz, n, c), axis=-1)
    idx = jnp.arange(c)
    incl = idx[:, None] >= idx[None, :]
    strict = idx[:, None] > idx[None, :]
    diff = gc[..., :, None] - gc[..., None, :]
    decay = jnp.where(incl, jnp.exp(jnp.where(incl, diff, 0.0)), 0.0)
    kb = k * beta
    low = jnp.where(strict, jnp.einsum('bhnid,bhnjd->bhnij', kb, k) * decay, 0.0)
    t_mat = low + jnp.eye(c, dtype=f32)
    rhs = jnp.concatenate([v * beta, kb * jnp.exp(gc)[..., None]], axis=-1)
    sol = lax.linalg.triangular_solve(t_mat, rhs, left_side=True, lower=True, unit_diagonal=True)
    u, w = sol[..., :dv], sol[..., dv:]
    attn = jnp.einsum('bhnid,bhnjd->bhnij', q, k) * decay
    q_g = q * jnp.exp(gc)[..., None]
    g_last = gc[..., -1]
    k_dec = k * jnp.exp(g_last[..., None] - gc)[..., None]
    xs = (jnp.moveaxis(q_g, 2, 0), jnp.moveaxis(w, 2, 0), jnp.moveaxis(u, 2, 0),
          jnp.moveaxis(attn, 2, 0), jnp.moveaxis(k_dec, 2, 0), jnp.moveaxis(jnp.exp(g_last), 2, 0))

    def step(state, inp):
        qg, wc, uc, ac, kd, gl = inp
        v_new = uc - jnp.einsum('bhcd,bhde->bhce', wc, state)
        o = jnp.einsum('bhcd,bhde->bhce', qg, state) + jnp.einsum('bhij,bhje->bhie', ac, v_new)
        state = state * gl[..., None, None] + jnp.einsum('bhcd,bhce->bhde', kd, v_new)
        return state, o

    s0 = jnp.zeros((b_sz, h_sz, dk, dv), f32)
    _, o = lax.scan(step, s0, xs)
    return jnp.moveaxis(o, 0, 2).reshape(b_sz, h_sz, t_len, dv)


def gated_deltanet_mixer(x, w_in, conv_w, a_log, dt_bias, norm_w, w_out):
    b_sz, t_len, _ = x.shape
    proj = x @ w_in
    qkv = jax.nn.silu(causal_short_conv(proj[..., :DN_QKV_WIDTH], conv_w))
    z = proj[..., DN_QKV_WIDTH:DN_QKV_WIDTH + DN_VAL_WIDTH]
    b_logit = proj[..., DN_QKV_WIDTH + DN_VAL_WIDTH:DN_QKV_WIDTH + DN_VAL_WIDTH + DN_HEADS]
    a_logit = proj[..., DN_QKV_WIDTH + DN_VAL_WIDTH + DN_HEADS:]
    q = l2_normalize(qkv[..., :DN_KEY_WIDTH].reshape(b_sz, t_len, DN_HEADS, DN_HEAD_K))
    k = l2_normalize(qkv[..., DN_KEY_WIDTH:2 * DN_KEY_WIDTH].reshape(b_sz, t_len, DN_HEADS, DN_HEAD_K))
    v = qkv[..., 2 * DN_KEY_WIDTH:].reshape(b_sz, t_len, DN_HEADS, DN_HEAD_V)
    beta = jax.nn.sigmoid(b_logit.astype(jnp.float32))
    g = -jnp.exp(a_log.astype(jnp.float32)) * jax.nn.softplus(
        a_logit.astype(jnp.float32) + dt_bias.astype(jnp.float32))
    o = gated_delta_rule_chunked(q.transpose(0, 2, 1, 3), k.transpose(0, 2, 1, 3), v.transpose(0, 2, 1, 3),
                                 g.transpose(0, 2, 1), beta.transpose(0, 2, 1))
    o = o.transpose(0, 2, 1, 3)
    zf = z.reshape(b_sz, t_len, DN_HEADS, DN_HEAD_V).astype(jnp.float32)
    o = (rms_norm(o, norm_w) * jax.nn.silu(zf)).astype(x.dtype)
    return o.reshape(b_sz, t_len, DN_VAL_WIDTH) @ w_out


def compress_blocks(x, pe, w1, w2):
    b_sz, g_sz, t_len, dh = x.shape
    n_chunk = t_len // CMP_STRIDE
    ratio = CMP_BLOCK // CMP_STRIDE
    n_cmp = n_chunk - ratio + 1
    chunks = x.reshape(b_sz, g_sz, n_chunk, CMP_STRIDE, dh)
    hid = None
    for m in range(ratio):
        part = chunks[:, :, m:m + n_cmp] + pe[m * CMP_STRIDE:(m + 1) * CMP_STRIDE]
        term = jnp.einsum('bgncd,cdh->bgnh', part, w1[m * CMP_STRIDE:(m + 1) * CMP_STRIDE])
        hid = term if hid is None else hid + term
    return jax.nn.silu(hid) @ w2


def nsa_shared_kv(h, w_kv, pe_k, pe_v, w1_k, w2_k, w1_v, w2_v):
    b_sz, t_len, _ = h.shape
    kv = (h @ w_kv).reshape(b_sz, t_len, 2 * N_BRANCH, NSA_KV_GROUPS, NSA_HEAD_DIM).transpose(2, 0, 3, 1, 4)
    pos = jnp.arange(t_len, dtype=jnp.int32)
    k_cmp = compress_blocks(kv[0], pe_k, w1_k, w2_k)
    n_cmp = k_cmp.shape[2]
    cmp_end = jnp.arange(n_cmp, dtype=jnp.int32) * CMP_STRIDE + CMP_BLOCK - 1
    k_cmp = rope(k_cmp, cmp_end)
    v_cmp = compress_blocks(kv[1], pe_v, w1_v, w2_v)
    n_sel = t_len // SEL_BLOCK
    k_sel = rope(kv[2], pos).reshape(b_sz, NSA_KV_GROUPS, n_sel, SEL_BLOCK, NSA_HEAD_DIM)
    v_sel = kv[3].reshape(b_sz, NSA_KV_GROUPS, n_sel, SEL_BLOCK, NSA_HEAD_DIM)
    pad = ((0, 0), (0, 0), (WINDOW, 0), (0, 0))
    k_win = jnp.pad(rope(kv[4], pos), pad)
    v_win = jnp.pad(kv[5], pad)
    return (k_cmp, v_cmp, k_sel, v_sel, k_win, v_win)


def cmp_to_sel_overlap(n_cmp, n_sel):
    start = np.arange(n_cmp)[:, None] * CMP_STRIDE
    bstart = np.arange(n_sel)[None, :] * SEL_BLOCK
    ov = np.clip(np.minimum(start + CMP_BLOCK, bstart + SEL_BLOCK) - np.maximum(start, bstart), 0, None)
    return (ov / CMP_BLOCK).astype(np.float32)


def masked_softmax(s, mask):
    s = jnp.where(mask, s.astype(jnp.float32), -1e30)
    return jnp.where(mask, jax.nn.softmax(s, axis=-1), 0.0)


def nsa_branch_attention(q, k_cmp, v_cmp, k_sel, v_sel, k_win, v_win):
    b_sz, g_sz, r_sz, t_len, dh = q.shape
    n_cmp = k_cmp.shape[2]
    n_sel = k_sel.shape[2]
    top_k = min(SEL_TOPK, n_sel)
    scale = dh ** -0.5
    cmp_end = jnp.arange(n_cmp, dtype=jnp.int32) * CMP_STRIDE + CMP_BLOCK - 1
    overlap = jnp.asarray(cmp_to_sel_overlap(n_cmp, n_sel))
    blk = jnp.arange(n_sel, dtype=jnp.int32)
    gather = jax.vmap(jax.vmap(lambda blocks, idx: blocks[idx]))

    def one_block(qb):
        s0 = qb * Q_BLOCK
        t = s0 + jnp.arange(Q_BLOCK, dtype=jnp.int32)
        qblk = lax.dynamic_slice_in_dim(q, s0, Q_BLOCK, axis=3)
        sc = jnp.einsum('bgrqd,bgnd->bgrqn', qblk, k_cmp) * scale
        p_cmp = masked_softmax(sc, cmp_end[None, :] <= t[:, None])
        o_cmp = jnp.einsum('bgrqn,bgnd->bgrqd', p_cmp, v_cmp)
        imp = jnp.einsum('bgrqn,ns->bgqs', p_cmp, overlap)
        cur = t // SEL_BLOCK
        forced = (blk[None, :] == 0) | (blk[None, :] == cur[:, None]) | (blk[None, :] == cur[:, None] - 1)
        visible = blk[None, :] * SEL_BLOCK <= t[:, None]
        score = jnp.where(visible, jnp.where(forced, FORCED_SCORE, imp), -1.0)
        _, idx = lax.top_k(score, top_k)
        ks = gather(k_sel, idx)
        vs = gather(v_sel, idx)
        ss = jnp.einsum('bgrqd,bgqnkd->bgrqnk', qblk, ks) * scale
        kpos = idx[..., None] * SEL_BLOCK + jnp.arange(SEL_BLOCK, dtype=jnp.int32)
        smask = (kpos <= t[:, None, None])[:, :, None]
        p_sel = masked_softmax(ss.reshape(b_sz, g_sz, r_sz, Q_BLOCK, -1),
                               smask.reshape(b_sz, g_sz, 1, Q_BLOCK, -1))
        o_sel = jnp.einsum('bgrqm,bgqmd->bgrqd', p_sel, vs.reshape(b_sz, g_sz, Q_BLOCK, -1, dh))
        kw = lax.dynamic_slice_in_dim(k_win, s0, WINDOW + Q_BLOCK, axis=2)
        vw = lax.dynamic_slice_in_dim(v_win, s0, WINDOW + Q_BLOCK, axis=2)
        wpos = s0 - WINDOW + jnp.arange(WINDOW + Q_BLOCK, dtype=jnp.int32)
        wmask = (wpos[None, :] <= t[:, None]) & (wpos[None, :] > t[:, None] - WINDOW) & (wpos[None, :] >= 0)
        sw = jnp.einsum('bgrqd,bgkd->bgrqk', qblk, kw) * scale
        p_win = masked_softmax(sw, wmask)
        o_win = jnp.einsum('bgrqk,bgkd->bgrqd', p_win, vw)
        return (o_cmp, o_sel, o_win)

    o_cmp, o_sel, o_win = lax.map(one_block, jnp.arange(t_len // Q_BLOCK, dtype=jnp.int32))

    def to_bthd(o):
        return o.transpose(1, 0, 4, 2, 3, 5).reshape(b_sz, t_len, g_sz * r_sz, dh)

    return (to_bthd(o_cmp), to_bthd(o_sel), to_bthd(o_win))


def nsa_mixer(x, shared, w_in, w_out):
    k_cmp, v_cmp, k_sel, v_sel, k_win, v_win = shared
    b_sz, t_len, _ = x.shape
    proj = x @ w_in
    q = proj[..., :NSA_WIDTH].reshape(b_sz, t_len, NSA_HEADS, NSA_HEAD_DIM).transpose(0, 2, 1, 3)
    q = rope(q, jnp.arange(t_len, dtype=jnp.int32)).reshape(b_sz, NSA_KV_GROUPS, NSA_REP, t_len, NSA_HEAD_DIM)
    z = proj[..., NSA_WIDTH:NSA_WIDTH * (1 + N_BRANCH)].reshape(b_sz, t_len, N_BRANCH, NSA_HEADS, NSA_HEAD_DIM)
    gates = jax.nn.sigmoid(proj[..., NSA_WIDTH * (1 + N_BRANCH):].astype(jnp.float32)).reshape(
        b_sz, t_len, N_BRANCH, NSA_HEADS)
    o_cmp, o_sel, o_win = nsa_branch_attention(q, k_cmp, v_cmp, k_sel, v_sel, k_win, v_win)
    o = jnp.stack([o_cmp, o_sel, o_win], axis=2).astype(jnp.float32)
    o = jnp.sum(gates[..., None] * o * jax.nn.silu(z.astype(jnp.float32)), axis=2).astype(x.dtype)
    return o.reshape(b_sz, t_len, NSA_WIDTH) @ w_out


def deepnorm_residual(x, y, w, b):
    return layer_norm(DEEPNORM_ALPHA * x + y, w, b)


def setup_inputs(seed: int = 0) -> dict:
    key = jax.random.key(seed)
    ks = jax.random.split(key, 24)
    nrm = jax.random.normal
    f32 = jnp.float32
    x = nrm(ks[0], (BATCH, SEQ, D_MODEL), f32)
    a_w_in = nrm(ks[1], (N_A_LAYERS, D_MODEL, DN_IN_WIDTH), f32) * D_MODEL ** -0.5
    a_conv_w = nrm(ks[2], (N_A_LAYERS, DN_CONV, DN_QKV_WIDTH), f32) * DN_CONV ** -0.5
    a_a_log = jnp.log(jax.random.uniform(ks[3], (N_A_LAYERS, DN_HEADS), f32, 1.0, 16.0))
    dt = jnp.exp(jax.random.uniform(ks[4], (N_A_LAYERS, DN_HEADS), f32, math.log(1e-3), math.log(1e-1)))
    a_dt_bias = dt + jnp.log(-jnp.expm1(-dt))
    a_norm_w = 1.0 + 0.02 * nrm(ks[5], (N_A_LAYERS, DN_HEAD_V), f32)
    a_w_out = nrm(ks[6], (N_A_LAYERS, DN_VAL_WIDTH, D_MODEL), f32) * DN_VAL_WIDTH ** -0.5 * DEEPNORM_BETA
    a_ln_w = 1.0 + 0.02 * nrm(ks[7], (N_A_LAYERS, D_MODEL), f32)
    a_ln_b = 0.02 * nrm(ks[8], (N_A_LAYERS, D_MODEL), f32)
    s_w_kv = nrm(ks[9], (D_MODEL, SHARED_KV_WIDTH), f32) * D_MODEL ** -0.5
    s_pe_k = 0.1 * nrm(ks[10], (CMP_BLOCK, NSA_HEAD_DIM), f32)
    s_pe_v = 0.1 * nrm(ks[11], (CMP_BLOCK, NSA_HEAD_DIM), f32)
    s_w1_k = nrm(ks[12], (CMP_BLOCK, NSA_HEAD_DIM, CMP_HIDDEN), f32) * (CMP_BLOCK * NSA_HEAD_DIM) ** -0.5
    s_w2_k = nrm(ks[13], (CMP_HIDDEN, NSA_HEAD_DIM), f32) * CMP_HIDDEN ** -0.5
    s_w1_v = nrm(ks[14], (CMP_BLOCK, NSA_HEAD_DIM, CMP_HIDDEN), f32) * (CMP_BLOCK * NSA_HEAD_DIM) ** -0.5
    s_w2_v = nrm(ks[15], (CMP_HIDDEN, NSA_HEAD_DIM), f32) * CMP_HIDDEN ** -0.5
    b_w_in = nrm(ks[16], (N_B_LAYERS, D_MODEL, NSA_IN_WIDTH), f32) * D_MODEL ** -0.5
    b_w_out = nrm(ks[17], (N_B_LAYERS, NSA_WIDTH, D_MODEL), f32) * NSA_WIDTH ** -0.5 * DEEPNORM_BETA
    b_ln_w = 1.0 + 0.02 * nrm(ks[18], (N_B_LAYERS, D_MODEL), f32)
    b_ln_b = 0.02 * nrm(ks[19], (N_B_LAYERS, D_MODEL), f32)
    return {'x': x, 'a_w_in': a_w_in, 'a_conv_w': a_conv_w, 'a_a_log': a_a_log, 'a_dt_bias': a_dt_bias,
            'a_norm_w': a_norm_w, 'a_w_out': a_w_out, 'a_ln_w': a_ln_w, 'a_ln_b': a_ln_b,
            's_w_kv': s_w_kv, 's_pe_k': s_pe_k, 's_pe_v': s_pe_v, 's_w1_k': s_w1_k, 's_w2_k': s_w2_k,
            's_w1_v': s_w1_v, 's_w2_v': s_w2_v,
            'b_w_in': b_w_in, 'b_w_out': b_w_out, 'b_ln_w': b_ln_w, 'b_ln_b': b_ln_b}


def reference(x, a_w_in, a_conv_w, a_a_log, a_dt_bias, a_norm_w, a_w_out, a_ln_w, a_ln_b,
              s_w_kv, s_pe_k, s_pe_v, s_w1_k, s_w2_k, s_w1_v, s_w2_v,
              b_w_in, b_w_out, b_ln_w, b_ln_b):
    shared = None
    for layer in range(DEPTH):
        if layer < N_A_LAYERS:
            i = layer
            y = gated_deltanet_mixer(x, a_w_in[i], a_conv_w[i], a_a_log[i], a_dt_bias[i], a_norm_w[i], a_w_out[i])
            x = deepnorm_residual(x, y, a_ln_w[i], a_ln_b[i])
            if layer == N_A_LAYERS - 1:
                shared = nsa_shared_kv(x, s_w_kv, s_pe_k, s_pe_v, s_w1_k, s_w2_k, s_w1_v, s_w2_v)
        else:
            j = layer - N_A_LAYERS
            y = nsa_mixer(x, shared, b_w_in[j], b_w_out[j])
            x = deepnorm_residual(x, y, b_ln_w[j], b_ln_b[j])
    return x
```

```python
import functools

import jax
import jax.numpy as jnp
from jax import lax
from jax.experimental import pallas as pl
from jax.experimental.pallas import tpu as pltpu

F32 = jnp.float32
BF16 = jnp.bfloat16

DN_HEADS = 8
DN_HEAD = 128
DN_CONV = 4
NSA_HEADS = 16
NSA_GROUPS = 4
NSA_REP = NSA_HEADS // NSA_GROUPS
NSA_DH = 64
CMP_BLOCK = 32
CMP_STRIDE = 16
CMP_HIDDEN = 128
SEL_BLOCK = 64
SEL_TOPK = 16
WINDOW = 512
Q_BLOCK = 128
FORCED_SCORE = 1.0e4
ROPE_THETA = 10000.0
NORM_EPS = 1e-6
DEPTH = 2
DEEPNORM_ALPHA = (2.0 * DEPTH) ** 0.25

GDN_CHUNK = 128
INV_BASE = 16
LANE = 128
SEL_KV_TILE = 512
NEG_BIG = -1e30
VMEM_LIMIT = 56 * 1024 * 1024


def _cparams(*sem):
    return pltpu.CompilerParams(dimension_semantics=sem, vmem_limit_bytes=VMEM_LIMIT)


def _dot(a, b):
    return jnp.dot(a, b, preferred_element_type=F32)


def _dot_nt(a, b):
    return lax.dot_general(a, b, (((1,), (1,)), ((), ())), preferred_element_type=F32)


def _dot_tn(a, b):
    return lax.dot_general(a, b, (((0,), (0,)), ((), ())), preferred_element_type=F32)


def _split2(a):
    hi = a.astype(BF16)
    lo = (a - hi.astype(F32)).astype(BF16)
    return hi, lo


def _split3(a):
    hi = a.astype(BF16)
    r = a - hi.astype(F32)
    mid = r.astype(BF16)
    lo = (r - mid.astype(F32)).astype(BF16)
    return hi, mid, lo


def _dot3(a, b):
    ah, al = _split2(a)
    bh, bl = _split2(b)
    return _dot(ah, bh) + (_dot(ah, bl) + _dot(al, bh))


def _blk(idx, size):
    return idx >> (size.bit_length() - 1)


def _silu(x):
    return x / (1.0 + jnp.exp(-x))


def _sigmoid(x):
    return 1.0 / (1.0 + jnp.exp(-x))


def _mm_kernel(x_ref, w_ref, o_ref):
    o_ref[...] = _dot(x_ref[...], w_ref[...]).astype(o_ref.dtype)


def _matmul(x, w, out_dtype, tm, tn):
    m, k = x.shape
    n = w.shape[1]
    return pl.pallas_call(
        _mm_kernel,
        grid=(m // tm, n // tn),
        in_specs=[pl.BlockSpec((tm, k), lambda i, j: (i, 0)),
                  pl.BlockSpec((k, tn), lambda i, j: (0, j))],
        out_specs=pl.BlockSpec((tm, tn), lambda i, j: (i, j)),
        out_shape=jax.ShapeDtypeStruct((m, n), out_dtype),
        compiler_params=_cparams("parallel", "parallel"),
        name="proj_matmul",
    )(x, w)


def _layer_norm_rows(r, w, b):
    mu = jnp.mean(r, axis=-1, keepdims=True)
    d = r - mu
    var = jnp.mean(d * d, axis=-1, keepdims=True)
    return d * lax.rsqrt(var + NORM_EPS) * w + b


def _out_ln_kernel(a_ref, w_ref, x_ref, lw_ref, lb_ref, o_ref, ob_ref):
    y = _dot(a_ref[...], w_ref[...])
    r = DEEPNORM_ALPHA * x_ref[...] + y
    out = _layer_norm_rows(r, lw_ref[...], lb_ref[...])
    o_ref[...] = out
    ob_ref[...] = out.astype(BF16)


def _out_proj_ln(a, w, x_res, ln_w, ln_b, tm):
    m, k = a.shape
    n = w.shape[1]
    row = lambda i: (i, 0)
    fixed = lambda i: (0, 0)
    return pl.pallas_call(
        _out_ln_kernel,
        grid=(m // tm,),
        in_specs=[pl.BlockSpec((tm, k), row), pl.BlockSpec((k, n), fixed),
                  pl.BlockSpec((tm, n), row), pl.BlockSpec((1, n), fixed),
                  pl.BlockSpec((1, n), fixed)],
        out_specs=[pl.BlockSpec((tm, n), row), pl.BlockSpec((tm, n), row)],
        out_shape=[jax.ShapeDtypeStruct((m, n), F32), jax.ShapeDtypeStruct((m, n), BF16)],
        compiler_params=_cparams("parallel"),
        name="out_proj_layernorm",
    )(a, w, x_res, ln_w.reshape(1, n), ln_b.reshape(1, n))


def _gdn_prep_kernel(x_ref, h_ref, cw_ref, o_ref, buf_ref, *, tm, t_len):
    i = pl.program_id(0)
    j = pl.program_id(1)
    first = (i * tm) % t_len == 0
    buf_ref[0:8, :] = jnp.where(first, 0.0, h_ref[...])
    buf_ref[8:8 + tm, :] = x_ref[...]
    cw = cw_ref[...]
    base = 8 - (DN_CONV - 1)
    y = buf_ref[base:base + tm, :] * cw[0:1, :]
    for tap in range(1, DN_CONV):
        y = y + buf_ref[base + tap:base + tap + tm, :] * cw[tap:tap + 1, :]
    y = _silu(y)
    qscale = jnp.where(j == 0, DN_HEAD ** -0.5, 1.0)
    for h in range(DN_HEADS):
        sl = slice(h * DN_HEAD, (h + 1) * DN_HEAD)
        yh = y[:, sl]
        ss = jnp.sum(yh * yh, axis=-1, keepdims=True)
        yn = yh * lax.rsqrt(ss + NORM_EPS) * qscale
        o_ref[:, sl] = jnp.where(j < 2, yn, yh)


def _gdn_prep(proj, conv_w, t_len, tm):
    m = proj.shape[0]
    width = DN_HEADS * DN_HEAD
    kern = functools.partial(_gdn_prep_kernel, tm=tm, t_len=t_len)
    return pl.pallas_call(
        kern,
        grid=(m // tm, 3),
        in_specs=[pl.BlockSpec((tm, width), lambda i, j: (i, j)),
                  pl.BlockSpec((8, width), lambda i, j: (jnp.maximum(i * (tm // 8) - 1, 0), j)),
                  pl.BlockSpec((DN_CONV, width), lambda i, j: (0, j))],
        out_specs=pl.BlockSpec((tm, width), lambda i, j: (i, j)),
        out_shape=jax.ShapeDtypeStruct((m, 3 * width), F32),
        scratch_shapes=[pltpu.VMEM((tm + 8, width), F32)],
        compiler_params=_cparams("parallel", "parallel"),
        name="gdn_conv_silu_l2norm",
    )(proj, proj, conv_w)


def _gdn_gate_kernel(t_ref, al_ref, dt_ref, o_ref):
    x = t_ref[...]
    lane = lax.broadcasted_iota(jnp.int32, x.shape, 1)
    beta = _sigmoid(x)
    z = x + dt_ref[...]
    softplus = jnp.maximum(z, 0.0) + jnp.log(1.0 + jnp.exp(-jnp.abs(z)))
    g = -jnp.exp(al_ref[...]) * softplus
    o_ref[...] = jnp.where(lane < DN_HEADS, beta, jnp.where(lane < 2 * DN_HEADS, g, 0.0))


def _gdn_gates(tail, a_log, dt_bias, tm):
    m = tail.shape[0]
    pad = lambda v: jnp.zeros((1, LANE), F32).at[0, DN_HEADS:2 * DN_HEADS].set(v.astype(F32))
    return pl.pallas_call(
        _gdn_gate_kernel,
        grid=(m // tm,),
        in_specs=[pl.BlockSpec((tm, LANE), lambda i: (i, 0)),
                  pl.BlockSpec((1, LANE), lambda i: (0, 0)),
                  pl.BlockSpec((1, LANE), lambda i: (0, 0))],
        out_specs=pl.BlockSpec((tm, LANE), lambda i: (i, 0)),
        out_shape=jax.ShapeDtypeStruct((m, LANE), F32),
        compiler_params=_cparams("parallel"),
        name="gdn_gates",
    )(tail, pad(a_log), pad(dt_bias))


def _inv_unit_lower(low, ii, jj):
    c = low.shape[0]
    eye = (ii == jj).astype(F32)
    d = jnp.where(_blk(ii, INV_BASE) == _blk(jj, INV_BASE), low, 0.0)
    x = eye - d
    p = d
    span = 2
    while span < INV_BASE:
        p = _dot3(p, p)
        x = _dot3(x, eye + p)
        span *= 2
    s = INV_BASE
    while s < c:
        off = (_blk(ii, 2 * s) == _blk(jj, 2 * s)) & (_blk(ii, s) != _blk(jj, s))
        o = jnp.where(off, low, 0.0)
        x = x - _dot3(_dot3(x, o), x)
        s *= 2
    return x


def _gdn_chunk_kernel(q_ref, k_ref, v_ref, z_ref, gc_ref, gr_ref, nw_ref, o_ref, s_ref):
    c = GDN_CHUNK

    @pl.when(pl.program_id(1) == 0)
    def _():
        s_ref[...] = jnp.zeros_like(s_ref)

    ii = lax.broadcasted_iota(jnp.int32, (c, c), 0)
    jj = lax.broadcasted_iota(jnp.int32, (c, c), 1)
    incl = ii >= jj
    strict = ii > jj
    gcol = gc_ref[...]
    grow = gr_ref[0, 0]
    ltri = incl.astype(BF16)
    utri = (ii <= jj).astype(BF16)
    gc3 = _split3(gcol)
    gr3 = _split3(grow)
    gcs_col = _dot(ltri, gc3[0]) + (_dot(ltri, gc3[1]) + _dot(ltri, gc3[2]))
    gcs_row = _dot(gr3[0], utri) + (_dot(gr3[1], utri) + _dot(gr3[2], utri))
    nw = nw_ref[...]

    for h in range(DN_HEADS):
        sl = slice(h * DN_HEAD, (h + 1) * DN_HEAD)
        q = q_ref[:, sl]
        k = k_ref[:, sl]
        v = v_ref[:, sl]
        beta = gcol[:, h:h + 1]
        gcc = gcs_col[:, DN_HEADS + h:DN_HEADS + h + 1]
        gcr = gcs_row[DN_HEADS + h:DN_HEADS + h + 1, :]
        diff = gcc - gcr
        decay = jnp.where(incl, jnp.exp(jnp.where(incl, diff, 0.0)), 0.0)
        kb = k * beta
        k16 = k.astype(BF16)
        low = jnp.where(strict, _dot_nt(kb.astype(BF16), k16) * decay, 0.0)
        tinv = _inv_unit_lower(low, ii, jj)
        eg = jnp.exp(gcc)
        rhs = jnp.concatenate([v * beta, kb * eg], axis=1)
        sol = _dot3(tinv, rhs)
        u = sol[:, :DN_HEAD]
        w = sol[:, DN_HEAD:]
        attn = jnp.where(incl, _dot_nt(q.astype(BF16), k16) * decay, 0.0)
        g_last = gcc[c - 1:c, :]
        kd = k * jnp.exp(g_last - gcc)
        state = s_ref[h]
        s16 = state.astype(BF16)
        v_new = u - _dot(w.astype(BF16), s16)
        vn16 = v_new.astype(BF16)
        o = _dot((q * eg).astype(BF16), s16) + _dot(attn.astype(BF16), vn16)
        s_ref[h] = state * jnp.exp(g_last) + _dot_tn(kd.astype(BF16), vn16)
        on = o * lax.rsqrt(jnp.mean(o * o, axis=-1, keepdims=True) + NORM_EPS) * nw
        o_ref[:, sl] = (on * _silu(z_ref[:, sl])).astype(o_ref.dtype)


def _gdn_chunks(qkv, proj, gates_col, gates_row, norm_w, b_sz, t_len):
    c = GDN_CHUNK
    nc = t_len // c
    width = DN_HEADS * DN_HEAD
    m = qkv.shape[0]
    row = lambda col: (lambda b, i: (b * nc + i, col))
    return pl.pallas_call(
        _gdn_chunk_kernel,
        grid=(b_sz, nc),
        in_specs=[pl.BlockSpec((c, width), row(0)), pl.BlockSpec((c, width), row(1)),
                  pl.BlockSpec((c, width), row(2)), pl.BlockSpec((c, width), row(3)),
                  pl.BlockSpec((c, LANE), row(0)),
                  pl.BlockSpec((1, 1, 2 * DN_HEADS, c), lambda b, i: (b, i, 0, 0)),
                  pl.BlockSpec((1, DN_HEAD), lambda b, i: (0, 0))],
        out_specs=pl.BlockSpec((c, width), row(0)),
        out_shape=jax.ShapeDtypeStruct((m, width), BF16),
        scratch_shapes=[pltpu.VMEM((DN_HEADS, DN_HEAD, DN_HEAD), F32)],
        compiler_params=_cparams("parallel", "arbitrary"),
        name="gdn_delta_rule",
    )(qkv, qkv, qkv, proj, gates_col, gates_row, norm_w.reshape(1, DN_HEAD).astype(F32))


def _rope_tables(pos):
    half = NSA_DH // 2
    inv_freq = ROPE_THETA ** (-jnp.arange(half, dtype=F32) / half)
    ang = pos.astype(F32)[:, None] * inv_freq[None, :]
    cos, sin = jnp.cos(ang), jnp.sin(ang)
    cos_t = jnp.concatenate([cos, cos, cos, cos], axis=-1)
    sin_t = jnp.concatenate([-sin, sin, -sin, sin], axis=-1)
    return cos_t, sin_t


def _rope128(x, cos_t, sin_t):
    lane = lax.broadcasted_iota(jnp.int32, x.shape, 1)
    lower = (lane & (NSA_DH - 1)) < (NSA_DH // 2)
    fwd = pltpu.roll(x, LANE - NSA_DH // 2, axis=1)
    bwd = pltpu.roll(x, NSA_DH // 2, axis=1)
    return x * cos_t + jnp.where(lower, fwd, bwd) * sin_t


def _kv_layout_kernel(kv_ref, cos_ref, sin_ref, ks_ref, vs_ref, kw_ref, vw_ref):
    gw = NSA_GROUPS * NSA_DH
    cos_t = cos_ref[...]
    sin_t = sin_ref[...]
    for pair in range(NSA_GROUPS // 2):
        for src, dst, rot in ((2, ks_ref, True), (3, vs_ref, False), (4, kw_ref, True), (5, vw_ref, False)):
            x = kv_ref[:, src * gw + pair * LANE: src * gw + (pair + 1) * LANE]
            if rot:
                x = _rope128(x, cos_t, sin_t)
            dst[0, 2 * pair] = x[:, :NSA_DH].astype(BF16)
            dst[0, 2 * pair + 1] = x[:, NSA_DH:].astype(BF16)


def _kv_layout(kv, cos_t, sin_t, b_sz, t_len, tm):
    nt = t_len // tm
    out = jax.ShapeDtypeStruct((b_sz, NSA_GROUPS, t_len, NSA_DH), BF16)
    ospec = pl.BlockSpec((1, NSA_GROUPS, tm, NSA_DH), lambda i: (i // nt, 0, i % nt, 0))
    return pl.pallas_call(
        _kv_layout_kernel,
        grid=(b_sz * nt,),
        in_specs=[pl.BlockSpec((tm, kv.shape[1]), lambda i: (i, 0)),
                  pl.BlockSpec((tm, LANE), lambda i: (i % nt, 0)),
                  pl.BlockSpec((tm, LANE), lambda i: (i % nt, 0))],
        out_specs=[ospec, ospec, ospec, ospec],
        out_shape=[out, out, out, out],
        compiler_params=_cparams("parallel"),
        name="kv_rope_layout",
    )(kv, cos_t, sin_t)


def _q_layout_kernel(q_ref, cos_ref, sin_ref, o_ref):
    cos_t = cos_ref[...]
    sin_t = sin_ref[...]
    scale = NSA_DH ** -0.5
    for pair in range(NSA_HEADS // 2):
        x = _rope128(q_ref[:, pair * LANE:(pair + 1) * LANE], cos_t, sin_t) * scale
        o_ref[0, 2 * pair] = x[:, :NSA_DH].astype(BF16)
        o_ref[0, 2 * pair + 1] = x[:, NSA_DH:].astype(BF16)


def _q_layout(proj, cos_t, sin_t, b_sz, t_len, tm):
    nt = t_len // tm
    width = NSA_HEADS * NSA_DH
    return pl.pallas_call(
        _q_layout_kernel,
        grid=(b_sz * nt,),
        in_specs=[pl.BlockSpec((tm, width), lambda i: (i, 0)),
                  pl.BlockSpec((tm, LANE), lambda i: (i % nt, 0)),
                  pl.BlockSpec((tm, LANE), lambda i: (i % nt, 0))],
        out_specs=pl.BlockSpec((1, NSA_HEADS, tm, NSA_DH), lambda i: (i // nt, 0, i % nt, 0)),
        out_shape=jax.ShapeDtypeStruct((b_sz, NSA_HEADS, t_len, NSA_DH), BF16),
        compiler_params=_cparams("parallel"),
        name="q_rope_layout",
    )(proj, cos_t, sin_t)


def _compress_kernel(x_ref, pe_ref, w1_ref, w2_ref, cos_ref, sin_ref, o_ref):
    is_key = pl.program_id(0) == 0
    x = x_ref[0, 0, 0]
    n = x.shape[0]
    pe = pe_ref[0]
    a = _dot((x + pe[0:1, :]).astype(BF16), w1_ref[0, 0])
    b = _dot((x + pe[1:2, :]).astype(BF16), w1_ref[0, 1])
    hid = a + pltpu.roll(b, n - 1, axis=0)
    y = _dot(_silu(hid).astype(BF16), w2_ref[0])
    y = jnp.where(is_key, _rope128(y, cos_ref[...], sin_ref[...]), y)
    o_ref[0, 0, 0] = y[:, :NSA_DH].astype(BF16)


def _compress(xg, pe, w1, w2, cos_c, sin_c):
    _, b_sz, g_sz, n_chunk, flat = xg.shape
    return pl.pallas_call(
        _compress_kernel,
        grid=(2, b_sz, g_sz),
        in_specs=[pl.BlockSpec((1, 1, 1, n_chunk, flat), lambda s, b, g: (s, b, g, 0, 0)),
                  pl.BlockSpec((1, 2, flat), lambda s, b, g: (s, 0, 0)),
                  pl.BlockSpec((1, 2, flat, CMP_HIDDEN), lambda s, b, g: (s, 0, 0, 0)),
                  pl.BlockSpec((1, CMP_HIDDEN, LANE), lambda s, b, g: (s, 0, 0)),
                  pl.BlockSpec((n_chunk, LANE), lambda s, b, g: (0, 0)),
                  pl.BlockSpec((n_chunk, LANE), lambda s, b, g: (0, 0))],
        out_specs=pl.BlockSpec((1, 1, 1, n_chunk, NSA_DH), lambda s, b, g: (s, b, g, 0, 0)),
        out_shape=jax.ShapeDtypeStruct((2, b_sz, g_sz, n_chunk, NSA_DH), BF16),
        compiler_params=_cparams("parallel", "parallel", "parallel"),
        name="compress_blocks",
    )(xg, pe, w1, w2, cos_c, sin_c)


def _cmp_kernel(q_ref, k_ref, v_ref, ov_ref, o_ref, sel_ref):
    qb = pl.program_id(2)
    tq = Q_BLOCK
    n_cmp = k_ref.shape[2]
    n_sel = ov_ref.shape[1]
    q = q_ref[0].reshape(NSA_REP * tq, NSA_DH)
    s = _dot_nt(q, k_ref[0, 0])
    row = lax.broadcasted_iota(jnp.int32, (NSA_REP * tq, n_cmp), 0)
    col = lax.broadcasted_iota(jnp.int32, (NSA_REP * tq, n_cmp), 1)
    t_row = qb * tq + (row & (tq - 1))
    valid = col * CMP_STRIDE + (CMP_BLOCK - 1) <= t_row
    s = jnp.where(valid, s, NEG_BIG)
    m = jnp.max(s, axis=-1, keepdims=True)
    e = jnp.where(valid, jnp.exp(s - m), 0.0)
    l = jnp.sum(e, axis=-1, keepdims=True)
    p = e / jnp.maximum(l, 1e-30)
    o = _dot(p.astype(BF16), v_ref[0, 0])
    o_ref[0] = o.reshape(NSA_REP, tq, NSA_DH).astype(o_ref.dtype)

    psum = p[0:tq]
    for r in range(1, NSA_REP):
        psum = psum + p[r * tq:(r + 1) * tq]
    ph, plo = _split2(psum)
    ov = ov_ref[...]
    imp = _dot(ph, ov) + _dot(plo, ov)

    t_q = qb * tq + lax.broadcasted_iota(jnp.int32, (tq, n_sel), 0)
    blk = lax.broadcasted_iota(jnp.int32, (tq, n_sel), 1)
    blk_f = blk.astype(F32)
    cur = _blk(t_q, SEL_BLOCK)
    forced = (blk == 0) | (blk == cur) | (blk == cur - 1)
    visible = blk * SEL_BLOCK <= t_q
    score = jnp.where(visible, jnp.where(forced, FORCED_SCORE, imp), -1.0)
    sel = jnp.zeros((tq, n_sel), F32)
    for _ in range(min(SEL_TOPK, n_sel)):
        mx = jnp.max(score, axis=-1, keepdims=True)
        first = jnp.min(jnp.where(score == mx, blk_f, float(n_sel)), axis=-1, keepdims=True)
        pick = blk_f == first
        sel = jnp.where(pick, 1.0, sel)
        score = jnp.where(pick, -3.0e38, score)
    sel_ref[0, 0] = sel.astype(sel_ref.dtype)


def _cmp_attention(q, k_cmp, v_cmp, overlap, t_len):
    b_sz = q.shape[0]
    n_cmp = k_cmp.shape[2]
    n_sel = overlap.shape[1]
    nq = t_len // Q_BLOCK
    return pl.pallas_call(
        _cmp_kernel,
        grid=(b_sz, NSA_GROUPS, nq),
        in_specs=[pl.BlockSpec((1, NSA_REP, Q_BLOCK, NSA_DH), lambda b, g, i: (b, g, i, 0)),
                  pl.BlockSpec((1, 1, n_cmp, NSA_DH), lambda b, g, i: (b, g, 0, 0)),
                  pl.BlockSpec((1, 1, n_cmp, NSA_DH), lambda b, g, i: (b, g, 0, 0)),
                  pl.BlockSpec((n_cmp, n_sel), lambda b, g, i: (0, 0))],
        out_specs=[pl.BlockSpec((1, NSA_REP, Q_BLOCK, NSA_DH), lambda b, g, i: (b, g, i, 0)),
                   pl.BlockSpec((1, 1, Q_BLOCK, n_sel), lambda b, g, i: (b, g, i, 0))],
        out_shape=[jax.ShapeDtypeStruct((b_sz, NSA_HEADS, t_len, NSA_DH), BF16),
                   jax.ShapeDtypeStruct((b_sz, NSA_GROUPS, t_len, n_sel), BF16)],
        compiler_params=_cparams("parallel", "parallel", "parallel"),
        name="cmp_attention_topk",
    )(q, k_cmp, v_cmp, overlap)


def _sel_kernel(q_ref, k_ref, v_ref, sel_ref, ex_ref, o_ref, m_ref, l_ref, acc_ref):
    qb = pl.program_id(2)
    tq = Q_BLOCK
    tk = SEL_KV_TILE
    rows = NSA_REP * tq
    q = q_ref[0].reshape(rows, NSA_DH)
    selm = sel_ref[0, 0]
    m_ref[...] = jnp.full(m_ref.shape, -jnp.inf, F32)
    l_ref[...] = jnp.zeros(l_ref.shape, F32)
    acc_ref[...] = jnp.zeros(acc_ref.shape, F32)
    t_q = qb * tq + lax.broadcasted_iota(jnp.int32, (tq, tk), 0)
    lane = lax.broadcasted_iota(jnp.int32, (tq, tk), 1)

    def body(j, carry):
        k0 = pl.multiple_of(j * tk, tk)
        kt = k_ref[0, 0, pl.ds(k0, tk), :]
        vt = v_ref[0, 0, pl.ds(k0, tk), :]
        chosen = _dot(selm, ex_ref[:, pl.ds(k0, tk)])
        ok = (chosen > 0.5) & (k0 + lane <= t_q)
        bias = jnp.where(ok, 0.0, NEG_BIG)
        s = _dot_nt(q, kt).reshape(NSA_REP, tq, tk) + bias[None]
        s = s.reshape(rows, tk)
        m_prev = m_ref[...]
        m_new = jnp.maximum(m_prev, jnp.max(s, axis=-1, keepdims=True))
        alpha = jnp.exp(m_prev - m_new)
        p = jnp.exp(s - m_new)
        l_ref[...] = alpha * l_ref[...] + jnp.sum(p, axis=-1, keepdims=True)
        acc_ref[...] = alpha * acc_ref[...] + _dot(p.astype(BF16), vt)
        m_ref[...] = m_new
        return carry

    lax.fori_loop(0, (qb * tq) // tk + 1, body, 0)
    o = acc_ref[...] / l_ref[...]
    o_ref[0] = o.reshape(NSA_REP, tq, NSA_DH).astype(o_ref.dtype)


def _sel_attention(q, k_sel, v_sel, sel_mask, expand, t_len):
    b_sz = q.shape[0]
    n_sel = sel_mask.shape[-1]
    nq = t_len // Q_BLOCK
    rows = NSA_REP * Q_BLOCK
    return pl.pallas_call(
        _sel_kernel,
        grid=(b_sz, NSA_GROUPS, nq),
        in_specs=[pl.BlockSpec((1, NSA_REP, Q_BLOCK, NSA_DH), lambda b, g, i: (b, g, i, 0)),
                  pl.BlockSpec((1, 1, t_len, NSA_DH), lambda b, g, i: (b, g, 0, 0)),
                  pl.BlockSpec((1, 1, t_len, NSA_DH), lambda b, g, i: (b, g, 0, 0)),
                  pl.BlockSpec((1, 1, Q_BLOCK, n_sel), lambda b, g, i: (b, g, i, 0)),
                  pl.BlockSpec((n_sel, t_len), lambda b, g, i: (0, 0))],
        out_specs=pl.BlockSpec((1, NSA_REP, Q_BLOCK, NSA_DH), lambda b, g, i: (b, g, i, 0)),
        out_shape=jax.ShapeDtypeStruct((b_sz, NSA_HEADS, t_len, NSA_DH), BF16),
        scratch_shapes=[pltpu.VMEM((rows, 1), F32), pltpu.VMEM((rows, 1), F32),
                        pltpu.VMEM((rows, NSA_DH), F32)],
        compiler_params=_cparams("parallel", "parallel", "arbitrary"),
        name="sel_attention",
    )(q, k_sel, v_sel, sel_mask, expand)


def _win_kernel(q_ref, k_ref, v_ref, o_ref):
    qb = pl.program_id(2)
    tq = Q_BLOCK
    span = WINDOW + tq
    rows = NSA_REP * tq
    q = q_ref[0].reshape(rows, NSA_DH)
    start = pl.multiple_of(jnp.maximum(qb * tq - WINDOW, 0), tq)
    kt = k_ref[0, 0, pl.ds(start, span), :]
    vt = v_ref[0, 0, pl.ds(start, span), :]
    t_q = qb * tq + lax.broadcasted_iota(jnp.int32, (tq, span), 0)
    kpos = start + lax.broadcasted_iota(jnp.int32, (tq, span), 1)
    ok = (kpos <= t_q) & (kpos > t_q - WINDOW)
    bias = jnp.where(ok, 0.0, NEG_BIG)
    s = (_dot_nt(q, kt).reshape(NSA_REP, tq, span) + bias[None]).reshape(rows, span)
    m = jnp.max(s, axis=-1, keepdims=True)
    p = jnp.exp(s - m)
    l = jnp.sum(p, axis=-1, keepdims=True)
    o = _dot(p.astype(BF16), vt) / l
    o_ref[0] = o.reshape(NSA_REP, tq, NSA_DH).astype(o_ref.dtype)


def _win_attention(q, k_win, v_win, t_len):
    b_sz = q.shape[0]
    nq = t_len // Q_BLOCK
    return pl.pallas_call(
        _win_kernel,
        grid=(b_sz, NSA_GROUPS, nq),
        in_specs=[pl.BlockSpec((1, NSA_REP, Q_BLOCK, NSA_DH), lambda b, g, i: (b, g, i, 0)),
                  pl.BlockSpec((1, 1, t_len, NSA_DH), lambda b, g, i: (b, g, 0, 0)),
                  pl.BlockSpec((1, 1, t_len, NSA_DH), lambda b, g, i: (b, g, 0, 0))],
        out_specs=pl.BlockSpec((1, NSA_REP, Q_BLOCK, NSA_DH), lambda b, g, i: (b, g, i, 0)),
        out_shape=jax.ShapeDtypeStruct((b_sz, NSA_HEADS, t_len, NSA_DH), BF16),
        compiler_params=_cparams("parallel", "parallel", "parallel"),
        name="win_attention",
    )(q, k_win, v_win)


def _merge_kernel(oc_ref, os_ref, ow_ref, zc_ref, zs_ref, zw_ref, gt_ref, ge_ref,
                  w_ref, x_ref, lw_ref, lb_ref, o_ref):
    gates = _sigmoid(gt_ref[...])
    gh, gl = _split2(gates)
    acc = None
    for n, (o_r, z_r) in enumerate(((oc_ref, zc_ref), (os_ref, zs_ref), (ow_ref, zw_ref))):
        ex = ge_ref[n]
        gate = _dot(gh, ex) + _dot(gl, ex)
        o = jnp.concatenate([o_r[0, h] for h in range(NSA_HEADS)], axis=-1).astype(F32)
        term = gate * o * _silu(z_r[...])
        acc = term if acc is None else acc + term
    y = _dot(acc.astype(BF16), w_ref[...])
    r = DEEPNORM_ALPHA * x_ref[...] + y
    o_ref[...] = _layer_norm_rows(r, lw_ref[...], lb_ref[...])


def _merge_out(o_cmp, o_sel, o_win, proj, tail, gate_expand, w_out, x_res, ln_w, ln_b, t_len, tm):
    m, d = x_res.shape
    width = NSA_HEADS * NSA_DH
    nt = t_len // tm
    ospec = pl.BlockSpec((1, NSA_HEADS, tm, NSA_DH), lambda i: (i // nt, 0, i % nt, 0))
    zspec = lambda col: pl.BlockSpec((tm, width), lambda i: (i, col))
    fixed2 = lambda i: (0, 0)
    return pl.pallas_call(
        _merge_kernel,
        grid=(m // tm,),
        in_specs=[ospec, ospec, ospec, zspec(1), zspec(2), zspec(3),
                  pl.BlockSpec((tm, LANE), lambda i: (i, 0)),
                  pl.BlockSpec((3, LANE, width), lambda i: (0, 0, 0)),
                  pl.BlockSpec((width, d), fixed2),
                  pl.BlockSpec((tm, d), lambda i: (i, 0)),
                  pl.BlockSpec((1, d), fixed2), pl.BlockSpec((1, d), fixed2)],
        out_specs=pl.BlockSpec((tm, d), lambda i: (i, 0)),
        out_shape=jax.ShapeDtypeStruct((m, d), F32),
        compiler_params=_cparams("parallel"),
        name="merge_out_proj_layernorm",
    )(o_cmp, o_sel, o_win, proj, proj, proj, tail, gate_expand, w_out, x_res,
      ln_w.reshape(1, d), ln_b.reshape(1, d))


def _pad_cols(w, n):
    return jnp.pad(w, ((0, 0), (0, n - w.shape[1])))


def kernel(x, a_w_in, a_conv_w, a_a_log, a_dt_bias, a_norm_w, a_w_out, a_ln_w, a_ln_b,
           s_w_kv, s_pe_k, s_pe_v, s_w1_k, s_w2_k, s_w1_v, s_w2_v,
           b_w_in, b_w_out, b_ln_w, b_ln_b):
    b_sz, t_len, d = x.shape
    m = b_sz * t_len
    assert t_len % SEL_KV_TILE == 0 and t_len >= WINDOW + Q_BLOCK and d % LANE == 0
    assert a_w_in.shape[0] == 1 and b_w_in.shape[0] == 1
    tm_mm = min(1024, m)
    tm_row = min(512, t_len)

    x2 = x.reshape(m, d)
    x16 = x2.astype(BF16)

    dn_main = 4 * DN_HEADS * DN_HEAD
    w_in = a_w_in[0]
    proj = _matmul(x16, w_in[:, :dn_main].astype(BF16), F32, tm_mm, 512)
    tail = _matmul(x16, _pad_cols(w_in[:, dn_main:], LANE).astype(BF16), F32, tm_mm, LANE)
    qkv = _gdn_prep(proj, a_conv_w[0], t_len, tm_row)
    gates_col = _gdn_gates(tail, a_a_log[0], a_dt_bias[0], tm_mm)
    nc = t_len // GDN_CHUNK
    gates_row = gates_col[:, :2 * DN_HEADS].reshape(b_sz, nc, GDN_CHUNK, 2 * DN_HEADS).transpose(0, 1, 3, 2)
    o_gdn = _gdn_chunks(qkv, proj, gates_col, gates_row, a_norm_w[0], b_sz, t_len)
    h1, h1_16 = _out_proj_ln(o_gdn, a_w_out[0].astype(BF16), x2, a_ln_w[0], a_ln_b[0], tm_row)

    kv = _matmul(h1_16, s_w_kv.astype(BF16), F32, tm_mm, 512)
    pos = jnp.arange(t_len, dtype=jnp.int32)
    cos_t, sin_t = _rope_tables(pos)
    k_sel, v_sel, k_win, v_win = _kv_layout(kv, cos_t, sin_t, b_sz, t_len, tm_row)

    n_chunk = t_len // CMP_STRIDE
    gw = NSA_GROUPS * NSA_DH
    flat = CMP_STRIDE * NSA_DH
    xg = kv[:, :2 * gw].reshape(b_sz, n_chunk, CMP_STRIDE, 2, NSA_GROUPS, NSA_DH)
    xg = xg.transpose(3, 0, 4, 1, 2, 5).reshape(2, b_sz, NSA_GROUPS, n_chunk, flat)
    pe = jnp.stack([s_pe_k, s_pe_v]).reshape(2, 2, flat).astype(F32)
    w1 = jnp.stack([s_w1_k, s_w1_v]).reshape(2, 2, flat, CMP_HIDDEN).astype(BF16)
    w2 = jnp.stack([_pad_cols(s_w2_k, LANE), _pad_cols(s_w2_v, LANE)]).astype(BF16)
    cmp_end = jnp.arange(n_chunk, dtype=jnp.int32) * CMP_STRIDE + CMP_BLOCK - 1
    cos_c, sin_c = _rope_tables(cmp_end)
    kv_cmp = _compress(xg, pe, w1, w2, cos_c, sin_c)
    k_cmp, v_cmp = kv_cmp[0], kv_cmp[1]

    nsa_main = 4 * NSA_HEADS * NSA_DH
    w_in_b = b_w_in[0]
    proj_b = _matmul(h1_16, w_in_b[:, :nsa_main].astype(BF16), F32, tm_mm, 512)
    tail_b = _matmul(h1_16, _pad_cols(w_in_b[:, nsa_main:], LANE).astype(BF16), F32, tm_mm, LANE)
    q = _q_layout(proj_b, cos_t, sin_t, b_sz, t_len, tm_row)

    n_sel = t_len // SEL_BLOCK
    c_start = jnp.arange(n_chunk, dtype=jnp.int32)[:, None] * CMP_STRIDE
    s_start = jnp.arange(n_sel, dtype=jnp.int32)[None, :] * SEL_BLOCK
    ov = jnp.clip(jnp.minimum(c_start + CMP_BLOCK, s_start + SEL_BLOCK) - jnp.maximum(c_start, s_start), 0, None)
    overlap = (ov.astype(F32) / CMP_BLOCK).astype(BF16)
    expand = (jnp.arange(t_len, dtype=jnp.int32)[None, :] // SEL_BLOCK
              == jnp.arange(n_sel, dtype=jnp.int32)[:, None]).astype(BF16)

    o_cmp, sel_mask = _cmp_attention(q, k_cmp, v_cmp, overlap, t_len)
    o_sel = _sel_attention(q, k_sel, v_sel, sel_mask, expand, t_len)
    o_win = _win_attention(q, k_win, v_win, t_len)

    lane_id = jnp.arange(LANE, dtype=jnp.int32)[None, :, None]
    col_head = (jnp.arange(NSA_HEADS * NSA_DH, dtype=jnp.int32) // NSA_DH)[None, None, :]
    branch = jnp.arange(3, dtype=jnp.int32)[:, None, None]
    gate_expand = (lane_id == branch * NSA_HEADS + col_head).astype(BF16)
    out = _merge_out(o_cmp, o_sel, o_win, proj_b, tail_b, gate_expand, b_w_out[0].astype(BF16),
                     h1, b_ln_w[0], b_ln_b[0], t_len, min(256, t_len))
    return out.reshape(b_sz, t_len, d)
```

```python
import functools

import jax
import jax.numpy as jnp
from jax import lax
from jax.experimental import pallas as pl
from jax.experimental.pallas import tpu as pltpu

F32 = jnp.float32
BF16 = jnp.bfloat16

DN_HEADS = 8
DN_HEAD = 128
DN_CONV = 4
NSA_HEADS = 16
NSA_GROUPS = 4
NSA_REP = NSA_HEADS // NSA_GROUPS
NSA_DH = 64
CMP_BLOCK = 32
CMP_STRIDE = 16
CMP_HIDDEN = 128
SEL_BLOCK = 64
SEL_TOPK = 16
WINDOW = 512
Q_BLOCK = 128
FORCED_SCORE = 1.0e4
ROPE_THETA = 10000.0
NORM_EPS = 1e-6
DEPTH = 2
DEEPNORM_ALPHA = (2.0 * DEPTH) ** 0.25

GDN_CHUNK = 128
INV_BASE = 16
LANE = 128
SEL_KV_TILE = 512
NEG_BIG = -1e30
LOG2_E = 1.4426950408889634
VMEM_LIMIT = 56 * 1024 * 1024


def _cparams(*sem):
    return pltpu.CompilerParams(dimension_semantics=sem, vmem_limit_bytes=VMEM_LIMIT)


def _dot(a, b):
    return jnp.dot(a, b, preferred_element_type=F32)


def _dot_nt(a, b):
    return lax.dot_general(a, b, (((1,), (1,)), ((), ())), preferred_element_type=F32)


def _dot_tn(a, b):
    return lax.dot_general(a, b, (((0,), (0,)), ((), ())), preferred_element_type=F32)


def _split2(a):
    hi = a.astype(BF16)
    lo = (a - hi.astype(F32)).astype(BF16)
    return hi, lo


def _split3(a):
    hi = a.astype(BF16)
    r = a - hi.astype(F32)
    mid = r.astype(BF16)
    lo = (r - mid.astype(F32)).astype(BF16)
    return hi, mid, lo


def _dot3(a, b):
    ah, al = _split2(a)
    bh, bl = _split2(b)
    return _dot(ah, bh) + (_dot(ah, bl) + _dot(al, bh))


def _blk(idx, size):
    return idx >> (size.bit_length() - 1)


def _silu(x):
    return x / (1.0 + jnp.exp(-x))


def _sigmoid(x):
    return 1.0 / (1.0 + jnp.exp(-x))


def _mm_kernel(x_ref, w_ref, o_ref):
    o_ref[...] = _dot(x_ref[...], w_ref[...]).astype(o_ref.dtype)


def _matmul(x, w, out_dtype, tm, tn):
    m, k = x.shape
    n = w.shape[1]
    return pl.pallas_call(
        _mm_kernel,
        grid=(m // tm, n // tn),
        in_specs=[pl.BlockSpec((tm, k), lambda i, j: (i, 0)),
                  pl.BlockSpec((k, tn), lambda i, j: (0, j))],
        out_specs=pl.BlockSpec((tm, tn), lambda i, j: (i, j)),
        out_shape=jax.ShapeDtypeStruct((m, n), out_dtype),
        compiler_params=_cparams("parallel", "parallel"),
        name="proj_matmul",
    )(x, w)


def _layer_norm_rows(r, w, b):
    mu = jnp.mean(r, axis=-1, keepdims=True)
    d = r - mu
    var = jnp.mean(d * d, axis=-1, keepdims=True)
    return d * lax.rsqrt(var + NORM_EPS) * w + b


def _out_ln_kernel(a_ref, w_ref, x_ref, lw_ref, lb_ref, o_ref, ob_ref):
    y = _dot(a_ref[...], w_ref[...])
    r = DEEPNORM_ALPHA * x_ref[...] + y
    out = _layer_norm_rows(r, lw_ref[...], lb_ref[...])
    o_ref[...] = out
    ob_ref[...] = out.astype(BF16)


def _out_proj_ln(a, w, x_res, ln_w, ln_b, tm):
    m, k = a.shape
    n = w.shape[1]
    row = lambda i: (i, 0)
    fixed = lambda i: (0, 0)
    return pl.pallas_call(
        _out_ln_kernel,
        grid=(m // tm,),
        in_specs=[pl.BlockSpec((tm, k), row), pl.BlockSpec((k, n), fixed),
                  pl.BlockSpec((tm, n), row), pl.BlockSpec((1, n), fixed),
                  pl.BlockSpec((1, n), fixed)],
        out_specs=[pl.BlockSpec((tm, n), row), pl.BlockSpec((tm, n), row)],
        out_shape=[jax.ShapeDtypeStruct((m, n), F32), jax.ShapeDtypeStruct((m, n), BF16)],
        compiler_params=_cparams("parallel"),
        name="out_proj_layernorm",
    )(a, w, x_res, ln_w.reshape(1, n), ln_b.reshape(1, n))


def _gdn_prep_kernel(x_ref, h_ref, cw_ref, o_ref, buf_ref, *, tm, t_len):
    i = pl.program_id(0)
    j = pl.program_id(1)
    first = (i * tm) % t_len == 0
    buf_ref[0:8, :] = jnp.where(first, 0.0, h_ref[...])
    buf_ref[8:8 + tm, :] = x_ref[...]
    cw = cw_ref[...]
    base = 8 - (DN_CONV - 1)
    y = buf_ref[base:base + tm, :] * cw[0:1, :]
    for tap in range(1, DN_CONV):
        y = y + buf_ref[base + tap:base + tap + tm, :] * cw[tap:tap + 1, :]
    y = _silu(y)
    qscale = jnp.where(j == 0, DN_HEAD ** -0.5, 1.0)
    for h in range(DN_HEADS):
        sl = slice(h * DN_HEAD, (h + 1) * DN_HEAD)
        yh = y[:, sl]
        ss = jnp.sum(yh * yh, axis=-1, keepdims=True)
        yn = yh * lax.rsqrt(ss + NORM_EPS) * qscale
        o_ref[:, sl] = jnp.where(j < 2, yn, yh)


def _gdn_prep(proj, conv_w, t_len, tm):
    m = proj.shape[0]
    width = DN_HEADS * DN_HEAD
    kern = functools.partial(_gdn_prep_kernel, tm=tm, t_len=t_len)
    return pl.pallas_call(
        kern,
        grid=(m // tm, 3),
        in_specs=[pl.BlockSpec((tm, width), lambda i, j: (i, j)),
                  pl.BlockSpec((8, width), lambda i, j: (jnp.maximum(i * (tm // 8) - 1, 0), j)),
                  pl.BlockSpec((DN_CONV, width), lambda i, j: (0, j))],
        out_specs=pl.BlockSpec((tm, width), lambda i, j: (i, j)),
        out_shape=jax.ShapeDtypeStruct((m, 3 * width), F32),
        scratch_shapes=[pltpu.VMEM((tm + 8, width), F32)],
        compiler_params=_cparams("parallel", "parallel"),
        name="gdn_conv_silu_l2norm",
    )(proj, proj, conv_w)


def _gdn_gate_kernel(t_ref, al_ref, dt_ref, o_ref):
    x = t_ref[...]
    lane = lax.broadcasted_iota(jnp.int32, x.shape, 1)
    beta = _sigmoid(x)
    z = x + dt_ref[...]
    softplus = jnp.maximum(z, 0.0) + jnp.log(1.0 + jnp.exp(-jnp.abs(z)))
    g = -jnp.exp(al_ref[...]) * softplus
    o_ref[...] = jnp.where(lane < DN_HEADS, beta, jnp.where(lane < 2 * DN_HEADS, g, 0.0))


def _gdn_gates(tail, a_log, dt_bias, tm):
    m = tail.shape[0]
    pad = lambda v: jnp.zeros((1, LANE), F32).at[0, DN_HEADS:2 * DN_HEADS].set(v.astype(F32))
    return pl.pallas_call(
        _gdn_gate_kernel,
        grid=(m // tm,),
        in_specs=[pl.BlockSpec((tm, LANE), lambda i: (i, 0)),
                  pl.BlockSpec((1, LANE), lambda i: (0, 0)),
                  pl.BlockSpec((1, LANE), lambda i: (0, 0))],
        out_specs=pl.BlockSpec((tm, LANE), lambda i: (i, 0)),
        out_shape=jax.ShapeDtypeStruct((m, LANE), F32),
        compiler_params=_cparams("parallel"),
        name="gdn_gates",
    )(tail, pad(a_log), pad(dt_bias))


def _inv_unit_lower(lows, ii, jj):
    c = lows[0].shape[0]
    eye = (ii == jj).astype(F32)
    same = _blk(ii, INV_BASE) == _blk(jj, INV_BASE)
    ds = [jnp.where(same, low, 0.0) for low in lows]
    xs = [eye - d for d in ds]
    ps = ds
    span = 2
    while span < INV_BASE:
        ps = [_dot3(p, p) for p in ps]
        xs = [_dot3(x, eye + p) for x, p in zip(xs, ps)]
        span *= 2
    s = INV_BASE
    while s < c:
        off = (_blk(ii, 2 * s) == _blk(jj, 2 * s)) & (_blk(ii, s) != _blk(jj, s))
        ts = [_dot3(x, jnp.where(off, low, 0.0)) for x, low in zip(xs, lows)]
        xs = [x - _dot3(t, x) for x, t in zip(xs, ts)]
        s *= 2
    return xs


def _gdn_chunk_kernel(q_ref, k_ref, v_ref, z_ref, gc_ref, gr_ref, nw_ref, o_ref, s_ref):
    c = GDN_CHUNK

    @pl.when(pl.program_id(1) == 0)
    def _():
        s_ref[...] = jnp.zeros_like(s_ref)

    ii = lax.broadcasted_iota(jnp.int32, (c, c), 0)
    jj = lax.broadcasted_iota(jnp.int32, (c, c), 1)
    incl = ii >= jj
    strict = ii > jj
    gcol = gc_ref[...]
    grow = gr_ref[0, 0]
    ltri = incl.astype(BF16)
    utri = (ii <= jj).astype(BF16)
    gc3 = _split3(gcol)
    gr3 = _split3(grow)
    gcs_col = _dot(ltri, gc3[0]) + (_dot(ltri, gc3[1]) + _dot(ltri, gc3[2]))
    gcs_row = _dot(gr3[0], utri) + (_dot(gr3[1], utri) + _dot(gr3[2], utri))
    nw = nw_ref[...]

    heads = range(DN_HEADS)
    sls = [slice(h * DN_HEAD, (h + 1) * DN_HEAD) for h in heads]
    betas = [gcol[:, h:h + 1] for h in heads]
    gccs = [gcs_col[:, DN_HEADS + h:DN_HEADS + h + 1] for h in heads]
    gcrs = [gcs_row[DN_HEADS + h:DN_HEADS + h + 1, :] for h in heads]
    decays = [jnp.where(incl, jnp.exp(jnp.where(incl, gcc - gcr, 0.0)), 0.0)
              for gcc, gcr in zip(gccs, gcrs)]
    kbs = [k_ref[:, sl] * beta for sl, beta in zip(sls, betas)]
    k16s = [k_ref[:, sl].astype(BF16) for sl in sls]
    lows = [jnp.where(strict, _dot_nt(kb.astype(BF16), k16) * decay, 0.0)
            for kb, k16, decay in zip(kbs, k16s, decays)]
    attns = [jnp.where(incl, _dot_nt(q_ref[:, sl].astype(BF16), k16) * decay, 0.0).astype(BF16)
             for sl, k16, decay in zip(sls, k16s, decays)]
    tinvs = _inv_unit_lower(lows, ii, jj)
    egs = [jnp.exp(gcc) for gcc in gccs]
    sols = [_dot3(tinv, jnp.concatenate([v_ref[:, sl] * beta, kb * eg], axis=1))
            for tinv, sl, beta, kb, eg in zip(tinvs, sls, betas, kbs, egs)]
    g_lasts = [gcc[c - 1:c, :] for gcc in gccs]
    kds = [(k_ref[:, sl] * jnp.exp(g_last - gcc)).astype(BF16)
           for sl, g_last, gcc in zip(sls, g_lasts, gccs)]
    qgs = [(q_ref[:, sl] * eg).astype(BF16) for sl, eg in zip(sls, egs)]
    states = [s_ref[h] for h in heads]
    s16s = [state.astype(BF16) for state in states]
    vn16s = [(sol[:, :DN_HEAD] - _dot(sol[:, DN_HEAD:].astype(BF16), s16)).astype(BF16)
             for sol, s16 in zip(sols, s16s)]
    outs = [_dot(qg, s16) + _dot(attn, vn16)
            for qg, s16, attn, vn16 in zip(qgs, s16s, attns, vn16s)]
    for h in heads:
        s_ref[h] = states[h] * jnp.exp(g_lasts[h]) + _dot_tn(kds[h], vn16s[h])
    for h in heads:
        o = outs[h]
        on = o * lax.rsqrt(jnp.mean(o * o, axis=-1, keepdims=True) + NORM_EPS) * nw
        o_ref[:, sls[h]] = (on * _silu(z_ref[:, sls[h]])).astype(o_ref.dtype)


def _gdn_chunks(qkv, proj, gates_col, gates_row, norm_w, b_sz, t_len):
    c = GDN_CHUNK
    nc = t_len // c
    width = DN_HEADS * DN_HEAD
    m = qkv.shape[0]
    row = lambda col: (lambda b, i: (b * nc + i, col))
    return pl.pallas_call(
        _gdn_chunk_kernel,
        grid=(b_sz, nc),
        in_specs=[pl.BlockSpec((c, width), row(0)), pl.BlockSpec((c, width), row(1)),
                  pl.BlockSpec((c, width), row(2)), pl.BlockSpec((c, width), row(3)),
                  pl.BlockSpec((c, LANE), row(0)),
                  pl.BlockSpec((1, 1, 2 * DN_HEADS, c), lambda b, i: (b, i, 0, 0)),
                  pl.BlockSpec((1, DN_HEAD), lambda b, i: (0, 0))],
        out_specs=pl.BlockSpec((c, width), row(0)),
        out_shape=jax.ShapeDtypeStruct((m, width), BF16),
        scratch_shapes=[pltpu.VMEM((DN_HEADS, DN_HEAD, DN_HEAD), F32)],
        compiler_params=_cparams("parallel", "arbitrary"),
        name="gdn_delta_rule",
    )(qkv, qkv, qkv, proj, gates_col, gates_row, norm_w.reshape(1, DN_HEAD).astype(F32))


def _rope_tables(pos):
    half = NSA_DH // 2
    inv_freq = ROPE_THETA ** (-jnp.arange(half, dtype=F32) / half)
    ang = pos.astype(F32)[:, None] * inv_freq[None, :]
    cos, sin = jnp.cos(ang), jnp.sin(ang)
    cos_t = jnp.concatenate([cos, cos, cos, cos], axis=-1)
    sin_t = jnp.concatenate([-sin, sin, -sin, sin], axis=-1)
    return cos_t, sin_t


def _rope128(x, cos_t, sin_t):
    lane = lax.broadcasted_iota(jnp.int32, x.shape, 1)
    lower = (lane & (NSA_DH - 1)) < (NSA_DH // 2)
    fwd = pltpu.roll(x, LANE - NSA_DH // 2, axis=1)
    bwd = pltpu.roll(x, NSA_DH // 2, axis=1)
    return x * cos_t + jnp.where(lower, fwd, bwd) * sin_t


def _kv_layout_kernel(kv_ref, cos_ref, sin_ref, ks_ref, vs_ref, kw_ref, vw_ref, *, tm, nt):
    gw = NSA_GROUPS * NSA_DH
    cos_t = cos_ref[...]
    sin_t = sin_ref[...]
    t_pos = (pl.program_id(0) % nt) * tm + lax.broadcasted_iota(jnp.int32, (tm, LANE), 0)
    lane = lax.broadcasted_iota(jnp.int32, (tm, LANE), 1)
    onehot = (_blk(t_pos, SEL_BLOCK) == lane).astype(BF16)
    ones_col = (lane == NSA_DH).astype(BF16)
    zeros_half = jnp.zeros((tm, NSA_DH), BF16)
    for pair in range(NSA_GROUPS // 2):
        cols = lambda src: kv_ref[:, src * gw + pair * LANE: src * gw + (pair + 1) * LANE]
        k_sel = _rope128(cols(2), cos_t, sin_t).astype(BF16)
        k_win = _rope128(cols(4), cos_t, sin_t).astype(BF16)
        v_sel = cols(3).astype(BF16)
        v_win = cols(5).astype(BF16)
        for half in range(2):
            g = 2 * pair + half
            sl = slice(half * NSA_DH, (half + 1) * NSA_DH)
            ks_ref[0, g, :, 0:LANE] = onehot
            ks_ref[0, g, :, LANE:LANE + NSA_DH] = k_sel[:, sl]
            ks_ref[0, g, :, LANE + NSA_DH:2 * LANE] = zeros_half
            kw_ref[0, g] = k_win[:, sl]
            for val, dst in ((v_sel, vs_ref), (v_win, vw_ref)):
                dst[0, g] = ones_col
                dst[0, g, :, 0:NSA_DH] = val[:, sl]


def _kv_layout(kv, cos_t, sin_t, b_sz, t_len, tm):
    nt = t_len // tm
    shape = lambda w: jax.ShapeDtypeStruct((b_sz, NSA_GROUPS, t_len, w), BF16)
    ospec = lambda w: pl.BlockSpec((1, NSA_GROUPS, tm, w), lambda i: (i // nt, 0, i % nt, 0))
    return pl.pallas_call(
        functools.partial(_kv_layout_kernel, tm=tm, nt=nt),
        grid=(b_sz * nt,),
        in_specs=[pl.BlockSpec((tm, kv.shape[1]), lambda i: (i, 0)),
                  pl.BlockSpec((tm, LANE), lambda i: (i % nt, 0)),
                  pl.BlockSpec((tm, LANE), lambda i: (i % nt, 0))],
        out_specs=[ospec(2 * LANE), ospec(LANE), ospec(NSA_DH), ospec(LANE)],
        out_shape=[shape(2 * LANE), shape(LANE), shape(NSA_DH), shape(LANE)],
        compiler_params=_cparams("parallel"),
        name="kv_rope_layout",
    )(kv, cos_t, sin_t)


def _q_layout_kernel(q_ref, cos_ref, sin_ref, o_ref):
    cos_t = cos_ref[...]
    sin_t = sin_ref[...]
    scale = NSA_DH ** -0.5 * LOG2_E
    for pair in range(NSA_HEADS // 2):
        x = _rope128(q_ref[:, pair * LANE:(pair + 1) * LANE], cos_t, sin_t) * scale
        o_ref[0, 2 * pair] = x[:, :NSA_DH].astype(BF16)
        o_ref[0, 2 * pair + 1] = x[:, NSA_DH:].astype(BF16)


def _q_layout(proj, cos_t, sin_t, b_sz, t_len, tm):
    nt = t_len // tm
    width = NSA_HEADS * NSA_DH
    return pl.pallas_call(
        _q_layout_kernel,
        grid=(b_sz * nt,),
        in_specs=[pl.BlockSpec((tm, width), lambda i: (i, 0)),
                  pl.BlockSpec((tm, LANE), lambda i: (i % nt, 0)),
                  pl.BlockSpec((tm, LANE), lambda i: (i % nt, 0))],
        out_specs=pl.BlockSpec((1, NSA_HEADS, tm, NSA_DH), lambda i: (i // nt, 0, i % nt, 0)),
        out_shape=jax.ShapeDtypeStruct((b_sz, NSA_HEADS, t_len, NSA_DH), BF16),
        compiler_params=_cparams("parallel"),
        name="q_rope_layout",
    )(proj, cos_t, sin_t)


def _compress_kernel(x_ref, pe_ref, w1_ref, w2_ref, cos_ref, sin_ref, o_ref):
    is_key = pl.program_id(0) == 0
    x = x_ref[0, 0, 0]
    n = x.shape[0]
    pe = pe_ref[0]
    a = _dot((x + pe[0:1, :]).astype(BF16), w1_ref[0, 0])
    b = _dot((x + pe[1:2, :]).astype(BF16), w1_ref[0, 1])
    hid = a + pltpu.roll(b, n - 1, axis=0)
    y = _dot(_silu(hid).astype(BF16), w2_ref[0])
    y = jnp.where(is_key, _rope128(y, cos_ref[...], sin_ref[...]), y)
    o_ref[0, 0, 0] = y[:, :NSA_DH].astype(BF16)


def _compress(xg, pe, w1, w2, cos_c, sin_c):
    _, b_sz, g_sz, n_chunk, flat = xg.shape
    return pl.pallas_call(
        _compress_kernel,
        grid=(2, b_sz, g_sz),
        in_specs=[pl.BlockSpec((1, 1, 1, n_chunk, flat), lambda s, b, g: (s, b, g, 0, 0)),
                  pl.BlockSpec((1, 2, flat), lambda s, b, g: (s, 0, 0)),
                  pl.BlockSpec((1, 2, flat, CMP_HIDDEN), lambda s, b, g: (s, 0, 0, 0)),
                  pl.BlockSpec((1, CMP_HIDDEN, LANE), lambda s, b, g: (s, 0, 0)),
                  pl.BlockSpec((n_chunk, LANE), lambda s, b, g: (0, 0)),
                  pl.BlockSpec((n_chunk, LANE), lambda s, b, g: (0, 0))],
        out_specs=pl.BlockSpec((1, 1, 1, n_chunk, NSA_DH), lambda s, b, g: (s, b, g, 0, 0)),
        out_shape=jax.ShapeDtypeStruct((2, b_sz, g_sz, n_chunk, NSA_DH), BF16),
        compiler_params=_cparams("parallel", "parallel", "parallel"),
        name="compress_blocks",
    )(xg, pe, w1, w2, cos_c, sin_c)


def _cmp_kernel(q_ref, k_ref, v_ref, ovt_ref, o_ref, sel_ref, *, n_sel):
    qb = pl.program_id(2)
    n_cmp = k_ref.shape[2]
    qb_per_group = LANE * CMP_STRIDE // Q_BLOCK
    n_var = max(1, n_cmp // LANE)
    for var in range(n_var):
        n_col = min(n_cmp, LANE * (var + 1))
        n_blk = min(n_sel, n_col * CMP_STRIDE // SEL_BLOCK)
        pl.when(qb // qb_per_group == var)(functools.partial(
            _cmp_body, qb, q_ref, k_ref, v_ref, ovt_ref, o_ref, sel_ref, n_col, n_blk))


def _cmp_body(qb, q_ref, k_ref, v_ref, ovt_ref, o_ref, sel_ref, n_cmp, n_sel):
    tq = Q_BLOCK
    rows = NSA_REP * tq
    n_pad = -(-n_sel // 8) * 8
    q = q_ref[0].reshape(rows, NSA_DH)
    s = _dot_nt(q, k_ref[0, 0, 0:n_cmp, :])
    row = lax.broadcasted_iota(jnp.int32, (rows, n_cmp), 0)
    col = lax.broadcasted_iota(jnp.int32, (rows, n_cmp), 1)
    t_row = qb * tq + (row & (tq - 1))
    s = jnp.where(col * CMP_STRIDE + (CMP_BLOCK - 1) <= t_row, s, NEG_BIG)
    m = jnp.max(s, axis=-1, keepdims=True)
    e = jnp.exp2(s - m)
    l = jnp.sum(e, axis=-1, keepdims=True)
    t_col = qb * tq + (lax.broadcasted_iota(jnp.int32, (rows, 1), 0) & (tq - 1))
    inv = jnp.where(t_col >= CMP_BLOCK - 1, 1.0 / l, 0.0)
    p = e * inv
    o = _dot(p.astype(BF16), v_ref[0, 0, 0:n_cmp, :])
    o_ref[0] = o.reshape(NSA_REP, tq, NSA_DH).astype(o_ref.dtype)

    psum = p[0:tq]
    for r in range(1, NSA_REP):
        psum = psum + p[r * tq:(r + 1) * tq]
    ph, plo = _split2(psum)
    ovt = ovt_ref[0:n_pad, 0:n_cmp]
    imp = _dot_nt(ovt, ph) + _dot_nt(ovt, plo)

    t_q = qb * tq + lax.broadcasted_iota(jnp.int32, (n_pad, tq), 1)
    blk = lax.broadcasted_iota(jnp.int32, (n_pad, tq), 0)
    blk_f = blk.astype(F32)
    cur = _blk(t_q, SEL_BLOCK)
    forced = (blk == 0) | (blk == cur) | (blk == cur - 1)
    visible = blk * SEL_BLOCK <= t_q
    score = jnp.where(visible, jnp.where(forced, FORCED_SCORE, imp), -1.0)
    if n_pad > n_sel:
        score = jnp.where(blk < n_sel, score, -2.0)
    taken = -float(2 ** 127)
    for _ in range(min(SEL_TOPK, n_sel)):
        mx = jnp.max(score, axis=0, keepdims=True)
        first = jnp.min(jnp.where(score == mx, blk_f, float(LANE)), axis=0, keepdims=True)
        score = jnp.where(blk_f == first, taken, score)
    chosen = jnp.where(score == taken, 0.0, NEG_BIG)
    if n_pad < LANE:
        chosen = jnp.concatenate([chosen, jnp.full((LANE - n_pad, tq), NEG_BIG, F32)], axis=0)
    sel_ref[0, 0] = chosen.T.astype(sel_ref.dtype)


def _cmp_attention(q, k_cmp, v_cmp, overlap_t, t_len):
    b_sz = q.shape[0]
    n_cmp = k_cmp.shape[2]
    nq = t_len // Q_BLOCK
    return pl.pallas_call(
        functools.partial(_cmp_kernel, n_sel=t_len // SEL_BLOCK),
        grid=(b_sz, NSA_GROUPS, nq),
        in_specs=[pl.BlockSpec((1, NSA_REP, Q_BLOCK, NSA_DH), lambda b, g, i: (b, g, i, 0)),
                  pl.BlockSpec((1, 1, n_cmp, NSA_DH), lambda b, g, i: (b, g, 0, 0)),
                  pl.BlockSpec((1, 1, n_cmp, NSA_DH), lambda b, g, i: (b, g, 0, 0)),
                  pl.BlockSpec((LANE, n_cmp), lambda b, g, i: (0, 0))],
        out_specs=[pl.BlockSpec((1, NSA_REP, Q_BLOCK, NSA_DH), lambda b, g, i: (b, g, i, 0)),
                   pl.BlockSpec((1, 1, Q_BLOCK, LANE), lambda b, g, i: (b, g, i, 0))],
        out_shape=[jax.ShapeDtypeStruct((b_sz, NSA_HEADS, t_len, NSA_DH), BF16),
                   jax.ShapeDtypeStruct((b_sz, NSA_GROUPS, t_len, LANE), BF16)],
        compiler_params=_cparams("parallel", "parallel", "parallel"),
        name="cmp_attention_topk",
    )(q, k_cmp, v_cmp, overlap_t)


def _online_softmax_step(s, vt, m_ref, acc_ref):
    reps = s.shape[1] // LANE
    m_prev = m_ref[...]
    m_next = jnp.maximum(m_prev, jnp.max(s, axis=-1, keepdims=True))
    p = jnp.exp2(s - jnp.concatenate([m_next] * reps, axis=1))
    alpha = jnp.exp2(m_prev - m_next)
    acc_ref[...] = alpha * acc_ref[...] + _dot(p.astype(BF16), vt)
    m_ref[...] = m_next


def _sel_kernel(q_ref, k_ref, v_ref, sel_ref, o_ref, qa_ref, m_ref, acc_ref, sa_ref, sb_ref):
    qb = pl.program_id(2)
    tq = Q_BLOCK
    tk = SEL_KV_TILE
    rows = NSA_REP * tq
    mask = sel_ref[0, 0]
    for r in range(NSA_REP):
        qa_ref[r * tq:(r + 1) * tq, 0:LANE] = mask
        qa_ref[r * tq:(r + 1) * tq, LANE:LANE + NSA_DH] = q_ref[0, r]
        qa_ref[r * tq:(r + 1) * tq, LANE + NSA_DH:2 * LANE] = jnp.zeros((tq, NSA_DH), BF16)
    m_ref[...] = jnp.full(m_ref.shape, -jnp.inf, F32)
    acc_ref[...] = jnp.zeros(acc_ref.shape, F32)
    last = (qb * tq) // tk

    def scores(j):
        k0 = pl.multiple_of(j * tk, tk)
        return _dot_nt(qa_ref[...], k_ref[0, 0, pl.ds(k0, tk), :])

    def values(j):
        return v_ref[0, 0, pl.ds(pl.multiple_of(j * tk, tk), tk), :]

    def causal_step(s):
        row = lax.broadcasted_iota(jnp.int32, (rows, tk), 0)
        kpos = last * tk + lax.broadcasted_iota(jnp.int32, (rows, tk), 1)
        s = jnp.where(kpos <= qb * tq + (row & (tq - 1)), s, NEG_BIG)
        _online_softmax_step(s, values(last), m_ref, acc_ref)

    sa_ref[...] = scores(0)

    def pair(i, carry):
        sb_ref[...] = scores(2 * i + 1)
        _online_softmax_step(sa_ref[...], values(2 * i), m_ref, acc_ref)
        sa_ref[...] = scores(2 * i + 2)
        _online_softmax_step(sb_ref[...], values(2 * i + 1), m_ref, acc_ref)
        return carry

    lax.fori_loop(0, last // 2, pair, 0)

    @pl.when(last % 2 == 0)
    def _():
        causal_step(sa_ref[...])

    @pl.when(last % 2 == 1)
    def _():
        sb_ref[...] = scores(last)
        _online_softmax_step(sa_ref[...], values(last - 1), m_ref, acc_ref)
        causal_step(sb_ref[...])

    acc = acc_ref[...]
    o = acc[:, 0:NSA_DH] / acc[:, NSA_DH:NSA_DH + 1]
    o_ref[0] = o.reshape(NSA_REP, tq, NSA_DH).astype(o_ref.dtype)


def _sel_attention(q, k_sel, v_sel, sel_mask, t_len):
    b_sz = q.shape[0]
    nq = t_len // Q_BLOCK
    rows = NSA_REP * Q_BLOCK
    return pl.pallas_call(
        _sel_kernel,
        grid=(b_sz, NSA_GROUPS, nq),
        in_specs=[pl.BlockSpec((1, NSA_REP, Q_BLOCK, NSA_DH), lambda b, g, i: (b, g, i, 0)),
                  pl.BlockSpec((1, 1, t_len, 2 * LANE), lambda b, g, i: (b, g, 0, 0)),
                  pl.BlockSpec((1, 1, t_len, LANE), lambda b, g, i: (b, g, 0, 0)),
                  pl.BlockSpec((1, 1, Q_BLOCK, LANE), lambda b, g, i: (b, g, i, 0))],
        out_specs=pl.BlockSpec((1, NSA_REP, Q_BLOCK, NSA_DH), lambda b, g, i: (b, g, i, 0)),
        out_shape=jax.ShapeDtypeStruct((b_sz, NSA_HEADS, t_len, NSA_DH), BF16),
        scratch_shapes=[pltpu.VMEM((rows, 2 * LANE), BF16), pltpu.VMEM((rows, LANE), F32),
                        pltpu.VMEM((rows, LANE), F32), pltpu.VMEM((rows, SEL_KV_TILE), F32),
                        pltpu.VMEM((rows, SEL_KV_TILE), F32)],
        compiler_params=_cparams("parallel", "parallel", "arbitrary"),
        name="sel_attention",
    )(q, k_sel, v_sel, sel_mask)


def _win_kernel(q_ref, k_ref, v_ref, o_ref):
    qb = pl.program_id(2)
    tq = Q_BLOCK
    span = WINDOW + tq
    rows = NSA_REP * tq
    q = q_ref[0].reshape(rows, NSA_DH)
    start = pl.multiple_of(jnp.maximum(qb * tq - WINDOW, 0), tq)
    kt = k_ref[0, 0, pl.ds(start, span), :]
    vt = v_ref[0, 0, pl.ds(start, span), :]
    t_q = qb * tq + lax.broadcasted_iota(jnp.int32, (tq, span), 0)
    kpos = start + lax.broadcasted_iota(jnp.int32, (tq, span), 1)
    ok = (kpos <= t_q) & (kpos > t_q - WINDOW)
    bias = jnp.where(ok, 0.0, NEG_BIG)
    s = (_dot_nt(q, kt).reshape(NSA_REP, tq, span) + bias[None]).reshape(rows, span)
    m = jnp.max(s, axis=-1, keepdims=True)
    p = jnp.exp2(s - m)
    acc = _dot(p.astype(BF16), vt)
    o = acc[:, 0:NSA_DH] / acc[:, NSA_DH:NSA_DH + 1]
    o_ref[0] = o.reshape(NSA_REP, tq, NSA_DH).astype(o_ref.dtype)


def _win_attention(q, k_win, v_win, t_len):
    b_sz = q.shape[0]
    nq = t_len // Q_BLOCK
    return pl.pallas_call(
        _win_kernel,
        grid=(b_sz, NSA_GROUPS, nq),
        in_specs=[pl.BlockSpec((1, NSA_REP, Q_BLOCK, NSA_DH), lambda b, g, i: (b, g, i, 0)),
                  pl.BlockSpec((1, 1, t_len, NSA_DH), lambda b, g, i: (b, g, 0, 0)),
                  pl.BlockSpec((1, 1, t_len, LANE), lambda b, g, i: (b, g, 0, 0))],
        out_specs=pl.BlockSpec((1, NSA_REP, Q_BLOCK, NSA_DH), lambda b, g, i: (b, g, i, 0)),
        out_shape=jax.ShapeDtypeStruct((b_sz, NSA_HEADS, t_len, NSA_DH), BF16),
        compiler_params=_cparams("parallel", "parallel", "parallel"),
        name="win_attention",
    )(q, k_win, v_win)


def _merge_kernel(oc_ref, os_ref, ow_ref, zc_ref, zs_ref, zw_ref, gt_ref, ge_ref,
                  w_ref, x_ref, lw_ref, lb_ref, o_ref):
    gates = _sigmoid(gt_ref[...])
    gh, gl = _split2(gates)
    acc = None
    for n, (o_r, z_r) in enumerate(((oc_ref, zc_ref), (os_ref, zs_ref), (ow_ref, zw_ref))):
        ex = ge_ref[n]
        gate = _dot(gh, ex) + _dot(gl, ex)
        o = jnp.concatenate([o_r[0, h] for h in range(NSA_HEADS)], axis=-1).astype(F32)
        term = gate * o * _silu(z_r[...])
        acc = term if acc is None else acc + term
    y = _dot(acc.astype(BF16), w_ref[...])
    r = DEEPNORM_ALPHA * x_ref[...] + y
    o_ref[...] = _layer_norm_rows(r, lw_ref[...], lb_ref[...])


def _merge_out(o_cmp, o_sel, o_win, proj, tail, gate_expand, w_out, x_res, ln_w, ln_b, t_len, tm):
    m, d = x_res.shape
    width = NSA_HEADS * NSA_DH
    nt = t_len // tm
    ospec = pl.BlockSpec((1, NSA_HEADS, tm, NSA_DH), lambda i: (i // nt, 0, i % nt, 0))
    zspec = lambda col: pl.BlockSpec((tm, width), lambda i: (i, col))
    fixed2 = lambda i: (0, 0)
    return pl.pallas_call(
        _merge_kernel,
        grid=(m // tm,),
        in_specs=[ospec, ospec, ospec, zspec(1), zspec(2), zspec(3),
                  pl.BlockSpec((tm, LANE), lambda i: (i, 0)),
                  pl.BlockSpec((3, LANE, width), lambda i: (0, 0, 0)),
                  pl.BlockSpec((width, d), fixed2),
                  pl.BlockSpec((tm, d), lambda i: (i, 0)),
                  pl.BlockSpec((1, d), fixed2), pl.BlockSpec((1, d), fixed2)],
        out_specs=pl.BlockSpec((tm, d), lambda i: (i, 0)),
        out_shape=jax.ShapeDtypeStruct((m, d), F32),
        compiler_params=_cparams("parallel"),
        name="merge_out_proj_layernorm",
    )(o_cmp, o_sel, o_win, proj, proj, proj, tail, gate_expand, w_out, x_res,
      ln_w.reshape(1, d), ln_b.reshape(1, d))


def _pad_cols(w, n):
    return jnp.pad(w, ((0, 0), (0, n - w.shape[1])))


def kernel(x, a_w_in, a_conv_w, a_a_log, a_dt_bias, a_norm_w, a_w_out, a_ln_w, a_ln_b,
           s_w_kv, s_pe_k, s_pe_v, s_w1_k, s_w2_k, s_w1_v, s_w2_v,
           b_w_in, b_w_out, b_ln_w, b_ln_b):
    b_sz, t_len, d = x.shape
    m = b_sz * t_len
    assert t_len % SEL_KV_TILE == 0 and t_len >= WINDOW + Q_BLOCK and d % LANE == 0
    assert a_w_in.shape[0] == 1 and b_w_in.shape[0] == 1
    tm_mm = min(1024, m)
    tm_row = min(512, t_len)

    x2 = x.reshape(m, d)
    x16 = x2.astype(BF16)

    dn_main = 4 * DN_HEADS * DN_HEAD
    w_in = a_w_in[0]
    proj = _matmul(x16, w_in[:, :dn_main].astype(BF16), F32, tm_mm, 512)
    tail = _matmul(x16, _pad_cols(w_in[:, dn_main:], LANE).astype(BF16), F32, tm_mm, LANE)
    qkv = _gdn_prep(proj, a_conv_w[0], t_len, tm_row)
    gates_col = _gdn_gates(tail, a_a_log[0], a_dt_bias[0], tm_mm)
    nc = t_len // GDN_CHUNK
    gates_row = gates_col[:, :2 * DN_HEADS].reshape(b_sz, nc, GDN_CHUNK, 2 * DN_HEADS).transpose(0, 1, 3, 2)
    o_gdn = _gdn_chunks(qkv, proj, gates_col, gates_row, a_norm_w[0], b_sz, t_len)
    h1, h1_16 = _out_proj_ln(o_gdn, a_w_out[0].astype(BF16), x2, a_ln_w[0], a_ln_b[0], tm_row)

    kv = _matmul(h1_16, s_w_kv.astype(BF16), F32, tm_mm, 512)
    pos = jnp.arange(t_len, dtype=jnp.int32)
    cos_t, sin_t = _rope_tables(pos)
    k_sel, v_sel, k_win, v_win = _kv_layout(kv, cos_t, sin_t, b_sz, t_len, tm_row)

    n_chunk = t_len // CMP_STRIDE
    gw = NSA_GROUPS * NSA_DH
    flat = CMP_STRIDE * NSA_DH
    xg = kv[:, :2 * gw].reshape(b_sz, n_chunk, CMP_STRIDE, 2, NSA_GROUPS, NSA_DH)
    xg = xg.transpose(3, 0, 4, 1, 2, 5).reshape(2, b_sz, NSA_GROUPS, n_chunk, flat)
    pe = jnp.stack([s_pe_k, s_pe_v]).reshape(2, 2, flat).astype(F32)
    w1 = jnp.stack([s_w1_k, s_w1_v]).reshape(2, 2, flat, CMP_HIDDEN).astype(BF16)
    w2 = jnp.stack([_pad_cols(s_w2_k, LANE), _pad_cols(s_w2_v, LANE)]).astype(BF16)
    cmp_end = jnp.arange(n_chunk, dtype=jnp.int32) * CMP_STRIDE + CMP_BLOCK - 1
    cos_c, sin_c = _rope_tables(cmp_end)
    kv_cmp = _compress(xg, pe, w1, w2, cos_c, sin_c)
    k_cmp, v_cmp = kv_cmp[0], kv_cmp[1]

    nsa_main = 4 * NSA_HEADS * NSA_DH
    w_in_b = b_w_in[0]
    proj_b = _matmul(h1_16, w_in_b[:, :nsa_main].astype(BF16), F32, tm_mm, 512)
    tail_b = _matmul(h1_16, _pad_cols(w_in_b[:, nsa_main:], LANE).astype(BF16), F32, tm_mm, LANE)
    q = _q_layout(proj_b, cos_t, sin_t, b_sz, t_len, tm_row)

    n_sel = t_len // SEL_BLOCK
    c_start = jnp.arange(n_chunk, dtype=jnp.int32)[:, None] * CMP_STRIDE
    s_start = jnp.arange(n_sel, dtype=jnp.int32)[None, :] * SEL_BLOCK
    ov = jnp.clip(jnp.minimum(c_start + CMP_BLOCK, s_start + SEL_BLOCK) - jnp.maximum(c_start, s_start), 0, None)
    overlap = (ov.astype(F32) / CMP_BLOCK).astype(BF16)
    overlap_t = jnp.pad(overlap.T, ((0, LANE - n_sel), (0, 0)))

    o_cmp, sel_mask = _cmp_attention(q, k_cmp, v_cmp, overlap_t, t_len)
    o_sel = _sel_attention(q, k_sel, v_sel, sel_mask, t_len)
    o_win = _win_attention(q, k_win, v_win, t_len)

    lane_id = jnp.arange(LANE, dtype=jnp.int32)[None, :, None]
    col_head = (jnp.arange(NSA_HEADS * NSA_DH, dtype=jnp.int32) // NSA_DH)[None, None, :]
    branch = jnp.arange(3, dtype=jnp.int32)[:, None, None]
    gate_expand = (lane_id == branch * NSA_HEADS + col_head).astype(BF16)
    out = _merge_out(o_cmp, o_sel, o_win, proj_b, tail_b, gate_expand, b_w_out[0].astype(BF16),
                     h1, b_ln_w[0], b_ln_b[0], t_len, min(256, t_len))
    return out.reshape(b_sz, t_len, d)
```

```python
import functools

import jax
import jax.numpy as jnp
from jax import lax
from jax.experimental import pallas as pl
from jax.experimental.pallas import tpu as pltpu

F32 = jnp.float32
BF16 = jnp.bfloat16

DN_HEADS = 8
DN_HEAD = 128
DN_CONV = 4
NSA_HEADS = 16
NSA_GROUPS = 4
NSA_REP = NSA_HEADS // NSA_GROUPS
NSA_DH = 64
CMP_BLOCK = 32
CMP_STRIDE = 16
CMP_HIDDEN = 128
SEL_BLOCK = 64
SEL_TOPK = 16
WINDOW = 512
Q_BLOCK = 128
FORCED_SCORE = 1.0e4
ROPE_THETA = 10000.0
NORM_EPS = 1e-6
DEPTH = 2
DEEPNORM_ALPHA = (2.0 * DEPTH) ** 0.25

GDN_CHUNK = 128
INV_BASE = 16
LANE = 128
SEL_KV_TILE = 512
SEL_Q_TILE = 512
CMP_Q_BLOCKS = 4
WIN_Q_BLOCKS = 4
NEG_BIG = -1e30
LOG2_E = 1.4426950408889634
VMEM_LIMIT = 56 * 1024 * 1024


def _cparams(*sem):
    return pltpu.CompilerParams(dimension_semantics=sem, vmem_limit_bytes=VMEM_LIMIT)


def _dot(a, b):
    return jnp.dot(a, b, preferred_element_type=F32)


def _dot_nt(a, b):
    return lax.dot_general(a, b, (((1,), (1,)), ((), ())), preferred_element_type=F32)


def _dot_tn(a, b):
    return lax.dot_general(a, b, (((0,), (0,)), ((), ())), preferred_element_type=F32)


def _split2(a):
    hi = a.astype(BF16)
    lo = (a - hi.astype(F32)).astype(BF16)
    return hi, lo


def _split3(a):
    hi = a.astype(BF16)
    r = a - hi.astype(F32)
    mid = r.astype(BF16)
    lo = (r - mid.astype(F32)).astype(BF16)
    return hi, mid, lo


def _mm(a, b):
    return _dot(a.astype(BF16), b.astype(BF16))


def _blk(idx, size):
    return idx >> (size.bit_length() - 1)


def _silu(x):
    return x / (1.0 + jnp.exp(-x))


def _sigmoid(x):
    return 1.0 / (1.0 + jnp.exp(-x))


def _mm_kernel(x_ref, w_ref, o_ref):
    o_ref[...] = _dot(x_ref[...], w_ref[...]).astype(o_ref.dtype)


def _matmul(x, w, out_dtype, tm, tn):
    m, k = x.shape
    n = w.shape[1]
    return pl.pallas_call(
        _mm_kernel,
        grid=(m // tm, n // tn),
        in_specs=[pl.BlockSpec((tm, k), lambda i, j: (i, 0)),
                  pl.BlockSpec((k, tn), lambda i, j: (0, j))],
        out_specs=pl.BlockSpec((tm, tn), lambda i, j: (i, j)),
        out_shape=jax.ShapeDtypeStruct((m, n), out_dtype),
        compiler_params=_cparams("parallel", "parallel"),
        name="proj_matmul",
    )(x, w)


def _layer_norm_rows(r, w, b):
    mu = jnp.mean(r, axis=-1, keepdims=True)
    d = r - mu
    var = jnp.mean(d * d, axis=-1, keepdims=True)
    return d * lax.rsqrt(var + NORM_EPS) * w + b


def _out_ln_kernel(a_ref, w_ref, x_ref, lw_ref, lb_ref, o_ref, ob_ref):
    y = _dot(a_ref[...], w_ref[...])
    r = DEEPNORM_ALPHA * x_ref[...] + y
    out = _layer_norm_rows(r, lw_ref[...], lb_ref[...])
    o_ref[...] = out
    ob_ref[...] = out.astype(BF16)


def _out_proj_ln(a, w, x_res, ln_w, ln_b, tm):
    m, k = a.shape
    n = w.shape[1]
    row = lambda i: (i, 0)
    fixed = lambda i: (0, 0)
    return pl.pallas_call(
        _out_ln_kernel,
        grid=(m // tm,),
        in_specs=[pl.BlockSpec((tm, k), row), pl.BlockSpec((k, n), fixed),
                  pl.BlockSpec((tm, n), row), pl.BlockSpec((1, n), fixed),
                  pl.BlockSpec((1, n), fixed)],
        out_specs=[pl.BlockSpec((tm, n), row), pl.BlockSpec((tm, n), row)],
        out_shape=[jax.ShapeDtypeStruct((m, n), F32), jax.ShapeDtypeStruct((m, n), BF16)],
        compiler_params=_cparams("parallel"),
        name="out_proj_layernorm",
    )(a, w, x_res, ln_w.reshape(1, n), ln_b.reshape(1, n))


def _gdn_qkv_kernel(x_ref, w_ref, cw_ref, o_ref, halo_ref, *, tm, t_len):
    j = pl.program_id(0)
    i = pl.program_id(1)
    @pl.when((i * tm) % t_len == 0)
    def _():
        halo_ref[...] = jnp.zeros(halo_ref.shape, F32)

    p = _dot(x_ref[...], w_ref[...])
    halo = halo_ref[...]
    halo_ref[...] = p[tm - 8:tm, :]
    cw = cw_ref[...]
    row = lax.broadcasted_iota(jnp.int32, halo.shape, 0)
    y = p * cw[DN_CONV - 1:DN_CONV, :]
    for back in range(1, DN_CONV):
        shifted = pltpu.roll(p, back, axis=0)
        head = jnp.where(row < back, pltpu.roll(halo, back, axis=0), shifted[0:8, :])
        shifted = jnp.concatenate([head, shifted[8:, :]], axis=0)
        y = y + shifted * cw[DN_CONV - 1 - back:DN_CONV - back, :]
    y = _silu(y)
    qscale = jnp.where(j == 0, DN_HEAD ** -0.5, 1.0)
    for h in range(DN_HEADS):
        sl = slice(h * DN_HEAD, (h + 1) * DN_HEAD)
        yh = y[:, sl]
        ss = jnp.sum(yh * yh, axis=-1, keepdims=True)
        scale = jnp.where(j < 2, lax.rsqrt(ss + NORM_EPS) * qscale, 1.0)
        o_ref[:, sl] = yh * scale


def _gdn_qkv(x16, w_qkv, conv_w, t_len, tm):
    m, d = x16.shape
    width = DN_HEADS * DN_HEAD
    kern = functools.partial(_gdn_qkv_kernel, tm=tm, t_len=t_len)
    return pl.pallas_call(
        kern,
        grid=(3, m // tm),
        in_specs=[pl.BlockSpec((tm, d), lambda j, i: (i, 0)),
                  pl.BlockSpec((d, width), lambda j, i: (0, j)),
                  pl.BlockSpec((DN_CONV, width), lambda j, i: (0, j))],
        out_specs=pl.BlockSpec((tm, width), lambda j, i: (i, j)),
        out_shape=jax.ShapeDtypeStruct((m, 3 * width), F32),
        scratch_shapes=[pltpu.VMEM((8, width), F32)],
        compiler_params=_cparams("parallel", "arbitrary"),
        name="gdn_qkv_proj_conv",
    )(x16, w_qkv, conv_w)


def _gdn_gate_kernel(t_ref, al_ref, dt_ref, o_ref):
    x = t_ref[...]
    lane = lax.broadcasted_iota(jnp.int32, x.shape, 1)
    beta = _sigmoid(x)
    z = x + dt_ref[...]
    softplus = jnp.maximum(z, 0.0) + jnp.log(1.0 + jnp.exp(-jnp.abs(z)))
    g = -jnp.exp(al_ref[...]) * softplus
    o_ref[...] = jnp.where(lane < DN_HEADS, beta, jnp.where(lane < 2 * DN_HEADS, g, 0.0))


def _gdn_gates(tail, a_log, dt_bias, tm):
    m = tail.shape[0]
    pad = lambda v: jnp.zeros((1, LANE), F32).at[0, DN_HEADS:2 * DN_HEADS].set(v.astype(F32))
    return pl.pallas_call(
        _gdn_gate_kernel,
        grid=(m // tm,),
        in_specs=[pl.BlockSpec((tm, LANE), lambda i: (i, 0)),
                  pl.BlockSpec((1, LANE), lambda i: (0, 0)),
                  pl.BlockSpec((1, LANE), lambda i: (0, 0))],
        out_specs=pl.BlockSpec((tm, LANE), lambda i: (i, 0)),
        out_shape=jax.ShapeDtypeStruct((m, LANE), F32),
        compiler_params=_cparams("parallel"),
        name="gdn_gates",
    )(tail, pad(a_log), pad(dt_bias))


def _inv_unit_lower(lows, ii, jj):
    c = lows[0].shape[0]
    eye = (ii == jj).astype(F32)
    same = _blk(ii, INV_BASE) == _blk(jj, INV_BASE)
    ds = [jnp.where(same, low, 0.0) for low in lows]
    xs = [eye - d for d in ds]
    ps = ds
    span = 2
    while span < INV_BASE:
        ps = [_mm(p, p) for p in ps]
        xs = [_mm(x, eye + p) for x, p in zip(xs, ps)]
        span *= 2
    s = INV_BASE
    while s < c:
        off = (_blk(ii, 2 * s) == _blk(jj, 2 * s)) & (_blk(ii, s) != _blk(jj, s))
        ts = [_mm(x, jnp.where(off, low, 0.0)) for x, low in zip(xs, lows)]
        xs = [x - _mm(t, x) for x, t in zip(xs, ts)]
        s *= 2
    return xs


def _gdn_chunk_kernel(q_ref, k_ref, v_ref, z_ref, gc_ref, gr_ref, nw_ref, o_ref, s_ref):
    c = GDN_CHUNK

    @pl.when(pl.program_id(1) == 0)
    def _():
        s_ref[...] = jnp.zeros_like(s_ref)

    ii = lax.broadcasted_iota(jnp.int32, (c, c), 0)
    jj = lax.broadcasted_iota(jnp.int32, (c, c), 1)
    incl = ii >= jj
    strict = ii > jj
    gcol = gc_ref[...]
    grow = gr_ref[0, 0]
    ltri = incl.astype(BF16)
    utri = (ii <= jj).astype(BF16)
    gc3 = _split3(gcol)
    gr3 = _split3(grow)
    gcs_col = _dot(ltri, gc3[0]) + (_dot(ltri, gc3[1]) + _dot(ltri, gc3[2]))
    gcs_row = _dot(gr3[0], utri) + (_dot(gr3[1], utri) + _dot(gr3[2], utri))
    nw = nw_ref[...]

    heads = range(DN_HEADS)
    sls = [slice(h * DN_HEAD, (h + 1) * DN_HEAD) for h in heads]
    betas = [gcol[:, h:h + 1] for h in heads]
    gccs = [gcs_col[:, DN_HEADS + h:DN_HEADS + h + 1] for h in heads]
    gcrs = [gcs_row[DN_HEADS + h:DN_HEADS + h + 1, :] for h in heads]
    decays = [jnp.where(incl, jnp.exp(jnp.where(incl, gcc - gcr, 0.0)), 0.0)
              for gcc, gcr in zip(gccs, gcrs)]
    kbs = [k_ref[:, sl] * beta for sl, beta in zip(sls, betas)]
    k16s = [k_ref[:, sl].astype(BF16) for sl in sls]
    lows = [jnp.where(strict, _dot_nt(kb.astype(BF16), k16) * decay, 0.0)
            for kb, k16, decay in zip(kbs, k16s, decays)]
    attns = [jnp.where(incl, _dot_nt(q_ref[:, sl].astype(BF16), k16) * decay, 0.0).astype(BF16)
             for sl, k16, decay in zip(sls, k16s, decays)]
    tinvs = _inv_unit_lower(lows, ii, jj)
    egs = [jnp.exp(gcc) for gcc in gccs]
    sols = [_mm(tinv, jnp.concatenate([v_ref[:, sl] * beta, kb * eg], axis=1))
            for tinv, sl, beta, kb, eg in zip(tinvs, sls, betas, kbs, egs)]
    g_lasts = [gcc[c - 1:c, :] for gcc in gccs]
    kds = [(k_ref[:, sl] * jnp.exp(g_last - gcc)).astype(BF16)
           for sl, g_last, gcc in zip(sls, g_lasts, gccs)]
    qgs = [(q_ref[:, sl] * eg).astype(BF16) for sl, eg in zip(sls, egs)]
    states = [s_ref[h] for h in heads]
    s16s = [state.astype(BF16) for state in states]
    vn16s = [(sol[:, :DN_HEAD] - _dot(sol[:, DN_HEAD:].astype(BF16), s16)).astype(BF16)
             for sol, s16 in zip(sols, s16s)]
    outs = [_dot(qg, s16) + _dot(attn, vn16)
            for qg, s16, attn, vn16 in zip(qgs, s16s, attns, vn16s)]
    for h in heads:
        s_ref[h] = states[h] * jnp.exp(g_lasts[h]) + _dot_tn(kds[h], vn16s[h])
    for h in heads:
        o = outs[h]
        on = o * lax.rsqrt(jnp.mean(o * o, axis=-1, keepdims=True) + NORM_EPS) * nw
        o_ref[:, sls[h]] = (on * _silu(z_ref[:, sls[h]])).astype(o_ref.dtype)


def _gdn_chunks(qkv, proj, gates_col, gates_row, norm_w, b_sz, t_len):
    c = GDN_CHUNK
    nc = t_len // c
    width = DN_HEADS * DN_HEAD
    m = qkv.shape[0]
    row = lambda col: (lambda b, i: (b * nc + i, col))
    return pl.pallas_call(
        _gdn_chunk_kernel,
        grid=(b_sz, nc),
        in_specs=[pl.BlockSpec((c, width), row(0)), pl.BlockSpec((c, width), row(1)),
                  pl.BlockSpec((c, width), row(2)), pl.BlockSpec((c, width), row(0)),
                  pl.BlockSpec((c, LANE), row(0)),
                  pl.BlockSpec((1, 1, 2 * DN_HEADS, c), lambda b, i: (b, i, 0, 0)),
                  pl.BlockSpec((1, DN_HEAD), lambda b, i: (0, 0))],
        out_specs=pl.BlockSpec((c, width), row(0)),
        out_shape=jax.ShapeDtypeStruct((m, width), BF16),
        scratch_shapes=[pltpu.VMEM((DN_HEADS, DN_HEAD, DN_HEAD), F32)],
        compiler_params=_cparams("parallel", "arbitrary"),
        name="gdn_delta_rule",
    )(qkv, qkv, qkv, proj, gates_col, gates_row, norm_w.reshape(1, DN_HEAD).astype(F32))


def _rope_tables(pos):
    half = NSA_DH // 2
    inv_freq = ROPE_THETA ** (-jnp.arange(half, dtype=F32) / half)
    ang = pos.astype(F32)[:, None] * inv_freq[None, :]
    cos, sin = jnp.cos(ang), jnp.sin(ang)
    cos_t = jnp.concatenate([cos, cos, cos, cos], axis=-1)
    sin_t = jnp.concatenate([-sin, sin, -sin, sin], axis=-1)
    return cos_t, sin_t


def _rope128(x, cos_t, sin_t):
    lane = lax.broadcasted_iota(jnp.int32, x.shape, 1)
    lower = (lane & (NSA_DH - 1)) < (NSA_DH // 2)
    fwd = pltpu.roll(x, LANE - NSA_DH // 2, axis=1)
    bwd = pltpu.roll(x, NSA_DH // 2, axis=1)
    return x * cos_t + jnp.where(lower, fwd, bwd) * sin_t


def _kv_proj_kernel(x_ref, w_ref, cos_ref, sin_ref, cmp_ref, ks_ref, vs_ref, kw_ref, vw_ref, *, tm, nt):
    gw = NSA_GROUPS * NSA_DH
    kv = _dot(x_ref[...], w_ref[...])
    cmp_ref[...] = kv[:, 0:2 * gw]
    cos_t = cos_ref[...]
    sin_t = sin_ref[...]
    t_pos = (pl.program_id(0) % nt) * tm + lax.broadcasted_iota(jnp.int32, (tm, LANE), 0)
    lane = lax.broadcasted_iota(jnp.int32, (tm, LANE), 1)
    onehot = (_blk(t_pos, SEL_BLOCK) == lane).astype(BF16)
    ones_col = (lane == NSA_DH).astype(BF16)
    zeros_half = jnp.zeros((tm, NSA_DH), BF16)
    for pair in range(NSA_GROUPS // 2):
        cols = lambda src: kv[:, src * gw + pair * LANE: src * gw + (pair + 1) * LANE]
        k_sel = _rope128(cols(2), cos_t, sin_t).astype(BF16)
        k_win = _rope128(cols(4), cos_t, sin_t).astype(BF16)
        v_sel = cols(3).astype(BF16)
        v_win = cols(5).astype(BF16)
        for half in range(2):
            g = 2 * pair + half
            sl = slice(half * NSA_DH, (half + 1) * NSA_DH)
            ks_ref[0, g, :, 0:LANE] = onehot
            ks_ref[0, g, :, LANE:LANE + NSA_DH] = k_sel[:, sl]
            ks_ref[0, g, :, LANE + NSA_DH:2 * LANE] = zeros_half
            kw_ref[0, g] = k_win[:, sl]
            for val, dst in ((v_sel, vs_ref), (v_win, vw_ref)):
                dst[0, g] = ones_col
                dst[0, g, :, 0:NSA_DH] = val[:, sl]


def _kv_proj(h16, w_kv, cos_t, sin_t, b_sz, t_len, tm):
    nt = t_len // tm
    m, d = h16.shape
    gw = NSA_GROUPS * NSA_DH
    shape = lambda w: jax.ShapeDtypeStruct((b_sz, NSA_GROUPS, t_len, w), BF16)
    ospec = lambda w: pl.BlockSpec((1, NSA_GROUPS, tm, w), lambda i: (i // nt, 0, i % nt, 0))
    return pl.pallas_call(
        functools.partial(_kv_proj_kernel, tm=tm, nt=nt),
        grid=(b_sz * nt,),
        in_specs=[pl.BlockSpec((tm, d), lambda i: (i, 0)),
                  pl.BlockSpec((d, 6 * gw), lambda i: (0, 0)),
                  pl.BlockSpec((tm, LANE), lambda i: (i % nt, 0)),
                  pl.BlockSpec((tm, LANE), lambda i: (i % nt, 0))],
        out_specs=[pl.BlockSpec((tm, 2 * gw), lambda i: (i, 0)),
                   ospec(2 * LANE), ospec(LANE), ospec(NSA_DH), ospec(LANE)],
        out_shape=[jax.ShapeDtypeStruct((m, 2 * gw), F32),
                   shape(2 * LANE), shape(LANE), shape(NSA_DH), shape(LANE)],
        compiler_params=_cparams("parallel"),
        name="kv_proj_rope_layout",
    )(h16, w_kv, cos_t, sin_t)


def _q_proj_kernel(x_ref, w_ref, cos_ref, sin_ref, o_ref):
    q = _dot(x_ref[...], w_ref[...])
    cos_t = cos_ref[...]
    sin_t = sin_ref[...]
    scale = NSA_DH ** -0.5 * LOG2_E
    for pair in range(NSA_HEADS // 2):
        x = _rope128(q[:, pair * LANE:(pair + 1) * LANE], cos_t, sin_t) * scale
        o_ref[0, 2 * pair] = x[:, :NSA_DH].astype(BF16)
        o_ref[0, 2 * pair + 1] = x[:, NSA_DH:].astype(BF16)


def _q_proj(h16, w_q, cos_t, sin_t, b_sz, t_len, tm):
    nt = t_len // tm
    m, d = h16.shape
    width = NSA_HEADS * NSA_DH
    return pl.pallas_call(
        _q_proj_kernel,
        grid=(b_sz * nt,),
        in_specs=[pl.BlockSpec((tm, d), lambda i: (i, 0)),
                  pl.BlockSpec((d, width), lambda i: (0, 0)),
                  pl.BlockSpec((tm, LANE), lambda i: (i % nt, 0)),
                  pl.BlockSpec((tm, LANE), lambda i: (i % nt, 0))],
        out_specs=pl.BlockSpec((1, NSA_HEADS, tm, NSA_DH), lambda i: (i // nt, 0, i % nt, 0)),
        out_shape=jax.ShapeDtypeStruct((b_sz, NSA_HEADS, t_len, NSA_DH), BF16),
        compiler_params=_cparams("parallel"),
        name="q_proj_rope_layout",
    )(h16, w_q, cos_t, sin_t)


def _compress_kernel(x_ref, pe_ref, w1_ref, w2_ref, cos_ref, sin_ref, o_ref, *, n_chunk):
    is_key = pl.program_id(0) == 0
    chunk = jnp.concatenate([x_ref[pl.ds(tau, n_chunk, stride=CMP_STRIDE), :]
                             for tau in range(CMP_STRIDE)], axis=1)
    a = _dot((chunk + pe_ref[0, 0:1, :]).astype(BF16), w1_ref[0, 0])
    b = _dot((chunk + pe_ref[0, 1:2, :]).astype(BF16), w1_ref[0, 1])
    hid = a + pltpu.roll(b, n_chunk - 1, axis=0)
    act = _silu(hid).astype(BF16)
    for half in range(2):
        y = _dot(act[:, half * CMP_HIDDEN:(half + 1) * CMP_HIDDEN], w2_ref[0])
        y = jnp.where(is_key, _rope128(y, cos_ref[...], sin_ref[...]), y)
        o_ref[0, 0, half] = y[:, :NSA_DH].astype(BF16)


def _compress(kv_raw, pe, w1, w2, cos_c, sin_c, b_sz, t_len):
    n_chunk = t_len // CMP_STRIDE
    pairs = NSA_GROUPS // 2
    flat = CMP_STRIDE * LANE
    return pl.pallas_call(
        functools.partial(_compress_kernel, n_chunk=n_chunk),
        grid=(2, b_sz, pairs),
        in_specs=[pl.BlockSpec((t_len, LANE), lambda s, b, p: (b, s * pairs + p)),
                  pl.BlockSpec((1, 2, flat), lambda s, b, p: (s, 0, 0)),
                  pl.BlockSpec((1, 2, flat, 2 * CMP_HIDDEN), lambda s, b, p: (s, 0, 0, 0)),
                  pl.BlockSpec((1, CMP_HIDDEN, LANE), lambda s, b, p: (s, 0, 0)),
                  pl.BlockSpec((n_chunk, LANE), lambda s, b, p: (0, 0)),
                  pl.BlockSpec((n_chunk, LANE), lambda s, b, p: (0, 0))],
        out_specs=pl.BlockSpec((1, 1, 2, n_chunk, NSA_DH), lambda s, b, p: (s, b, p, 0, 0)),
        out_shape=jax.ShapeDtypeStruct((2, b_sz, NSA_GROUPS, n_chunk, NSA_DH), BF16),
        compiler_params=_cparams("parallel", "parallel", "parallel"),
        name="compress_blocks",
    )(kv_raw, pe, w1, w2, cos_c, sin_c)


def _cmp_kernel(q_ref, k_ref, v_ref, ovt_ref, o_ref, sel_ref, *, n_sel):
    step = pl.program_id(2)
    n_cmp = k_ref.shape[2]
    steps_per_group = LANE * CMP_STRIDE // (Q_BLOCK * CMP_Q_BLOCKS)
    n_var = max(1, n_cmp // LANE)
    for var in range(n_var):
        n_col = min(n_cmp, LANE * (var + 1))
        n_blk = min(n_sel, n_col * CMP_STRIDE // SEL_BLOCK)
        pl.when(step // steps_per_group == var)(functools.partial(
            _cmp_body, step, q_ref, k_ref, v_ref, ovt_ref, o_ref, sel_ref, n_col, n_blk))


def _cmp_body(step, q_ref, k_ref, v_ref, ovt_ref, o_ref, sel_ref, n_cmp, n_sel):
    tq = Q_BLOCK
    rows = NSA_REP * tq
    n_pad = -(-n_sel // 8) * 8
    blocks = range(CMP_Q_BLOCKS)
    t0s = [(step * CMP_Q_BLOCKS + x) * tq for x in blocks]
    kc = k_ref[0, 0, 0:n_cmp, :]
    ss = [_dot_nt(q_ref[0, :, x * tq:(x + 1) * tq, :].reshape(rows, NSA_DH), kc) for x in blocks]
    i_row = lax.broadcasted_iota(jnp.int32, (rows, n_cmp), 0) & (tq - 1)
    end = lax.broadcasted_iota(jnp.int32, (rows, n_cmp), 1) * CMP_STRIDE + (CMP_BLOCK - 1)
    ss = [jnp.where(end <= t0 + i_row, s, NEG_BIG) for s, t0 in zip(ss, t0s)]
    ms = [jnp.max(s, axis=-1, keepdims=True) for s in ss]
    es = [jnp.exp2(s - m) for s, m in zip(ss, ms)]
    ls = [jnp.sum(e, axis=-1, keepdims=True) for e in es]
    i_col = lax.broadcasted_iota(jnp.int32, (rows, 1), 0) & (tq - 1)
    invs = [jnp.where(t0 + i_col >= CMP_BLOCK - 1, 1.0 / l, 0.0) for t0, l in zip(t0s, ls)]
    ps = [e * inv for e, inv in zip(es, invs)]
    vc = v_ref[0, 0, 0:n_cmp, :]
    for x in blocks:
        o = _dot(ps[x].astype(BF16), vc)
        o_ref[0, :, x * tq:(x + 1) * tq, :] = o.reshape(NSA_REP, tq, NSA_DH).astype(o_ref.dtype)

    ovt = ovt_ref[0:n_pad, 0:n_cmp]
    imps = []
    for p in ps:
        psum = p[0:tq]
        for r in range(1, NSA_REP):
            psum = psum + p[r * tq:(r + 1) * tq]
        ph, plo = _split2(psum)
        imps.append(_dot_nt(ovt, ph) + _dot_nt(ovt, plo))

    i_q = lax.broadcasted_iota(jnp.int32, (n_pad, tq), 1)
    blk = lax.broadcasted_iota(jnp.int32, (n_pad, tq), 0)
    blk_f = blk.astype(F32)
    scores = []
    for t0, imp in zip(t0s, imps):
        t_q = t0 + i_q
        cur = _blk(t_q, SEL_BLOCK)
        forced = (blk == 0) | (blk == cur) | (blk == cur - 1)
        score = jnp.where(blk * SEL_BLOCK <= t_q, jnp.where(forced, FORCED_SCORE, imp), -1.0)
        if n_pad > n_sel:
            score = jnp.where(blk < n_sel, score, -2.0)
        scores.append(score)
    taken = -float(2 ** 127)
    for _ in range(min(SEL_TOPK, n_sel)):
        mxs = [jnp.max(score, axis=0, keepdims=True) for score in scores]
        firsts = [jnp.min(jnp.where(score == mx, blk_f, float(LANE)), axis=0, keepdims=True)
                  for score, mx in zip(scores, mxs)]
        scores = [jnp.where(blk_f == first, taken, score)
                  for score, first in zip(scores, firsts)]
    for x in blocks:
        chosen = jnp.where(scores[x] == taken, 0.0, NEG_BIG)
        if n_pad < LANE:
            chosen = jnp.concatenate([chosen, jnp.full((LANE - n_pad, tq), NEG_BIG, F32)], axis=0)
        sel_ref[0, 0, x * tq:(x + 1) * tq, :] = chosen.T.astype(sel_ref.dtype)


def _cmp_attention(q, k_cmp, v_cmp, overlap_t, t_len):
    b_sz = q.shape[0]
    n_cmp = k_cmp.shape[2]
    tq = Q_BLOCK * CMP_Q_BLOCKS
    nq = t_len // tq
    return pl.pallas_call(
        functools.partial(_cmp_kernel, n_sel=t_len // SEL_BLOCK),
        grid=(b_sz, NSA_GROUPS, nq),
        in_specs=[pl.BlockSpec((1, NSA_REP, tq, NSA_DH), lambda b, g, i: (b, g, i, 0)),
                  pl.BlockSpec((1, 1, n_cmp, NSA_DH), lambda b, g, i: (b, g, 0, 0)),
                  pl.BlockSpec((1, 1, n_cmp, NSA_DH), lambda b, g, i: (b, g, 0, 0)),
                  pl.BlockSpec((LANE, n_cmp), lambda b, g, i: (0, 0))],
        out_specs=[pl.BlockSpec((1, NSA_REP, tq, NSA_DH), lambda b, g, i: (b, g, i, 0)),
                   pl.BlockSpec((1, 1, tq, LANE), lambda b, g, i: (b, g, i, 0))],
        out_shape=[jax.ShapeDtypeStruct((b_sz, NSA_HEADS, t_len, NSA_DH), BF16),
                   jax.ShapeDtypeStruct((b_sz, NSA_GROUPS, t_len, LANE), BF16)],
        compiler_params=_cparams("parallel", "parallel", "parallel"),
        name="cmp_attention_topk",
    )(q, k_cmp, v_cmp, overlap_t)


def _online_softmax_step(s, vt, m_ref, acc_ref):
    reps = s.shape[1] // LANE
    m_prev = m_ref[...]
    m_next = jnp.maximum(m_prev, jnp.max(s, axis=-1, keepdims=True))
    p = jnp.exp2(s - jnp.concatenate([m_next] * reps, axis=1))
    alpha = jnp.exp2(m_prev - m_next)
    acc_ref[...] = alpha * acc_ref[...] + _dot(p.astype(BF16), vt)
    m_ref[...] = m_next


def _sel_kernel(q_ref, k_ref, v_ref, sel_ref, o_ref, qa_ref, m_ref, acc_ref, sa_ref, sb_ref):
    qb = pl.program_id(2)
    tq = SEL_Q_TILE
    tk = SEL_KV_TILE
    rows = NSA_REP * tq
    mask = sel_ref[0, 0]
    for r in range(NSA_REP):
        qa_ref[r * tq:(r + 1) * tq, 0:LANE] = mask
        qa_ref[r * tq:(r + 1) * tq, LANE:LANE + NSA_DH] = q_ref[0, r]
        qa_ref[r * tq:(r + 1) * tq, LANE + NSA_DH:2 * LANE] = jnp.zeros((tq, NSA_DH), BF16)
    m_ref[...] = jnp.full(m_ref.shape, -jnp.inf, F32)
    acc_ref[...] = jnp.zeros(acc_ref.shape, F32)
    last = (qb * tq) // tk

    def scores(j):
        k0 = pl.multiple_of(j * tk, tk)
        return _dot_nt(qa_ref[...], k_ref[0, 0, pl.ds(k0, tk), :])

    def values(j):
        return v_ref[0, 0, pl.ds(pl.multiple_of(j * tk, tk), tk), :]

    def causal_step(s):
        row = lax.broadcasted_iota(jnp.int32, (rows, tk), 0)
        kpos = last * tk + lax.broadcasted_iota(jnp.int32, (rows, tk), 1)
        s = jnp.where(kpos <= qb * tq + (row & (tq - 1)), s, NEG_BIG)
        _online_softmax_step(s, values(last), m_ref, acc_ref)

    sa_ref[...] = scores(0)

    def pair(i, carry):
        sb_ref[...] = scores(2 * i + 1)
        _online_softmax_step(sa_ref[...], values(2 * i), m_ref, acc_ref)
        sa_ref[...] = scores(2 * i + 2)
        _online_softmax_step(sb_ref[...], values(2 * i + 1), m_ref, acc_ref)
        return carry

    lax.fori_loop(0, last // 2, pair, 0)

    @pl.when(last % 2 == 0)
    def _():
        causal_step(sa_ref[...])

    @pl.when(last % 2 == 1)
    def _():
        sb_ref[...] = scores(last)
        _online_softmax_step(sa_ref[...], values(last - 1), m_ref, acc_ref)
        causal_step(sb_ref[...])

    acc = acc_ref[...]
    o = acc[:, 0:NSA_DH] / acc[:, NSA_DH:NSA_DH + 1]
    o_ref[0] = o.reshape(NSA_REP, tq, NSA_DH).astype(o_ref.dtype)


def _sel_attention(q, k_sel, v_sel, sel_mask, t_len):
    b_sz = q.shape[0]
    tq = SEL_Q_TILE
    nq = t_len // tq
    rows = NSA_REP * tq
    return pl.pallas_call(
        _sel_kernel,
        grid=(b_sz, NSA_GROUPS, nq),
        in_specs=[pl.BlockSpec((1, NSA_REP, tq, NSA_DH), lambda b, g, i: (b, g, i, 0)),
                  pl.BlockSpec((1, 1, t_len, 2 * LANE), lambda b, g, i: (b, g, 0, 0)),
                  pl.BlockSpec((1, 1, t_len, LANE), lambda b, g, i: (b, g, 0, 0)),
                  pl.BlockSpec((1, 1, tq, LANE), lambda b, g, i: (b, g, i, 0))],
        out_specs=pl.BlockSpec((1, NSA_REP, tq, NSA_DH), lambda b, g, i: (b, g, i, 0)),
        out_shape=jax.ShapeDtypeStruct((b_sz, NSA_HEADS, t_len, NSA_DH), BF16),
        scratch_shapes=[pltpu.VMEM((rows, 2 * LANE), BF16), pltpu.VMEM((rows, LANE), F32),
                        pltpu.VMEM((rows, LANE), F32), pltpu.VMEM((rows, SEL_KV_TILE), F32),
                        pltpu.VMEM((rows, SEL_KV_TILE), F32)],
        compiler_params=_cparams("parallel", "parallel", "arbitrary"),
        name="sel_attention",
    )(q, k_sel, v_sel, sel_mask)


def _win_body(step, q_ref, k_ref, v_ref, o_ref, interior):
    tq = Q_BLOCK
    span = WINDOW + tq
    rows = NSA_REP * tq
    blocks = range(WIN_Q_BLOCKS)
    qbs = [step * WIN_Q_BLOCKS + x for x in blocks]
    starts = [pl.multiple_of(jnp.maximum(qb * tq - WINDOW, 0), tq) for qb in qbs]
    ss = [_dot_nt(q_ref[0, :, x * tq:(x + 1) * tq, :].reshape(rows, NSA_DH),
                  k_ref[0, 0, pl.ds(starts[x], span), :]) for x in blocks]
    if interior:
        i_row = lax.broadcasted_iota(jnp.int32, (rows, tq), 0) & (tq - 1)
        col = lax.broadcasted_iota(jnp.int32, (rows, tq), 1)
        ss = [jnp.concatenate([jnp.where(col > i_row, s[:, 0:tq], NEG_BIG), s[:, tq:WINDOW],
                               jnp.where(col <= i_row, s[:, WINDOW:span], NEG_BIG)], axis=1) for s in ss]
    else:
        i_row = lax.broadcasted_iota(jnp.int32, (rows, span), 0) & (tq - 1)
        col = lax.broadcasted_iota(jnp.int32, (rows, span), 1)
        masked = []
        for x in blocks:
            t_q = qbs[x] * tq + i_row
            kpos = starts[x] + col
            masked.append(jnp.where((kpos <= t_q) & (kpos > t_q - WINDOW), ss[x], NEG_BIG))
        ss = masked
    ms = [jnp.max(s, axis=-1, keepdims=True) for s in ss]
    ps = [jnp.exp2(s - m).astype(BF16) for s, m in zip(ss, ms)]
    accs = [_dot(ps[x], v_ref[0, 0, pl.ds(starts[x], span), :]) for x in blocks]
    for x in blocks:
        o = accs[x][:, 0:NSA_DH] / accs[x][:, NSA_DH:NSA_DH + 1]
        o_ref[0, :, x * tq:(x + 1) * tq, :] = o.reshape(NSA_REP, tq, NSA_DH).astype(o_ref.dtype)


def _win_kernel(q_ref, k_ref, v_ref, o_ref):
    step = pl.program_id(2)
    edge_steps = -(-(WINDOW // Q_BLOCK) // WIN_Q_BLOCKS)
    pl.when(step < edge_steps)(functools.partial(_win_body, step, q_ref, k_ref, v_ref, o_ref, False))
    pl.when(step >= edge_steps)(functools.partial(_win_body, step, q_ref, k_ref, v_ref, o_ref, True))


def _win_attention(q, k_win, v_win, t_len):
    b_sz = q.shape[0]
    tq = Q_BLOCK * WIN_Q_BLOCKS
    nq = t_len // tq
    return pl.pallas_call(
        _win_kernel,
        grid=(b_sz, NSA_GROUPS, nq),
        in_specs=[pl.BlockSpec((1, NSA_REP, tq, NSA_DH), lambda b, g, i: (b, g, i, 0)),
                  pl.BlockSpec((1, 1, t_len, NSA_DH), lambda b, g, i: (b, g, 0, 0)),
                  pl.BlockSpec((1, 1, t_len, LANE), lambda b, g, i: (b, g, 0, 0))],
        out_specs=pl.BlockSpec((1, NSA_REP, tq, NSA_DH), lambda b, g, i: (b, g, i, 0)),
        out_shape=jax.ShapeDtypeStruct((b_sz, NSA_HEADS, t_len, NSA_DH), BF16),
        compiler_params=_cparams("parallel", "parallel", "parallel"),
        name="win_attention",
    )(q, k_win, v_win)


def _merge_kernel(oc_ref, os_ref, ow_ref, zc_ref, zs_ref, zw_ref, gt_ref, ge_ref,
                  w_ref, x_ref, lw_ref, lb_ref, o_ref):
    gates = _sigmoid(gt_ref[...])
    gh, gl = _split2(gates)
    acc = None
    for n, (o_r, z_r) in enumerate(((oc_ref, zc_ref), (os_ref, zs_ref), (ow_ref, zw_ref))):
        ex = ge_ref[n]
        gate = _dot(gh, ex) + _dot(gl, ex)
        o = jnp.concatenate([o_r[0, h] for h in range(NSA_HEADS)], axis=-1).astype(F32)
        term = gate * o * _silu(z_r[...])
        acc = term if acc is None else acc + term
    y = _dot(acc.astype(BF16), w_ref[...])
    r = DEEPNORM_ALPHA * x_ref[...] + y
    o_ref[...] = _layer_norm_rows(r, lw_ref[...], lb_ref[...])


def _merge_out(o_cmp, o_sel, o_win, proj, tail, gate_expand, w_out, x_res, ln_w, ln_b, t_len, tm):
    m, d = x_res.shape
    width = NSA_HEADS * NSA_DH
    nt = t_len // tm
    ospec = pl.BlockSpec((1, NSA_HEADS, tm, NSA_DH), lambda i: (i // nt, 0, i % nt, 0))
    zspec = lambda col: pl.BlockSpec((tm, width), lambda i: (i, col))
    fixed2 = lambda i: (0, 0)
    return pl.pallas_call(
        _merge_kernel,
        grid=(m // tm,),
        in_specs=[ospec, ospec, ospec, zspec(0), zspec(1), zspec(2),
                  pl.BlockSpec((tm, LANE), lambda i: (i, 0)),
                  pl.BlockSpec((3, LANE, width), lambda i: (0, 0, 0)),
                  pl.BlockSpec((width, d), fixed2),
                  pl.BlockSpec((tm, d), lambda i: (i, 0)),
                  pl.BlockSpec((1, d), fixed2), pl.BlockSpec((1, d), fixed2)],
        out_specs=pl.BlockSpec((tm, d), lambda i: (i, 0)),
        out_shape=jax.ShapeDtypeStruct((m, d), F32),
        compiler_params=_cparams("parallel"),
        name="merge_out_proj_layernorm",
    )(o_cmp, o_sel, o_win, proj, proj, proj, tail, gate_expand, w_out, x_res,
      ln_w.reshape(1, d), ln_b.reshape(1, d))


def _pad_cols(w, n):
    return jnp.pad(w, ((0, 0), (0, n - w.shape[1])))


def kernel(x, a_w_in, a_conv_w, a_a_log, a_dt_bias, a_norm_w, a_w_out, a_ln_w, a_ln_b,
           s_w_kv, s_pe_k, s_pe_v, s_w1_k, s_w2_k, s_w1_v, s_w2_v,
           b_w_in, b_w_out, b_ln_w, b_ln_b):
    b_sz, t_len, d = x.shape
    m = b_sz * t_len
    assert t_len % SEL_KV_TILE == 0 and t_len >= WINDOW + Q_BLOCK and d % LANE == 0
    assert a_w_in.shape[0] == 1 and b_w_in.shape[0] == 1
    tm_mm = min(1024, m)
    tm_row = min(512, t_len)

    x2 = x.reshape(m, d)
    x16 = x2.astype(BF16)

    dn_width = DN_HEADS * DN_HEAD
    w_in = a_w_in[0]
    qkv = _gdn_qkv(x16, w_in[:, :3 * dn_width].astype(BF16), a_conv_w[0], t_len, tm_row)
    z_gdn = _matmul(x16, w_in[:, 3 * dn_width:4 * dn_width].astype(BF16), F32, tm_mm, 512)
    tail = _matmul(x16, _pad_cols(w_in[:, 4 * dn_width:], LANE).astype(BF16), F32, tm_mm, LANE)
    gates_col = _gdn_gates(tail, a_a_log[0], a_dt_bias[0], tm_mm)
    nc = t_len // GDN_CHUNK
    gates_row = gates_col[:, :2 * DN_HEADS].reshape(b_sz, nc, GDN_CHUNK, 2 * DN_HEADS).transpose(0, 1, 3, 2)
    o_gdn = _gdn_chunks(qkv, z_gdn, gates_col, gates_row, a_norm_w[0], b_sz, t_len)
    h1, h1_16 = _out_proj_ln(o_gdn, a_w_out[0].astype(BF16), x2, a_ln_w[0], a_ln_b[0], tm_row)

    pos = jnp.arange(t_len, dtype=jnp.int32)
    cos_t, sin_t = _rope_tables(pos)
    kv_cmp_raw, k_sel, v_sel, k_win, v_win = _kv_proj(h1_16, s_w_kv.astype(BF16), cos_t, sin_t,
                                                      b_sz, t_len, tm_row)

    n_chunk = t_len // CMP_STRIDE
    halves = CMP_BLOCK // CMP_STRIDE
    pe = jnp.stack([s_pe_k, s_pe_v]).astype(F32).reshape(2, halves, CMP_STRIDE, 1, NSA_DH)
    pe = jnp.broadcast_to(pe, (2, halves, CMP_STRIDE, 2, NSA_DH)).reshape(2, halves, CMP_STRIDE * LANE)
    w1 = jnp.stack([s_w1_k, s_w1_v]).reshape(2, halves, CMP_STRIDE, NSA_DH, CMP_HIDDEN)
    w1 = jnp.einsum('ab,shtdk->shtadbk', jnp.eye(2, dtype=w1.dtype), w1)
    w1 = w1.reshape(2, halves, CMP_STRIDE * LANE, 2 * CMP_HIDDEN).astype(BF16)
    w2 = jnp.stack([_pad_cols(s_w2_k, LANE), _pad_cols(s_w2_v, LANE)]).astype(BF16)
    cmp_end = jnp.arange(n_chunk, dtype=jnp.int32) * CMP_STRIDE + CMP_BLOCK - 1
    cos_c, sin_c = _rope_tables(cmp_end)
    kv_cmp = _compress(kv_cmp_raw, pe, w1, w2, cos_c, sin_c, b_sz, t_len)
    k_cmp, v_cmp = kv_cmp[0], kv_cmp[1]

    nsa_width = NSA_HEADS * NSA_DH
    w_in_b = b_w_in[0]
    q = _q_proj(h1_16, w_in_b[:, :nsa_width].astype(BF16), cos_t, sin_t, b_sz, t_len, tm_row)
    z_b = _matmul(h1_16, w_in_b[:, nsa_width:4 * nsa_width].astype(BF16), F32, tm_mm, 512)
    tail_b = _matmul(h1_16, _pad_cols(w_in_b[:, 4 * nsa_width:], LANE).astype(BF16), F32, tm_mm, LANE)

    n_sel = t_len // SEL_BLOCK
    c_start = jnp.arange(n_chunk, dtype=jnp.int32)[:, None] * CMP_STRIDE
    s_start = jnp.arange(n_sel, dtype=jnp.int32)[None, :] * SEL_BLOCK
    ov = jnp.clip(jnp.minimum(c_start + CMP_BLOCK, s_start + SEL_BLOCK) - jnp.maximum(c_start, s_start), 0, None)
    overlap = (ov.astype(F32) / CMP_BLOCK).astype(BF16)
    overlap_t = jnp.pad(overlap.T, ((0, LANE - n_sel), (0, 0)))

    o_cmp, sel_mask = _cmp_attention(q, k_cmp, v_cmp, overlap_t, t_len)
    o_sel = _sel_attention(q, k_sel, v_sel, sel_mask, t_len)
    o_win = _win_attention(q, k_win, v_win, t_len)

    lane_id = jnp.arange(LANE, dtype=jnp.int32)[None, :, None]
    col_head = (jnp.arange(NSA_HEADS * NSA_DH, dtype=jnp.int32) // NSA_DH)[None, None, :]
    branch = jnp.arange(3, dtype=jnp.int32)[:, None, None]
    gate_expand = (lane_id == branch * NSA_HEADS + col_head).astype(BF16)
    out = _merge_out(o_cmp, o_sel, o_win, z_b, tail_b, gate_expand, b_w_out[0].astype(BF16),
                     h1, b_ln_w[0], b_ln_b[0], t_len, min(256, t_len))
    return out.reshape(b_sz, t_len, d)
```

```python
import functools

import jax
import jax.numpy as jnp
from jax import lax
from jax.experimental import pallas as pl
from jax.experimental.pallas import tpu as pltpu

F32 = jnp.float32
BF16 = jnp.bfloat16

DN_HEADS = 8
DN_HEAD = 128
DN_CONV = 4
NSA_HEADS = 16
NSA_GROUPS = 4
NSA_REP = NSA_HEADS // NSA_GROUPS
NSA_DH = 64
CMP_BLOCK = 32
CMP_STRIDE = 16
CMP_HIDDEN = 128
SEL_BLOCK = 64
SEL_TOPK = 16
WINDOW = 512
Q_BLOCK = 128
FORCED_SCORE = 1.0e4
ROPE_THETA = 10000.0
NORM_EPS = 1e-6
DEPTH = 2
DEEPNORM_ALPHA = (2.0 * DEPTH) ** 0.25

GDN_CHUNK = 128
INV_BASE = 16
GDN_SEQS = 2
LANE = 128
SEL_KV_TILE = 512
SEL_Q_TILE = 512
CMP_Q_BLOCKS = 4
WIN_Q_BLOCKS = 4
NEG_BIG = -1e30
LOG2_E = 1.4426950408889634
VMEM_LIMIT = 56 * 1024 * 1024


def _cparams(*sem):
    return pltpu.CompilerParams(dimension_semantics=sem, vmem_limit_bytes=VMEM_LIMIT)


def _dot(a, b):
    return jnp.dot(a, b, preferred_element_type=F32)


def _dot_nt(a, b):
    return lax.dot_general(a, b, (((1,), (1,)), ((), ())), preferred_element_type=F32)


def _dot_tn(a, b):
    return lax.dot_general(a, b, (((0,), (0,)), ((), ())), preferred_element_type=F32)


def _split2(a):
    hi = a.astype(BF16)
    lo = (a - hi.astype(F32)).astype(BF16)
    return hi, lo


def _split3(a):
    hi = a.astype(BF16)
    r = a - hi.astype(F32)
    mid = r.astype(BF16)
    lo = (r - mid.astype(F32)).astype(BF16)
    return hi, mid, lo


def _mm(a, b):
    return _dot(a.astype(BF16), b.astype(BF16))


def _blk(idx, size):
    return idx >> (size.bit_length() - 1)


def _silu(x):
    return x / (1.0 + jnp.exp(-x))


def _sigmoid(x):
    return 1.0 / (1.0 + jnp.exp(-x))


def _mm_kernel(x_ref, w_ref, o_ref):
    o_ref[...] = _dot(x_ref[...], w_ref[...]).astype(o_ref.dtype)


def _matmul(x, w, out_dtype, tm, tn):
    m, k = x.shape
    n = w.shape[1]
    return pl.pallas_call(
        _mm_kernel,
        grid=(m // tm, n // tn),
        in_specs=[pl.BlockSpec((tm, k), lambda i, j: (i, 0)),
                  pl.BlockSpec((k, tn), lambda i, j: (0, j))],
        out_specs=pl.BlockSpec((tm, tn), lambda i, j: (i, j)),
        out_shape=jax.ShapeDtypeStruct((m, n), out_dtype),
        compiler_params=_cparams("parallel", "parallel"),
        name="proj_matmul",
    )(x, w)


def _layer_norm_rows(r, w, b):
    mu = jnp.mean(r, axis=-1, keepdims=True)
    d = r - mu
    var = jnp.mean(d * d, axis=-1, keepdims=True)
    return d * lax.rsqrt(var + NORM_EPS) * w + b


def _out_ln_kernel(a_ref, w_ref, x_ref, lw_ref, lb_ref, o_ref, ob_ref):
    y = _dot(a_ref[...], w_ref[...])
    r = DEEPNORM_ALPHA * x_ref[...] + y
    out = _layer_norm_rows(r, lw_ref[...], lb_ref[...])
    o_ref[...] = out
    ob_ref[...] = out.astype(BF16)


def _out_proj_ln(a, w, x_res, ln_w, ln_b, tm):
    m, k = a.shape
    n = w.shape[1]
    row = lambda i: (i, 0)
    fixed = lambda i: (0, 0)
    return pl.pallas_call(
        _out_ln_kernel,
        grid=(m // tm,),
        in_specs=[pl.BlockSpec((tm, k), row), pl.BlockSpec((k, n), fixed),
                  pl.BlockSpec((tm, n), row), pl.BlockSpec((1, n), fixed),
                  pl.BlockSpec((1, n), fixed)],
        out_specs=[pl.BlockSpec((tm, n), row), pl.BlockSpec((tm, n), row)],
        out_shape=[jax.ShapeDtypeStruct((m, n), F32), jax.ShapeDtypeStruct((m, n), BF16)],
        compiler_params=_cparams("parallel"),
        name="out_proj_layernorm",
    )(a, w, x_res, ln_w.reshape(1, n), ln_b.reshape(1, n))


def _gdn_qkv_kernel(x_ref, w_ref, cw_ref, o_ref, halo_ref, *, tm, t_len):
    j = pl.program_id(0)
    i = pl.program_id(1)
    @pl.when((i * tm) % t_len == 0)
    def _():
        halo_ref[...] = jnp.zeros(halo_ref.shape, F32)

    p = _dot(x_ref[...], w_ref[...])
    halo = halo_ref[...]
    halo_ref[...] = p[tm - 8:tm, :]
    cw = cw_ref[...]
    row = lax.broadcasted_iota(jnp.int32, halo.shape, 0)
    y = p * cw[DN_CONV - 1:DN_CONV, :]
    for back in range(1, DN_CONV):
        shifted = pltpu.roll(p, back, axis=0)
        head = jnp.where(row < back, pltpu.roll(halo, back, axis=0), shifted[0:8, :])
        shifted = jnp.concatenate([head, shifted[8:, :]], axis=0)
        y = y + shifted * cw[DN_CONV - 1 - back:DN_CONV - back, :]
    y = _silu(y)
    qscale = jnp.where(j == 0, DN_HEAD ** -0.5, 1.0)
    for h in range(DN_HEADS):
        sl = slice(h * DN_HEAD, (h + 1) * DN_HEAD)
        yh = y[:, sl]
        ss = jnp.sum(yh * yh, axis=-1, keepdims=True)
        scale = jnp.where(j < 2, lax.rsqrt(ss + NORM_EPS) * qscale, 1.0)
        o_ref[:, sl] = yh * scale


def _gdn_qkv(x16, w_qkv, conv_w, t_len, tm):
    m, d = x16.shape
    width = DN_HEADS * DN_HEAD
    kern = functools.partial(_gdn_qkv_kernel, tm=tm, t_len=t_len)
    return pl.pallas_call(
        kern,
        grid=(3, m // tm),
        in_specs=[pl.BlockSpec((tm, d), lambda j, i: (i, 0)),
                  pl.BlockSpec((d, width), lambda j, i: (0, j)),
                  pl.BlockSpec((DN_CONV, width), lambda j, i: (0, j))],
        out_specs=pl.BlockSpec((tm, width), lambda j, i: (i, j)),
        out_shape=jax.ShapeDtypeStruct((m, 3 * width), F32),
        scratch_shapes=[pltpu.VMEM((8, width), F32)],
        compiler_params=_cparams("parallel", "arbitrary"),
        name="gdn_qkv_proj_conv",
    )(x16, w_qkv, conv_w)


def _gdn_gate_kernel(t_ref, al_ref, dt_ref, o_ref):
    x = t_ref[...]
    lane = lax.broadcasted_iota(jnp.int32, x.shape, 1)
    beta = _sigmoid(x)
    z = x + dt_ref[...]
    softplus = jnp.maximum(z, 0.0) + jnp.log(1.0 + jnp.exp(-jnp.abs(z)))
    g = -jnp.exp(al_ref[...]) * softplus
    o_ref[...] = jnp.where(lane < DN_HEADS, beta, jnp.where(lane < 2 * DN_HEADS, g, 0.0))


def _gdn_gates(tail, a_log, dt_bias, tm):
    m = tail.shape[0]
    pad = lambda v: jnp.zeros((1, LANE), F32).at[0, DN_HEADS:2 * DN_HEADS].set(v.astype(F32))
    return pl.pallas_call(
        _gdn_gate_kernel,
        grid=(m // tm,),
        in_specs=[pl.BlockSpec((tm, LANE), lambda i: (i, 0)),
                  pl.BlockSpec((1, LANE), lambda i: (0, 0)),
                  pl.BlockSpec((1, LANE), lambda i: (0, 0))],
        out_specs=pl.BlockSpec((tm, LANE), lambda i: (i, 0)),
        out_shape=jax.ShapeDtypeStruct((m, LANE), F32),
        compiler_params=_cparams("parallel"),
        name="gdn_gates",
    )(tail, pad(a_log), pad(dt_bias))


def _inv_unit_lower(lows, ii, jj):
    c = lows[0].shape[0]
    eye = (ii == jj).astype(F32)
    same = _blk(ii, INV_BASE) == _blk(jj, INV_BASE)
    ds = [jnp.where(same, low, 0.0) for low in lows]
    xs = [eye - d for d in ds]
    ps = ds
    span = 2
    while span < INV_BASE:
        ps = [_mm(p, p) for p in ps]
        xs = [_mm(x, eye + p) for x, p in zip(xs, ps)]
        span *= 2
    s = INV_BASE
    while s < c:
        off = (_blk(ii, 2 * s) == _blk(jj, 2 * s)) & (_blk(ii, s) != _blk(jj, s))
        ts = [_mm(x, jnp.where(off, low, 0.0)) for x, low in zip(xs, lows)]
        xs = [x - _mm(t, x) for x, t in zip(xs, ts)]
        s *= 2
    return xs


def _gdn_chunk_kernel(q_ref, k_ref, v_ref, z_ref, gc_ref, gr_ref, nw_ref, o_ref, s_ref):
    c = GDN_CHUNK

    @pl.when(pl.program_id(1) == 0)
    def _():
        s_ref[...] = jnp.zeros_like(s_ref)

    ii = lax.broadcasted_iota(jnp.int32, (c, c), 0)
    jj = lax.broadcasted_iota(jnp.int32, (c, c), 1)
    incl = ii >= jj
    strict = ii > jj
    ltri = incl.astype(BF16)
    utri = (ii <= jj).astype(BF16)
    nb = q_ref.shape[0]
    gcols, gcs_cols, gcs_rows = [], [], []
    for bb in range(nb):
        gcol = gc_ref[bb]
        gc3 = _split3(gcol)
        gr3 = _split3(gr_ref[bb, 0])
        gcols.append(gcol)
        gcs_cols.append(_dot(ltri, gc3[0]) + (_dot(ltri, gc3[1]) + _dot(ltri, gc3[2])))
        gcs_rows.append(_dot(gr3[0], utri) + (_dot(gr3[1], utri) + _dot(gr3[2], utri)))
    nw = nw_ref[...]

    chains = [(bb, h) for bb in range(nb) for h in range(DN_HEADS)]
    tile = lambda ref, bb, h: ref[bb, :, h * DN_HEAD:(h + 1) * DN_HEAD]
    betas = [gcols[bb][:, h:h + 1] for bb, h in chains]
    gccs = [gcs_cols[bb][:, DN_HEADS + h:DN_HEADS + h + 1] for bb, h in chains]
    gcrs = [gcs_rows[bb][DN_HEADS + h:DN_HEADS + h + 1, :] for bb, h in chains]
    decays = [jnp.where(incl, jnp.exp(jnp.where(incl, gcc - gcr, 0.0)), 0.0)
              for gcc, gcr in zip(gccs, gcrs)]
    kbs = [tile(k_ref, bb, h) * beta for (bb, h), beta in zip(chains, betas)]
    k16s = [tile(k_ref, bb, h).astype(BF16) for bb, h in chains]
    lows = [jnp.where(strict, _dot_nt(kb.astype(BF16), k16) * decay, 0.0)
            for kb, k16, decay in zip(kbs, k16s, decays)]
    attns = [jnp.where(incl, _dot_nt(tile(q_ref, bb, h).astype(BF16), k16) * decay, 0.0).astype(BF16)
             for (bb, h), k16, decay in zip(chains, k16s, decays)]
    tinvs = _inv_unit_lower(lows, ii, jj)
    egs = [jnp.exp(gcc) for gcc in gccs]
    sols = [_mm(tinv, jnp.concatenate([tile(v_ref, bb, h) * beta, kb * eg], axis=1))
            for tinv, (bb, h), beta, kb, eg in zip(tinvs, chains, betas, kbs, egs)]
    g_lasts = [gcc[c - 1:c, :] for gcc in gccs]
    kds = [(tile(k_ref, bb, h) * jnp.exp(g_last - gcc)).astype(BF16)
           for (bb, h), g_last, gcc in zip(chains, g_lasts, gccs)]
    qgs = [(tile(q_ref, bb, h) * eg).astype(BF16) for (bb, h), eg in zip(chains, egs)]
    states = [s_ref[bb * DN_HEADS + h] for bb, h in chains]
    s16s = [state.astype(BF16) for state in states]
    vn16s = [(sol[:, :DN_HEAD] - _dot(sol[:, DN_HEAD:].astype(BF16), s16)).astype(BF16)
             for sol, s16 in zip(sols, s16s)]
    outs = [_dot(qg, s16) + _dot(attn, vn16)
            for qg, s16, attn, vn16 in zip(qgs, s16s, attns, vn16s)]
    for n, (bb, h) in enumerate(chains):
        s_ref[bb * DN_HEADS + h] = states[n] * jnp.exp(g_lasts[n]) + _dot_tn(kds[n], vn16s[n])
    for n, (bb, h) in enumerate(chains):
        o = outs[n]
        on = o * lax.rsqrt(jnp.mean(o * o, axis=-1, keepdims=True) + NORM_EPS) * nw
        o_ref[bb, :, h * DN_HEAD:(h + 1) * DN_HEAD] = (on * _silu(tile(z_ref, bb, h))).astype(o_ref.dtype)


def _gdn_chunks(qkv, z, gates_col, gates_row, norm_w, b_sz, t_len):
    c = GDN_CHUNK
    nc = t_len // c
    width = DN_HEADS * DN_HEAD
    nb = GDN_SEQS if b_sz % GDN_SEQS == 0 else 1
    seq = lambda a: a.reshape(b_sz, t_len, a.shape[-1])
    tok = lambda col: (lambda b, i: (b, i, col))
    out = pl.pallas_call(
        _gdn_chunk_kernel,
        grid=(b_sz // nb, nc),
        in_specs=[pl.BlockSpec((nb, c, width), tok(0)), pl.BlockSpec((nb, c, width), tok(1)),
                  pl.BlockSpec((nb, c, width), tok(2)), pl.BlockSpec((nb, c, width), tok(0)),
                  pl.BlockSpec((nb, c, LANE), tok(0)),
                  pl.BlockSpec((nb, 1, 2 * DN_HEADS, c), lambda b, i: (b, i, 0, 0)),
                  pl.BlockSpec((1, DN_HEAD), lambda b, i: (0, 0))],
        out_specs=pl.BlockSpec((nb, c, width), tok(0)),
        out_shape=jax.ShapeDtypeStruct((b_sz, t_len, width), BF16),
        scratch_shapes=[pltpu.VMEM((nb * DN_HEADS, DN_HEAD, DN_HEAD), F32)],
        compiler_params=_cparams("parallel", "arbitrary"),
        name="gdn_delta_rule",
    )(seq(qkv), seq(qkv), seq(qkv), seq(z), seq(gates_col), gates_row,
      norm_w.reshape(1, DN_HEAD).astype(F32))
    return out.reshape(b_sz * t_len, width)


def _rope_tables(pos):
    half = NSA_DH // 2
    inv_freq = ROPE_THETA ** (-jnp.arange(half, dtype=F32) / half)
    ang = pos.astype(F32)[:, None] * inv_freq[None, :]
    cos, sin = jnp.cos(ang), jnp.sin(ang)
    cos_t = jnp.concatenate([cos, cos, cos, cos], axis=-1)
    sin_t = jnp.concatenate([-sin, sin, -sin, sin], axis=-1)
    return cos_t, sin_t


def _rope128(x, cos_t, sin_t):
    lane = lax.broadcasted_iota(jnp.int32, x.shape, 1)
    lower = (lane & (NSA_DH - 1)) < (NSA_DH // 2)
    fwd = pltpu.roll(x, LANE - NSA_DH // 2, axis=1)
    bwd = pltpu.roll(x, NSA_DH // 2, axis=1)
    return x * cos_t + jnp.where(lower, fwd, bwd) * sin_t


def _kv_proj_kernel(x_ref, w_ref, cos_ref, sin_ref, cmp_ref, ks_ref, vs_ref, kw_ref, vw_ref, *, tm, nt):
    gw = NSA_GROUPS * NSA_DH
    kv = _dot(x_ref[...], w_ref[...])
    cmp_ref[...] = kv[:, 0:2 * gw]
    cos_t = cos_ref[...]
    sin_t = sin_ref[...]
    t_pos = (pl.program_id(0) % nt) * tm + lax.broadcasted_iota(jnp.int32, (tm, LANE), 0)
    lane = lax.broadcasted_iota(jnp.int32, (tm, LANE), 1)
    onehot = (_blk(t_pos, SEL_BLOCK) == lane).astype(BF16)
    ones_col = (lane == NSA_DH).astype(BF16)
    zeros_half = jnp.zeros((tm, NSA_DH), BF16)
    for pair in range(NSA_GROUPS // 2):
        cols = lambda src: kv[:, src * gw + pair * LANE: src * gw + (pair + 1) * LANE]
        k_sel = _rope128(cols(2), cos_t, sin_t).astype(BF16)
        k_win = _rope128(cols(4), cos_t, sin_t).astype(BF16)
        v_sel = cols(3).astype(BF16)
        v_win = cols(5).astype(BF16)
        for half in range(2):
            g = 2 * pair + half
            sl = slice(half * NSA_DH, (half + 1) * NSA_DH)
            ks_ref[0, g, :, 0:LANE] = onehot
            ks_ref[0, g, :, LANE:LANE + NSA_DH] = k_sel[:, sl]
            ks_ref[0, g, :, LANE + NSA_DH:2 * LANE] = zeros_half
            kw_ref[0, g] = k_win[:, sl]
            for val, dst in ((v_sel, vs_ref), (v_win, vw_ref)):
                dst[0, g] = ones_col
                dst[0, g, :, 0:NSA_DH] = val[:, sl]


def _kv_proj(h16, w_kv, cos_t, sin_t, b_sz, t_len, tm):
    nt = t_len // tm
    m, d = h16.shape
    gw = NSA_GROUPS * NSA_DH
    shape = lambda w: jax.ShapeDtypeStruct((b_sz, NSA_GROUPS, t_len, w), BF16)
    ospec = lambda w: pl.BlockSpec((1, NSA_GROUPS, tm, w), lambda i: (i // nt, 0, i % nt, 0))
    return pl.pallas_call(
        functools.partial(_kv_proj_kernel, tm=tm, nt=nt),
        grid=(b_sz * nt,),
        in_specs=[pl.BlockSpec((tm, d), lambda i: (i, 0)),
                  pl.BlockSpec((d, 6 * gw), lambda i: (0, 0)),
                  pl.BlockSpec((tm, LANE), lambda i: (i % nt, 0)),
                  pl.BlockSpec((tm, LANE), lambda i: (i % nt, 0))],
        out_specs=[pl.BlockSpec((tm, 2 * gw), lambda i: (i, 0)),
                   ospec(2 * LANE), ospec(LANE), ospec(NSA_DH), ospec(LANE)],
        out_shape=[jax.ShapeDtypeStruct((m, 2 * gw), F32),
                   shape(2 * LANE), shape(LANE), shape(NSA_DH), shape(LANE)],
        compiler_params=_cparams("parallel"),
        name="kv_proj_rope_layout",
    )(h16, w_kv, cos_t, sin_t)


def _q_proj_kernel(x_ref, w_ref, cos_ref, sin_ref, o_ref):
    q = _dot(x_ref[...], w_ref[...])
    cos_t = cos_ref[...]
    sin_t = sin_ref[...]
    scale = NSA_DH ** -0.5 * LOG2_E
    for pair in range(NSA_HEADS // 2):
        x = _rope128(q[:, pair * LANE:(pair + 1) * LANE], cos_t, sin_t) * scale
        o_ref[0, 2 * pair] = x[:, :NSA_DH].astype(BF16)
        o_ref[0, 2 * pair + 1] = x[:, NSA_DH:].astype(BF16)


def _q_proj(h16, w_q, cos_t, sin_t, b_sz, t_len, tm):
    nt = t_len // tm
    m, d = h16.shape
    width = NSA_HEADS * NSA_DH
    return pl.pallas_call(
        _q_proj_kernel,
        grid=(b_sz * nt,),
        in_specs=[pl.BlockSpec((tm, d), lambda i: (i, 0)),
                  pl.BlockSpec((d, width), lambda i: (0, 0)),
                  pl.BlockSpec((tm, LANE), lambda i: (i % nt, 0)),
                  pl.BlockSpec((tm, LANE), lambda i: (i % nt, 0))],
        out_specs=pl.BlockSpec((1, NSA_HEADS, tm, NSA_DH), lambda i: (i // nt, 0, i % nt, 0)),
        out_shape=jax.ShapeDtypeStruct((b_sz, NSA_HEADS, t_len, NSA_DH), BF16),
        compiler_params=_cparams("parallel"),
        name="q_proj_rope_layout",
    )(h16, w_q, cos_t, sin_t)


def _compress_kernel(x_ref, pe_ref, w1_ref, w2_ref, cos_ref, sin_ref, o_ref, *, n_chunk):
    is_key = pl.program_id(0) == 0
    chunk = jnp.concatenate([x_ref[pl.ds(tau, n_chunk, stride=CMP_STRIDE), :]
                             for tau in range(CMP_STRIDE)], axis=1)
    a = _dot((chunk + pe_ref[0, 0:1, :]).astype(BF16), w1_ref[0, 0])
    b = _dot((chunk + pe_ref[0, 1:2, :]).astype(BF16), w1_ref[0, 1])
    hid = a + pltpu.roll(b, n_chunk - 1, axis=0)
    act = _silu(hid).astype(BF16)
    for half in range(2):
        y = _dot(act[:, half * CMP_HIDDEN:(half + 1) * CMP_HIDDEN], w2_ref[0])
        y = jnp.where(is_key, _rope128(y, cos_ref[...], sin_ref[...]), y)
        o_ref[0, 0, half] = y[:, :NSA_DH].astype(BF16)


def _compress(kv_raw, pe, w1, w2, cos_c, sin_c, b_sz, t_len):
    n_chunk = t_len // CMP_STRIDE
    pairs = NSA_GROUPS // 2
    flat = CMP_STRIDE * LANE
    return pl.pallas_call(
        functools.partial(_compress_kernel, n_chunk=n_chunk),
        grid=(2, b_sz, pairs),
        in_specs=[pl.BlockSpec((t_len, LANE), lambda s, b, p: (b, s * pairs + p)),
                  pl.BlockSpec((1, 2, flat), lambda s, b, p: (s, 0, 0)),
                  pl.BlockSpec((1, 2, flat, 2 * CMP_HIDDEN), lambda s, b, p: (s, 0, 0, 0)),
                  pl.BlockSpec((1, CMP_HIDDEN, LANE), lambda s, b, p: (s, 0, 0)),
                  pl.BlockSpec((n_chunk, LANE), lambda s, b, p: (0, 0)),
                  pl.BlockSpec((n_chunk, LANE), lambda s, b, p: (0, 0))],
        out_specs=pl.BlockSpec((1, 1, 2, n_chunk, NSA_DH), lambda s, b, p: (s, b, p, 0, 0)),
        out_shape=jax.ShapeDtypeStruct((2, b_sz, NSA_GROUPS, n_chunk, NSA_DH), BF16),
        compiler_params=_cparams("parallel", "parallel", "parallel"),
        name="compress_blocks",
    )(kv_raw, pe, w1, w2, cos_c, sin_c)


def _cmp_kernel(q_ref, k_ref, v_ref, ovt_ref, o_ref, sel_ref, *, n_sel):
    step = pl.program_id(2)
    n_cmp = k_ref.shape[2]
    steps_per_group = LANE * CMP_STRIDE // (Q_BLOCK * CMP_Q_BLOCKS)
    n_var = max(1, n_cmp // LANE)
    for var in range(n_var):
        n_col = min(n_cmp, LANE * (var + 1))
        n_blk = min(n_sel, n_col * CMP_STRIDE // SEL_BLOCK)
        pl.when(step // steps_per_group == var)(functools.partial(
            _cmp_body, step, q_ref, k_ref, v_ref, ovt_ref, o_ref, sel_ref, n_col, n_blk))


def _cmp_body(step, q_ref, k_ref, v_ref, ovt_ref, o_ref, sel_ref, n_cmp, n_sel):
    tq = Q_BLOCK
    rows = NSA_REP * tq
    n_pad = -(-n_sel // 8) * 8
    blocks = range(CMP_Q_BLOCKS)
    t0s = [(step * CMP_Q_BLOCKS + x) * tq for x in blocks]
    kc = k_ref[0, 0, 0:n_cmp, :]
    ss = [_dot_nt(q_ref[0, :, x * tq:(x + 1) * tq, :].reshape(rows, NSA_DH), kc) for x in blocks]
    i_row = lax.broadcasted_iota(jnp.int32, (rows, n_cmp), 0) & (tq - 1)
    end = lax.broadcasted_iota(jnp.int32, (rows, n_cmp), 1) * CMP_STRIDE + (CMP_BLOCK - 1)
    ss = [jnp.where(end <= t0 + i_row, s, NEG_BIG) for s, t0 in zip(ss, t0s)]
    ms = [jnp.max(s, axis=-1, keepdims=True) for s in ss]
    es = [jnp.exp2(s - m) for s, m in zip(ss, ms)]
    ls = [jnp.sum(e, axis=-1, keepdims=True) for e in es]
    i_col = lax.broadcasted_iota(jnp.int32, (rows, 1), 0) & (tq - 1)
    invs = [jnp.where(t0 + i_col >= CMP_BLOCK - 1, 1.0 / l, 0.0) for t0, l in zip(t0s, ls)]
    ps = [e * inv for e, inv in zip(es, invs)]
    vc = v_ref[0, 0, 0:n_cmp, :]
    for x in blocks:
        o = _dot(ps[x].astype(BF16), vc)
        o_ref[0, :, x * tq:(x + 1) * tq, :] = o.reshape(NSA_REP, tq, NSA_DH).astype(o_ref.dtype)

    ovt = ovt_ref[0:n_pad, 0:n_cmp]
    imps = []
    for p in ps:
        psum = p[0:tq]
        for r in range(1, NSA_REP):
            psum = psum + p[r * tq:(r + 1) * tq]
        ph, plo = _split2(psum)
        imps.append(_dot_nt(ovt, ph) + _dot_nt(ovt, plo))

    i_q = lax.broadcasted_iota(jnp.int32, (n_pad, tq), 1)
    blk = lax.broadcasted_iota(jnp.int32, (n_pad, tq), 0)
    blk_f = blk.astype(F32)
    scores = []
    for t0, imp in zip(t0s, imps):
        t_q = t0 + i_q
        cur = _blk(t_q, SEL_BLOCK)
        forced = (blk == 0) | (blk == cur) | (blk == cur - 1)
        score = jnp.where(blk * SEL_BLOCK <= t_q, jnp.where(forced, FORCED_SCORE, imp), -1.0)
        if n_pad > n_sel:
            score = jnp.where(blk < n_sel, score, -2.0)
        scores.append(score)
    taken = -float(2 ** 127)
    for _ in range(min(SEL_TOPK, n_sel)):
        mxs = [jnp.max(score, axis=0, keepdims=True) for score in scores]
        firsts = [jnp.min(jnp.where(score == mx, blk_f, float(LANE)), axis=0, keepdims=True)
                  for score, mx in zip(scores, mxs)]
        scores = [jnp.where(blk_f == first, taken, score)
                  for score, first in zip(scores, firsts)]
    for x in blocks:
        chosen = jnp.where(scores[x] == taken, 0.0, NEG_BIG)
        if n_pad < LANE:
            chosen = jnp.concatenate([chosen, jnp.full((LANE - n_pad, tq), NEG_BIG, F32)], axis=0)
        sel_ref[0, 0, x * tq:(x + 1) * tq, :] = chosen.T.astype(sel_ref.dtype)


def _cmp_attention(q, k_cmp, v_cmp, overlap_t, t_len):
    b_sz = q.shape[0]
    n_cmp = k_cmp.shape[2]
    tq = Q_BLOCK * CMP_Q_BLOCKS
    nq = t_len // tq
    return pl.pallas_call(
        functools.partial(_cmp_kernel, n_sel=t_len // SEL_BLOCK),
        grid=(b_sz, NSA_GROUPS, nq),
        in_specs=[pl.BlockSpec((1, NSA_REP, tq, NSA_DH), lambda b, g, i: (b, g, i, 0)),
                  pl.BlockSpec((1, 1, n_cmp, NSA_DH), lambda b, g, i: (b, g, 0, 0)),
                  pl.BlockSpec((1, 1, n_cmp, NSA_DH), lambda b, g, i: (b, g, 0, 0)),
                  pl.BlockSpec((LANE, n_cmp), lambda b, g, i: (0, 0))],
        out_specs=[pl.BlockSpec((1, NSA_REP, tq, NSA_DH), lambda b, g, i: (b, g, i, 0)),
                   pl.BlockSpec((1, 1, tq, LANE), lambda b, g, i: (b, g, i, 0))],
        out_shape=[jax.ShapeDtypeStruct((b_sz, NSA_HEADS, t_len, NSA_DH), BF16),
                   jax.ShapeDtypeStruct((b_sz, NSA_GROUPS, t_len, LANE), BF16)],
        compiler_params=_cparams("parallel", "parallel", "parallel"),
        name="cmp_attention_topk",
    )(q, k_cmp, v_cmp, overlap_t)


def _online_softmax_step(s, vt, m_ref, acc_ref):
    reps = s.shape[1] // LANE
    m_prev = m_ref[...]
    m_next = jnp.maximum(m_prev, jnp.max(s, axis=-1, keepdims=True))
    p = jnp.exp2(s - jnp.concatenate([m_next] * reps, axis=1))
    alpha = jnp.exp2(m_prev - m_next)
    acc_ref[...] = alpha * acc_ref[...] + _dot(p.astype(BF16), vt)
    m_ref[...] = m_next


def _sel_kernel(q_ref, k_ref, v_ref, sel_ref, o_ref, qa_ref, m_ref, acc_ref, sa_ref, sb_ref):
    qb = pl.program_id(2)
    tq = SEL_Q_TILE
    tk = SEL_KV_TILE
    rows = NSA_REP * tq
    mask = sel_ref[0, 0]
    for r in range(NSA_REP):
        qa_ref[r * tq:(r + 1) * tq, 0:LANE] = mask
        qa_ref[r * tq:(r + 1) * tq, LANE:LANE + NSA_DH] = q_ref[0, r]
        qa_ref[r * tq:(r + 1) * tq, LANE + NSA_DH:2 * LANE] = jnp.zeros((tq, NSA_DH), BF16)
    m_ref[...] = jnp.full(m_ref.shape, -jnp.inf, F32)
    acc_ref[...] = jnp.zeros(acc_ref.shape, F32)
    last = (qb * tq) // tk

    def scores(j):
        k0 = pl.multiple_of(j * tk, tk)
        return _dot_nt(qa_ref[...], k_ref[0, 0, pl.ds(k0, tk), :])

    def values(j):
        return v_ref[0, 0, pl.ds(pl.multiple_of(j * tk, tk), tk), :]

    def causal_step(s):
        row = lax.broadcasted_iota(jnp.int32, (rows, tk), 0)
        kpos = last * tk + lax.broadcasted_iota(jnp.int32, (rows, tk), 1)
        s = jnp.where(kpos <= qb * tq + (row & (tq - 1)), s, NEG_BIG)
        _online_softmax_step(s, values(last), m_ref, acc_ref)

    sa_ref[...] = scores(0)

    def pair(i, carry):
        sb_ref[...] = scores(2 * i + 1)
        _online_softmax_step(sa_ref[...], values(2 * i), m_ref, acc_ref)
        sa_ref[...] = scores(2 * i + 2)
        _online_softmax_step(sb_ref[...], values(2 * i + 1), m_ref, acc_ref)
        return carry

    lax.fori_loop(0, last // 2, pair, 0)

    @pl.when(last % 2 == 0)
    def _():
        causal_step(sa_ref[...])

    @pl.when(last % 2 == 1)
    def _():
        sb_ref[...] = scores(last)
        _online_softmax_step(sa_ref[...], values(last - 1), m_ref, acc_ref)
        causal_step(sb_ref[...])

    acc = acc_ref[...]
    o = acc[:, 0:NSA_DH] / acc[:, NSA_DH:NSA_DH + 1]
    o_ref[0] = o.reshape(NSA_REP, tq, NSA_DH).astype(o_ref.dtype)


def _sel_attention(q, k_sel, v_sel, sel_mask, t_len):
    b_sz = q.shape[0]
    tq = SEL_Q_TILE
    nq = t_len // tq
    rows = NSA_REP * tq
    return pl.pallas_call(
        _sel_kernel,
        grid=(b_sz, NSA_GROUPS, nq),
        in_specs=[pl.BlockSpec((1, NSA_REP, tq, NSA_DH), lambda b, g, i: (b, g, i, 0)),
                  pl.BlockSpec((1, 1, t_len, 2 * LANE), lambda b, g, i: (b, g, 0, 0)),
                  pl.BlockSpec((1, 1, t_len, LANE), lambda b, g, i: (b, g, 0, 0)),
                  pl.BlockSpec((1, 1, tq, LANE), lambda b, g, i: (b, g, i, 0))],
        out_specs=pl.BlockSpec((1, NSA_REP, tq, NSA_DH), lambda b, g, i: (b, g, i, 0)),
        out_shape=jax.ShapeDtypeStruct((b_sz, NSA_HEADS, t_len, NSA_DH), BF16),
        scratch_shapes=[pltpu.VMEM((rows, 2 * LANE), BF16), pltpu.VMEM((rows, LANE), F32),
                        pltpu.VMEM((rows, LANE), F32), pltpu.VMEM((rows, SEL_KV_TILE), F32),
                        pltpu.VMEM((rows, SEL_KV_TILE), F32)],
        compiler_params=_cparams("parallel", "parallel", "arbitrary"),
        name="sel_attention",
    )(q, k_sel, v_sel, sel_mask)


def _win_body(step, q_ref, k_ref, v_ref, o_ref, interior):
    tq = Q_BLOCK
    span = WINDOW + tq
    rows = NSA_REP * tq
    blocks = range(WIN_Q_BLOCKS)
    qbs = [step * WIN_Q_BLOCKS + x for x in blocks]
    starts = [pl.multiple_of(jnp.maximum(qb * tq - WINDOW, 0), tq) for qb in qbs]
    ss = [_dot_nt(q_ref[0, :, x * tq:(x + 1) * tq, :].reshape(rows, NSA_DH),
                  k_ref[0, 0, pl.ds(starts[x], span), :]) for x in blocks]
    if interior:
        i_row = lax.broadcasted_iota(jnp.int32, (rows, tq), 0) & (tq - 1)
        col = lax.broadcasted_iota(jnp.int32, (rows, tq), 1)
        ss = [jnp.concatenate([jnp.where(col > i_row, s[:, 0:tq], NEG_BIG), s[:, tq:WINDOW],
                               jnp.where(col <= i_row, s[:, WINDOW:span], NEG_BIG)], axis=1) for s in ss]
    else:
        i_row = lax.broadcasted_iota(jnp.int32, (rows, span), 0) & (tq - 1)
        col = lax.broadcasted_iota(jnp.int32, (rows, span), 1)
        masked = []
        for x in blocks:
            t_q = qbs[x] * tq + i_row
            kpos = starts[x] + col
            masked.append(jnp.where((kpos <= t_q) & (kpos > t_q - WINDOW), ss[x], NEG_BIG))
        ss = masked
    ms = [jnp.max(s, axis=-1, keepdims=True) for s in ss]
    ps = [jnp.exp2(s - m).astype(BF16) for s, m in zip(ss, ms)]
    accs = [_dot(ps[x], v_ref[0, 0, pl.ds(starts[x], span), :]) for x in blocks]
    for x in blocks:
        o = accs[x][:, 0:NSA_DH] / accs[x][:, NSA_DH:NSA_DH + 1]
        o_ref[0, :, x * tq:(x + 1) * tq, :] = o.reshape(NSA_REP, tq, NSA_DH).astype(o_ref.dtype)


def _win_kernel(q_ref, k_ref, v_ref, o_ref):
    step = pl.program_id(2)
    edge_steps = -(-(WINDOW // Q_BLOCK) // WIN_Q_BLOCKS)
    pl.when(step < edge_steps)(functools.partial(_win_body, step, q_ref, k_ref, v_ref, o_ref, False))
    pl.when(step >= edge_steps)(functools.partial(_win_body, step, q_ref, k_ref, v_ref, o_ref, True))


def _win_attention(q, k_win, v_win, t_len):
    b_sz = q.shape[0]
    tq = Q_BLOCK * WIN_Q_BLOCKS
    nq = t_len // tq
    return pl.pallas_call(
        _win_kernel,
        grid=(b_sz, NSA_GROUPS, nq),
        in_specs=[pl.BlockSpec((1, NSA_REP, tq, NSA_DH), lambda b, g, i: (b, g, i, 0)),
                  pl.BlockSpec((1, 1, t_len, NSA_DH), lambda b, g, i: (b, g, 0, 0)),
                  pl.BlockSpec((1, 1, t_len, LANE), lambda b, g, i: (b, g, 0, 0))],
        out_specs=pl.BlockSpec((1, NSA_REP, tq, NSA_DH), lambda b, g, i: (b, g, i, 0)),
        out_shape=jax.ShapeDtypeStruct((b_sz, NSA_HEADS, t_len, NSA_DH), BF16),
        compiler_params=_cparams("parallel", "parallel", "parallel"),
        name="win_attention",
    )(q, k_win, v_win)


def _merge_kernel(oc_ref, os_ref, ow_ref, zc_ref, zs_ref, zw_ref, gt_ref, ge_ref,
                  w_ref, x_ref, lw_ref, lb_ref, o_ref):
    gates = _sigmoid(gt_ref[...])
    g2 = jnp.concatenate(_split2(gates), axis=1)
    acc = None
    for n, (o_r, z_r) in enumerate(((oc_ref, zc_ref), (os_ref, zs_ref), (ow_ref, zw_ref))):
        gate = _dot(g2, ge_ref[n])
        o = jnp.concatenate([o_r[0, h] for h in range(NSA_HEADS)], axis=-1).astype(F32)
        term = gate * o * _silu(z_r[...])
        acc = term if acc is None else acc + term
    y = _dot(acc.astype(BF16), w_ref[...])
    r = DEEPNORM_ALPHA * x_ref[...] + y
    o_ref[...] = _layer_norm_rows(r, lw_ref[...], lb_ref[...])


def _merge_out(o_cmp, o_sel, o_win, proj, tail, gate_expand, w_out, x_res, ln_w, ln_b, t_len, tm):
    m, d = x_res.shape
    width = NSA_HEADS * NSA_DH
    nt = t_len // tm
    ospec = pl.BlockSpec((1, NSA_HEADS, tm, NSA_DH), lambda i: (i // nt, 0, i % nt, 0))
    zspec = lambda col: pl.BlockSpec((tm, width), lambda i: (i, col))
    fixed2 = lambda i: (0, 0)
    return pl.pallas_call(
        _merge_kernel,
        grid=(m // tm,),
        in_specs=[ospec, ospec, ospec, zspec(0), zspec(1), zspec(2),
                  pl.BlockSpec((tm, LANE), lambda i: (i, 0)),
                  pl.BlockSpec((3, 2 * LANE, width), lambda i: (0, 0, 0)),
                  pl.BlockSpec((width, d), fixed2),
                  pl.BlockSpec((tm, d), lambda i: (i, 0)),
                  pl.BlockSpec((1, d), fixed2), pl.BlockSpec((1, d), fixed2)],
        out_specs=pl.BlockSpec((tm, d), lambda i: (i, 0)),
        out_shape=jax.ShapeDtypeStruct((m, d), F32),
        compiler_params=_cparams("parallel"),
        name="merge_out_proj_layernorm",
    )(o_cmp, o_sel, o_win, proj, proj, proj, tail, gate_expand, w_out, x_res,
      ln_w.reshape(1, d), ln_b.reshape(1, d))


def _pad_cols(w, n):
    return jnp.pad(w, ((0, 0), (0, n - w.shape[1])))


def kernel(x, a_w_in, a_conv_w, a_a_log, a_dt_bias, a_norm_w, a_w_out, a_ln_w, a_ln_b,
           s_w_kv, s_pe_k, s_pe_v, s_w1_k, s_w2_k, s_w1_v, s_w2_v,
           b_w_in, b_w_out, b_ln_w, b_ln_b):
    b_sz, t_len, d = x.shape
    m = b_sz * t_len
    assert t_len % SEL_KV_TILE == 0 and t_len >= WINDOW + Q_BLOCK and d % LANE == 0
    assert a_w_in.shape[0] == 1 and b_w_in.shape[0] == 1
    tm_mm = min(1024, m)
    tm_row = min(512, t_len)

    x2 = x.reshape(m, d)
    x16 = x2.astype(BF16)

    dn_width = DN_HEADS * DN_HEAD
    w_in = a_w_in[0]
    qkv = _gdn_qkv(x16, w_in[:, :3 * dn_width].astype(BF16), a_conv_w[0], t_len, tm_row)
    z_gdn = _matmul(x16, w_in[:, 3 * dn_width:4 * dn_width].astype(BF16), F32, tm_mm, 1024)
    tail = _matmul(x16, _pad_cols(w_in[:, 4 * dn_width:], LANE).astype(BF16), F32, tm_mm, LANE)
    gates_col = _gdn_gates(tail, a_a_log[0], a_dt_bias[0], tm_mm)
    nc = t_len // GDN_CHUNK
    gates_row = gates_col[:, :2 * DN_HEADS].reshape(b_sz, nc, GDN_CHUNK, 2 * DN_HEADS).transpose(0, 1, 3, 2)
    o_gdn = _gdn_chunks(qkv, z_gdn, gates_col, gates_row, a_norm_w[0], b_sz, t_len)
    h1, h1_16 = _out_proj_ln(o_gdn, a_w_out[0].astype(BF16), x2, a_ln_w[0], a_ln_b[0], tm_row)

    pos = jnp.arange(t_len, dtype=jnp.int32)
    cos_t, sin_t = _rope_tables(pos)
    kv_cmp_raw, k_sel, v_sel, k_win, v_win = _kv_proj(h1_16, s_w_kv.astype(BF16), cos_t, sin_t,
                                                      b_sz, t_len, tm_row)

    n_chunk = t_len // CMP_STRIDE
    halves = CMP_BLOCK // CMP_STRIDE
    pe = jnp.stack([s_pe_k, s_pe_v]).astype(F32).reshape(2, halves, CMP_STRIDE, 1, NSA_DH)
    pe = jnp.broadcast_to(pe, (2, halves, CMP_STRIDE, 2, NSA_DH)).reshape(2, halves, CMP_STRIDE * LANE)
    w1 = jnp.stack([s_w1_k, s_w1_v]).reshape(2, halves, CMP_STRIDE, NSA_DH, CMP_HIDDEN)
    w1 = jnp.einsum('ab,shtdk->shtadbk', jnp.eye(2, dtype=w1.dtype), w1)
    w1 = w1.reshape(2, halves, CMP_STRIDE * LANE, 2 * CMP_HIDDEN).astype(BF16)
    w2 = jnp.stack([_pad_cols(s_w2_k, LANE), _pad_cols(s_w2_v, LANE)]).astype(BF16)
    cmp_end = jnp.arange(n_chunk, dtype=jnp.int32) * CMP_STRIDE + CMP_BLOCK - 1
    cos_c, sin_c = _rope_tables(cmp_end)
    kv_cmp = _compress(kv_cmp_raw, pe, w1, w2, cos_c, sin_c, b_sz, t_len)
    k_cmp, v_cmp = kv_cmp[0], kv_cmp[1]

    nsa_width = NSA_HEADS * NSA_DH
    w_in_b = b_w_in[0]
    q = _q_proj(h1_16, w_in_b[:, :nsa_width].astype(BF16), cos_t, sin_t, b_sz, t_len, tm_row)
    z_b = _matmul(h1_16, w_in_b[:, nsa_width:4 * nsa_width].astype(BF16), F32, tm_mm, 1024)
    tail_b = _matmul(h1_16, _pad_cols(w_in_b[:, 4 * nsa_width:], LANE).astype(BF16), F32, tm_mm, LANE)

    n_sel = t_len // SEL_BLOCK
    c_start = jnp.arange(n_chunk, dtype=jnp.int32)[:, None] * CMP_STRIDE
    s_start = jnp.arange(n_sel, dtype=jnp.int32)[None, :] * SEL_BLOCK
    ov = jnp.clip(jnp.minimum(c_start + CMP_BLOCK, s_start + SEL_BLOCK) - jnp.maximum(c_start, s_start), 0, None)
    overlap = (ov.astype(F32) / CMP_BLOCK).astype(BF16)
    overlap_t = jnp.pad(overlap.T, ((0, LANE - n_sel), (0, 0)))

    o_cmp, sel_mask = _cmp_attention(q, k_cmp, v_cmp, overlap_t, t_len)
    o_sel = _sel_attention(q, k_sel, v_sel, sel_mask, t_len)
    o_win = _win_attention(q, k_win, v_win, t_len)

    lane_id = (jnp.arange(2 * LANE, dtype=jnp.int32) % LANE)[None, :, None]
    col_head = (jnp.arange(NSA_HEADS * NSA_DH, dtype=jnp.int32) // NSA_DH)[None, None, :]
    branch = jnp.arange(3, dtype=jnp.int32)[:, None, None]
    gate_expand = (lane_id == branch * NSA_HEADS + col_head).astype(BF16)
    out = _merge_out(o_cmp, o_sel, o_win, z_b, tail_b, gate_expand, b_w_out[0].astype(BF16),
                     h1, b_ln_w[0], b_ln_b[0], t_len, min(256, t_len))
    return out.reshape(b_sz, t_len, d)
```

```python
import functools

import jax
import jax.numpy as jnp
from jax import lax
from jax.experimental import pallas as pl
from jax.experimental.pallas import tpu as pltpu

F32 = jnp.float32
BF16 = jnp.bfloat16

DN_HEADS = 8
DN_HEAD = 128
DN_CONV = 4
NSA_HEADS = 16
NSA_GROUPS = 4
NSA_REP = NSA_HEADS // NSA_GROUPS
NSA_DH = 64
CMP_BLOCK = 32
CMP_STRIDE = 16
CMP_HIDDEN = 128
SEL_BLOCK = 64
SEL_TOPK = 16
WINDOW = 512
Q_BLOCK = 128
FORCED_SCORE = 1.0e4
ROPE_THETA = 10000.0
NORM_EPS = 1e-6
DEPTH = 2
DEEPNORM_ALPHA = (2.0 * DEPTH) ** 0.25

GDN_CHUNK = 128
INV_BASE = 16
GDN_SEQS = 2
LANE = 128
SEL_KV_TILE = 512
SEL_Q_TILE = 512
CMP_Q_BLOCKS = 4
WIN_Q_BLOCKS = 8
NEG_BIG = -1e30
LOG2_E = 1.4426950408889634
VMEM_LIMIT = 56 * 1024 * 1024


def _cparams(*sem):
    return pltpu.CompilerParams(dimension_semantics=sem, vmem_limit_bytes=VMEM_LIMIT)


def _dot(a, b):
    return jnp.dot(a, b, preferred_element_type=F32)


def _dot_nt(a, b):
    return lax.dot_general(a, b, (((1,), (1,)), ((), ())), preferred_element_type=F32)


def _dot_tn(a, b):
    return lax.dot_general(a, b, (((0,), (0,)), ((), ())), preferred_element_type=F32)


def _split2(a):
    hi = a.astype(BF16)
    lo = (a - hi.astype(F32)).astype(BF16)
    return hi, lo


def _split3(a):
    hi = a.astype(BF16)
    r = a - hi.astype(F32)
    mid = r.astype(BF16)
    lo = (r - mid.astype(F32)).astype(BF16)
    return hi, mid, lo


def _mm(a, b):
    return _dot(a.astype(BF16), b.astype(BF16))


def _blk(idx, size):
    return idx >> (size.bit_length() - 1)


def _silu(x):
    return x / (1.0 + jnp.exp(-x))


def _sigmoid(x):
    return 1.0 / (1.0 + jnp.exp(-x))


def _mm_kernel(x_ref, w_ref, o_ref):
    o_ref[...] = _dot(x_ref[...], w_ref[...]).astype(o_ref.dtype)


def _matmul(x, w, out_dtype, tm, tn):
    m, k = x.shape
    n = w.shape[1]
    return pl.pallas_call(
        _mm_kernel,
        grid=(m // tm, n // tn),
        in_specs=[pl.BlockSpec((tm, k), lambda i, j: (i, 0)),
                  pl.BlockSpec((k, tn), lambda i, j: (0, j))],
        out_specs=pl.BlockSpec((tm, tn), lambda i, j: (i, j)),
        out_shape=jax.ShapeDtypeStruct((m, n), out_dtype),
        compiler_params=_cparams("parallel", "parallel"),
        name="proj_matmul",
    )(x, w)


def _layer_norm_rows(r, w, b):
    mu = jnp.mean(r, axis=-1, keepdims=True)
    d = r - mu
    var = jnp.mean(d * d, axis=-1, keepdims=True)
    return d * lax.rsqrt(var + NORM_EPS) * w + b


def _out_ln_kernel(a_ref, w_ref, x_ref, lw_ref, lb_ref, o_ref, ob_ref):
    y = _dot(a_ref[...], w_ref[...])
    r = DEEPNORM_ALPHA * x_ref[...] + y
    out = _layer_norm_rows(r, lw_ref[...], lb_ref[...])
    o_ref[...] = out
    ob_ref[...] = out.astype(BF16)


def _out_proj_ln(a, w, x_res, ln_w, ln_b, tm):
    m, k = a.shape
    n = w.shape[1]
    row = lambda i: (i, 0)
    fixed = lambda i: (0, 0)
    return pl.pallas_call(
        _out_ln_kernel,
        grid=(m // tm,),
        in_specs=[pl.BlockSpec((tm, k), row), pl.BlockSpec((k, n), fixed),
                  pl.BlockSpec((tm, n), row), pl.BlockSpec((1, n), fixed),
                  pl.BlockSpec((1, n), fixed)],
        out_specs=[pl.BlockSpec((tm, n), row), pl.BlockSpec((tm, n), row)],
        out_shape=[jax.ShapeDtypeStruct((m, n), F32), jax.ShapeDtypeStruct((m, n), BF16)],
        compiler_params=_cparams("parallel"),
        name="out_proj_layernorm",
    )(a, w, x_res, ln_w.reshape(1, n), ln_b.reshape(1, n))


def _gdn_qkv_kernel(x_ref, w_ref, cw_ref, o_ref, halo_ref, *, tm, t_len):
    j = pl.program_id(0)
    i = pl.program_id(1)
    @pl.when((i * tm) % t_len == 0)
    def _():
        halo_ref[...] = jnp.zeros(halo_ref.shape, F32)

    p = _dot(x_ref[...], w_ref[...])
    halo = halo_ref[...]
    halo_ref[...] = p[tm - 8:tm, :]
    cw = cw_ref[...]
    row = lax.broadcasted_iota(jnp.int32, halo.shape, 0)
    y = p * cw[DN_CONV - 1:DN_CONV, :]
    for back in range(1, DN_CONV):
        shifted = pltpu.roll(p, back, axis=0)
        head = jnp.where(row < back, pltpu.roll(halo, back, axis=0), shifted[0:8, :])
        shifted = jnp.concatenate([head, shifted[8:, :]], axis=0)
        y = y + shifted * cw[DN_CONV - 1 - back:DN_CONV - back, :]
    y = _silu(y)
    qscale = jnp.where(j == 0, DN_HEAD ** -0.5, 1.0)
    for h in range(DN_HEADS):
        sl = slice(h * DN_HEAD, (h + 1) * DN_HEAD)
        yh = y[:, sl]
        ss = jnp.sum(yh * yh, axis=-1, keepdims=True)
        scale = jnp.where(j < 2, lax.rsqrt(ss + NORM_EPS) * qscale, 1.0)
        o_ref[:, sl] = yh * scale


def _gdn_qkv(x16, w_qkv, conv_w, t_len, tm):
    m, d = x16.shape
    width = DN_HEADS * DN_HEAD
    kern = functools.partial(_gdn_qkv_kernel, tm=tm, t_len=t_len)
    return pl.pallas_call(
        kern,
        grid=(3, m // tm),
        in_specs=[pl.BlockSpec((tm, d), lambda j, i: (i, 0)),
                  pl.BlockSpec((d, width), lambda j, i: (0, j)),
                  pl.BlockSpec((DN_CONV, width), lambda j, i: (0, j))],
        out_specs=pl.BlockSpec((tm, width), lambda j, i: (i, j)),
        out_shape=jax.ShapeDtypeStruct((m, 3 * width), F32),
        scratch_shapes=[pltpu.VMEM((8, width), F32)],
        compiler_params=_cparams("parallel", "arbitrary"),
        name="gdn_qkv_proj_conv",
    )(x16, w_qkv, conv_w)


def _gdn_gate_kernel(t_ref, al_ref, dt_ref, o_ref):
    x = t_ref[...]
    lane = lax.broadcasted_iota(jnp.int32, x.shape, 1)
    beta = _sigmoid(x)
    z = x + dt_ref[...]
    softplus = jnp.maximum(z, 0.0) + jnp.log(1.0 + jnp.exp(-jnp.abs(z)))
    g = -jnp.exp(al_ref[...]) * softplus
    o_ref[...] = jnp.where(lane < DN_HEADS, beta, jnp.where(lane < 2 * DN_HEADS, g, 0.0))


def _gdn_gates(tail, a_log, dt_bias, tm):
    m = tail.shape[0]
    pad = lambda v: jnp.zeros((1, LANE), F32).at[0, DN_HEADS:2 * DN_HEADS].set(v.astype(F32))
    return pl.pallas_call(
        _gdn_gate_kernel,
        grid=(m // tm,),
        in_specs=[pl.BlockSpec((tm, LANE), lambda i: (i, 0)),
                  pl.BlockSpec((1, LANE), lambda i: (0, 0)),
                  pl.BlockSpec((1, LANE), lambda i: (0, 0))],
        out_specs=pl.BlockSpec((tm, LANE), lambda i: (i, 0)),
        out_shape=jax.ShapeDtypeStruct((m, LANE), F32),
        compiler_params=_cparams("parallel"),
        name="gdn_gates",
    )(tail, pad(a_log), pad(dt_bias))


def _inv_unit_lower(lows, ii, jj):
    c = lows[0].shape[0]
    eye = (ii == jj).astype(F32)
    same = _blk(ii, INV_BASE) == _blk(jj, INV_BASE)
    ds = [jnp.where(same, low, 0.0) for low in lows]
    xs = [eye - d for d in ds]
    ps = ds
    span = 2
    while span < INV_BASE:
        ps = [_mm(p, p) for p in ps]
        xs = [_mm(x, eye + p) for x, p in zip(xs, ps)]
        span *= 2
    s = INV_BASE
    while s < c:
        off = (_blk(ii, 2 * s) == _blk(jj, 2 * s)) & (_blk(ii, s) != _blk(jj, s))
        ts = [_mm(x, jnp.where(off, low, 0.0)) for x, low in zip(xs, lows)]
        xs = [x - _mm(t, x) for x, t in zip(xs, ts)]
        s *= 2
    return xs


def _gdn_chunk_kernel(q_ref, k_ref, v_ref, z_ref, gc_ref, gr_ref, nw_ref, o_ref, s_ref):
    c = GDN_CHUNK

    @pl.when(pl.program_id(1) == 0)
    def _():
        s_ref[...] = jnp.zeros_like(s_ref)

    ii = lax.broadcasted_iota(jnp.int32, (c, c), 0)
    jj = lax.broadcasted_iota(jnp.int32, (c, c), 1)
    incl = ii >= jj
    strict = ii > jj
    ltri = incl.astype(BF16)
    utri = (ii <= jj).astype(BF16)
    nb = q_ref.shape[0]
    gcols, gcs_cols, gcs_rows = [], [], []
    for bb in range(nb):
        gcol = gc_ref[bb]
        gc3 = _split3(gcol)
        gr3 = _split3(gr_ref[bb, 0])
        gcols.append(gcol)
        gcs_cols.append(_dot(ltri, gc3[0]) + (_dot(ltri, gc3[1]) + _dot(ltri, gc3[2])))
        gcs_rows.append(_dot(gr3[0], utri) + (_dot(gr3[1], utri) + _dot(gr3[2], utri)))
    nw = nw_ref[...]

    chains = [(bb, h) for bb in range(nb) for h in range(DN_HEADS)]
    tile = lambda ref, bb, h: ref[bb, :, h * DN_HEAD:(h + 1) * DN_HEAD]
    betas = [gcols[bb][:, h:h + 1] for bb, h in chains]
    gccs = [gcs_cols[bb][:, DN_HEADS + h:DN_HEADS + h + 1] for bb, h in chains]
    gcrs = [gcs_rows[bb][DN_HEADS + h:DN_HEADS + h + 1, :] for bb, h in chains]
    decays = [jnp.where(incl, jnp.exp(jnp.where(incl, gcc - gcr, 0.0)), 0.0)
              for gcc, gcr in zip(gccs, gcrs)]
    kbs = [tile(k_ref, bb, h) * beta for (bb, h), beta in zip(chains, betas)]
    k16s = [tile(k_ref, bb, h).astype(BF16) for bb, h in chains]
    lows = [jnp.where(strict, _dot_nt(kb.astype(BF16), k16) * decay, 0.0)
            for kb, k16, decay in zip(kbs, k16s, decays)]
    attns = [jnp.where(incl, _dot_nt(tile(q_ref, bb, h).astype(BF16), k16) * decay, 0.0).astype(BF16)
             for (bb, h), k16, decay in zip(chains, k16s, decays)]
    tinvs = _inv_unit_lower(lows, ii, jj)
    egs = [jnp.exp(gcc) for gcc in gccs]
    sols = [_mm(tinv, jnp.concatenate([tile(v_ref, bb, h) * beta, kb * eg], axis=1))
            for tinv, (bb, h), beta, kb, eg in zip(tinvs, chains, betas, kbs, egs)]
    g_lasts = [gcc[c - 1:c, :] for gcc in gccs]
    kds = [(tile(k_ref, bb, h) * jnp.exp(g_last - gcc)).astype(BF16)
           for (bb, h), g_last, gcc in zip(chains, g_lasts, gccs)]
    qgs = [(tile(q_ref, bb, h) * eg).astype(BF16) for (bb, h), eg in zip(chains, egs)]
    states = [s_ref[bb * DN_HEADS + h] for bb, h in chains]
    s16s = [state.astype(BF16) for state in states]
    vn16s = [(sol[:, :DN_HEAD] - _dot(sol[:, DN_HEAD:].astype(BF16), s16)).astype(BF16)
             for sol, s16 in zip(sols, s16s)]
    outs = [_dot(qg, s16) + _dot(attn, vn16)
            for qg, s16, attn, vn16 in zip(qgs, s16s, attns, vn16s)]
    for n, (bb, h) in enumerate(chains):
        s_ref[bb * DN_HEADS + h] = states[n] * jnp.exp(g_lasts[n]) + _dot_tn(kds[n], vn16s[n])
    for n, (bb, h) in enumerate(chains):
        o = outs[n]
        on = o * lax.rsqrt(jnp.mean(o * o, axis=-1, keepdims=True) + NORM_EPS) * nw
        o_ref[bb, :, h * DN_HEAD:(h + 1) * DN_HEAD] = (on * _silu(tile(z_ref, bb, h))).astype(o_ref.dtype)


def _gdn_chunks(qkv, z, gates_col, gates_row, norm_w, b_sz, t_len):
    c = GDN_CHUNK
    nc = t_len // c
    width = DN_HEADS * DN_HEAD
    nb = GDN_SEQS if b_sz % GDN_SEQS == 0 else 1
    seq = lambda a: a.reshape(b_sz, t_len, a.shape[-1])
    tok = lambda col: (lambda b, i: (b, i, col))
    out = pl.pallas_call(
        _gdn_chunk_kernel,
        grid=(b_sz // nb, nc),
        in_specs=[pl.BlockSpec((nb, c, width), tok(0)), pl.BlockSpec((nb, c, width), tok(1)),
                  pl.BlockSpec((nb, c, width), tok(2)), pl.BlockSpec((nb, c, width), tok(0)),
                  pl.BlockSpec((nb, c, LANE), tok(0)),
                  pl.BlockSpec((nb, 1, 2 * DN_HEADS, c), lambda b, i: (b, i, 0, 0)),
                  pl.BlockSpec((1, DN_HEAD), lambda b, i: (0, 0))],
        out_specs=pl.BlockSpec((nb, c, width), tok(0)),
        out_shape=jax.ShapeDtypeStruct((b_sz, t_len, width), BF16),
        scratch_shapes=[pltpu.VMEM((nb * DN_HEADS, DN_HEAD, DN_HEAD), F32)],
        compiler_params=_cparams("parallel", "arbitrary"),
        name="gdn_delta_rule",
    )(seq(qkv), seq(qkv), seq(qkv), seq(z), seq(gates_col), gates_row,
      norm_w.reshape(1, DN_HEAD).astype(F32))
    return out.reshape(b_sz * t_len, width)


def _rope_tables(pos):
    half = NSA_DH // 2
    inv_freq = ROPE_THETA ** (-jnp.arange(half, dtype=F32) / half)
    ang = pos.astype(F32)[:, None] * inv_freq[None, :]
    cos, sin = jnp.cos(ang), jnp.sin(ang)
    cos_t = jnp.concatenate([cos, cos, cos, cos], axis=-1)
    sin_t = jnp.concatenate([-sin, sin, -sin, sin], axis=-1)
    return cos_t, sin_t


def _rope128(x, cos_t, sin_t):
    lane = lax.broadcasted_iota(jnp.int32, x.shape, 1)
    lower = (lane & (NSA_DH - 1)) < (NSA_DH // 2)
    fwd = pltpu.roll(x, LANE - NSA_DH // 2, axis=1)
    bwd = pltpu.roll(x, NSA_DH // 2, axis=1)
    return x * cos_t + jnp.where(lower, fwd, bwd) * sin_t


def _kv_proj_kernel(x_ref, w_ref, cos_ref, sin_ref, cmp_ref, ks_ref, vs_ref, kw_ref, vw_ref, *, tm, nt):
    gw = NSA_GROUPS * NSA_DH
    kv = _dot(x_ref[...], w_ref[...])
    cmp_ref[...] = kv[:, 0:2 * gw]
    cos_t = cos_ref[...]
    sin_t = sin_ref[...]
    t_pos = (pl.program_id(0) % nt) * tm + lax.broadcasted_iota(jnp.int32, (tm, LANE), 0)
    lane = lax.broadcasted_iota(jnp.int32, (tm, LANE), 1)
    onehot = (_blk(t_pos, SEL_BLOCK) == lane).astype(BF16)
    ones_col = (lane == NSA_DH).astype(BF16)
    zeros_half = jnp.zeros((tm, NSA_DH), BF16)
    for pair in range(NSA_GROUPS // 2):
        cols = lambda src: kv[:, src * gw + pair * LANE: src * gw + (pair + 1) * LANE]
        k_sel = _rope128(cols(2), cos_t, sin_t).astype(BF16)
        k_win = _rope128(cols(4), cos_t, sin_t).astype(BF16)
        v_sel = cols(3).astype(BF16)
        v_win = cols(5).astype(BF16)
        for half in range(2):
            g = 2 * pair + half
            sl = slice(half * NSA_DH, (half + 1) * NSA_DH)
            ks_ref[0, g, :, 0:LANE] = onehot
            ks_ref[0, g, :, LANE:LANE + NSA_DH] = k_sel[:, sl]
            ks_ref[0, g, :, LANE + NSA_DH:2 * LANE] = zeros_half
            kw_ref[0, g] = k_win[:, sl]
            for val, dst in ((v_sel, vs_ref), (v_win, vw_ref)):
                dst[0, g] = ones_col
                dst[0, g, :, 0:NSA_DH] = val[:, sl]


def _kv_proj(h16, w_kv, cos_t, sin_t, b_sz, t_len, tm):
    nt = t_len // tm
    m, d = h16.shape
    gw = NSA_GROUPS * NSA_DH
    shape = lambda w: jax.ShapeDtypeStruct((b_sz, NSA_GROUPS, t_len, w), BF16)
    ospec = lambda w: pl.BlockSpec((1, NSA_GROUPS, tm, w), lambda i: (i // nt, 0, i % nt, 0))
    return pl.pallas_call(
        functools.partial(_kv_proj_kernel, tm=tm, nt=nt),
        grid=(b_sz * nt,),
        in_specs=[pl.BlockSpec((tm, d), lambda i: (i, 0)),
                  pl.BlockSpec((d, 6 * gw), lambda i: (0, 0)),
                  pl.BlockSpec((tm, LANE), lambda i: (i % nt, 0)),
                  pl.BlockSpec((tm, LANE), lambda i: (i % nt, 0))],
        out_specs=[pl.BlockSpec((tm, 2 * gw), lambda i: (i, 0)),
                   ospec(2 * LANE), ospec(LANE), ospec(NSA_DH), ospec(LANE)],
        out_shape=[jax.ShapeDtypeStruct((m, 2 * gw), F32),
                   shape(2 * LANE), shape(LANE), shape(NSA_DH), shape(LANE)],
        compiler_params=_cparams("parallel"),
        name="kv_proj_rope_layout",
    )(h16, w_kv, cos_t, sin_t)


def _q_proj_kernel(x_ref, w_ref, cos_ref, sin_ref, o_ref):
    q = _dot(x_ref[...], w_ref[...])
    cos_t = cos_ref[...]
    sin_t = sin_ref[...]
    scale = NSA_DH ** -0.5 * LOG2_E
    for pair in range(NSA_HEADS // 2):
        x = _rope128(q[:, pair * LANE:(pair + 1) * LANE], cos_t, sin_t) * scale
        o_ref[0, 2 * pair] = x[:, :NSA_DH].astype(BF16)
        o_ref[0, 2 * pair + 1] = x[:, NSA_DH:].astype(BF16)


def _q_proj(h16, w_q, cos_t, sin_t, b_sz, t_len, tm):
    nt = t_len // tm
    m, d = h16.shape
    width = NSA_HEADS * NSA_DH
    return pl.pallas_call(
        _q_proj_kernel,
        grid=(b_sz * nt,),
        in_specs=[pl.BlockSpec((tm, d), lambda i: (i, 0)),
                  pl.BlockSpec((d, width), lambda i: (0, 0)),
                  pl.BlockSpec((tm, LANE), lambda i: (i % nt, 0)),
                  pl.BlockSpec((tm, LANE), lambda i: (i % nt, 0))],
        out_specs=pl.BlockSpec((1, NSA_HEADS, tm, NSA_DH), lambda i: (i // nt, 0, i % nt, 0)),
        out_shape=jax.ShapeDtypeStruct((b_sz, NSA_HEADS, t_len, NSA_DH), BF16),
        compiler_params=_cparams("parallel"),
        name="q_proj_rope_layout",
    )(h16, w_q, cos_t, sin_t)


def _compress_kernel(x_ref, pe_ref, w1_ref, w2_ref, cos_ref, sin_ref, o_ref, *, n_chunk):
    is_key = pl.program_id(0) == 0
    chunk = jnp.concatenate([x_ref[pl.ds(tau, n_chunk, stride=CMP_STRIDE), :]
                             for tau in range(CMP_STRIDE)], axis=1)
    a = _dot((chunk + pe_ref[0, 0:1, :]).astype(BF16), w1_ref[0, 0])
    b = _dot((chunk + pe_ref[0, 1:2, :]).astype(BF16), w1_ref[0, 1])
    hid = a + pltpu.roll(b, n_chunk - 1, axis=0)
    act = _silu(hid).astype(BF16)
    for half in range(2):
        y = _dot(act[:, half * CMP_HIDDEN:(half + 1) * CMP_HIDDEN], w2_ref[0])
        y = jnp.where(is_key, _rope128(y, cos_ref[...], sin_ref[...]), y)
        o_ref[0, 0, half] = y[:, :NSA_DH].astype(BF16)


def _compress(kv_raw, pe, w1, w2, cos_c, sin_c, b_sz, t_len):
    n_chunk = t_len // CMP_STRIDE
    pairs = NSA_GROUPS // 2
    flat = CMP_STRIDE * LANE
    return pl.pallas_call(
        functools.partial(_compress_kernel, n_chunk=n_chunk),
        grid=(2, b_sz, pairs),
        in_specs=[pl.BlockSpec((t_len, LANE), lambda s, b, p: (b, s * pairs + p)),
                  pl.BlockSpec((1, 2, flat), lambda s, b, p: (s, 0, 0)),
                  pl.BlockSpec((1, 2, flat, 2 * CMP_HIDDEN), lambda s, b, p: (s, 0, 0, 0)),
                  pl.BlockSpec((1, CMP_HIDDEN, LANE), lambda s, b, p: (s, 0, 0)),
                  pl.BlockSpec((n_chunk, LANE), lambda s, b, p: (0, 0)),
                  pl.BlockSpec((n_chunk, LANE), lambda s, b, p: (0, 0))],
        out_specs=pl.BlockSpec((1, 1, 2, n_chunk, NSA_DH), lambda s, b, p: (s, b, p, 0, 0)),
        out_shape=jax.ShapeDtypeStruct((2, b_sz, NSA_GROUPS, n_chunk, NSA_DH), BF16),
        compiler_params=_cparams("parallel", "parallel", "parallel"),
        name="compress_blocks",
    )(kv_raw, pe, w1, w2, cos_c, sin_c)


def _cmp_kernel(q_ref, k_ref, v_ref, ovt_ref, o_ref, sel_ref, *, n_sel):
    step = pl.program_id(2)
    n_cmp = k_ref.shape[2]
    steps_per_group = LANE * CMP_STRIDE // (Q_BLOCK * CMP_Q_BLOCKS)
    n_var = max(1, n_cmp // LANE)
    for var in range(n_var):
        n_col = min(n_cmp, LANE * (var + 1))
        n_blk = min(n_sel, n_col * CMP_STRIDE // SEL_BLOCK)
        pl.when(step // steps_per_group == var)(functools.partial(
            _cmp_body, step, q_ref, k_ref, v_ref, ovt_ref, o_ref, sel_ref, n_col, n_blk))


def _cmp_body(step, q_ref, k_ref, v_ref, ovt_ref, o_ref, sel_ref, n_cmp, n_sel):
    tq = Q_BLOCK
    rows = NSA_REP * tq
    n_pad = -(-n_sel // 8) * 8
    blocks = range(CMP_Q_BLOCKS)
    t0s = [(step * CMP_Q_BLOCKS + x) * tq for x in blocks]
    kc = k_ref[0, 0, 0:n_cmp, :]
    ss = [_dot_nt(q_ref[0, :, x * tq:(x + 1) * tq, :].reshape(rows, NSA_DH), kc) for x in blocks]
    lag = (lax.broadcasted_iota(jnp.int32, (tq, n_cmp), 1) * CMP_STRIDE + (CMP_BLOCK - 1)
           - lax.broadcasted_iota(jnp.int32, (tq, n_cmp), 0))
    ss = [s + jnp.concatenate([jnp.where(lag <= t0, 0.0, NEG_BIG)] * NSA_REP, axis=0)
          for s, t0 in zip(ss, t0s)]
    ms = [jnp.max(s, axis=-1, keepdims=True) for s in ss]
    es = [jnp.exp2(s - m) for s, m in zip(ss, ms)]
    ls = [jnp.sum(e, axis=-1, keepdims=True) for e in es]
    i_col = lax.broadcasted_iota(jnp.int32, (rows, 1), 0) & (tq - 1)
    invs = [jnp.where(t0 + i_col >= CMP_BLOCK - 1, 1.0 / l, 0.0) for t0, l in zip(t0s, ls)]
    ps = [e * inv for e, inv in zip(es, invs)]
    vc = v_ref[0, 0, 0:n_cmp, :]
    for x in blocks:
        o = _dot(ps[x].astype(BF16), vc)
        o_ref[0, :, x * tq:(x + 1) * tq, :] = o.reshape(NSA_REP, tq, NSA_DH).astype(o_ref.dtype)

    ovt = ovt_ref[0:n_pad, 0:n_cmp]
    imps = []
    for p in ps:
        psum = p[0:tq]
        for r in range(1, NSA_REP):
            psum = psum + p[r * tq:(r + 1) * tq]
        ph, plo = _split2(psum)
        imps.append(_dot_nt(ovt, ph) + _dot_nt(ovt, plo))

    i_q = lax.broadcasted_iota(jnp.int32, (n_pad, tq), 1)
    blk = lax.broadcasted_iota(jnp.int32, (n_pad, tq), 0)
    blk_f = blk.astype(F32)
    scores = []
    for t0, imp in zip(t0s, imps):
        t_q = t0 + i_q
        cur = _blk(t_q, SEL_BLOCK)
        forced = (blk == 0) | (blk == cur) | (blk == cur - 1)
        score = jnp.where(blk * SEL_BLOCK <= t_q, jnp.where(forced, FORCED_SCORE, imp), -1.0)
        if n_pad > n_sel:
            score = jnp.where(blk < n_sel, score, -2.0)
        scores.append(score)
    taken = -float(2 ** 127)
    for _ in range(min(SEL_TOPK, n_sel)):
        mxs = [jnp.max(score, axis=0, keepdims=True) for score in scores]
        firsts = [jnp.min(jnp.where(score == mx, blk_f, float(LANE)), axis=0, keepdims=True)
                  for score, mx in zip(scores, mxs)]
        scores = [jnp.where(blk_f == first, taken, score)
                  for score, first in zip(scores, firsts)]
    for x in blocks:
        chosen = jnp.where(scores[x] == taken, 0.0, NEG_BIG)
        if n_pad < LANE:
            chosen = jnp.concatenate([chosen, jnp.full((LANE - n_pad, tq), NEG_BIG, F32)], axis=0)
        sel_ref[0, 0, x * tq:(x + 1) * tq, :] = chosen.T.astype(sel_ref.dtype)


def _cmp_attention(q, k_cmp, v_cmp, overlap_t, t_len):
    b_sz = q.shape[0]
    n_cmp = k_cmp.shape[2]
    tq = Q_BLOCK * CMP_Q_BLOCKS
    nq = t_len // tq
    return pl.pallas_call(
        functools.partial(_cmp_kernel, n_sel=t_len // SEL_BLOCK),
        grid=(b_sz, NSA_GROUPS, nq),
        in_specs=[pl.BlockSpec((1, NSA_REP, tq, NSA_DH), lambda b, g, i: (b, g, i, 0)),
                  pl.BlockSpec((1, 1, n_cmp, NSA_DH), lambda b, g, i: (b, g, 0, 0)),
                  pl.BlockSpec((1, 1, n_cmp, NSA_DH), lambda b, g, i: (b, g, 0, 0)),
                  pl.BlockSpec((LANE, n_cmp), lambda b, g, i: (0, 0))],
        out_specs=[pl.BlockSpec((1, NSA_REP, tq, NSA_DH), lambda b, g, i: (b, g, i, 0)),
                   pl.BlockSpec((1, 1, tq, LANE), lambda b, g, i: (b, g, i, 0))],
        out_shape=[jax.ShapeDtypeStruct((b_sz, NSA_HEADS, t_len, NSA_DH), BF16),
                   jax.ShapeDtypeStruct((b_sz, NSA_GROUPS, t_len, LANE), BF16)],
        compiler_params=_cparams("parallel", "parallel", "parallel"),
        name="cmp_attention_topk",
    )(q, k_cmp, v_cmp, overlap_t)


def _online_softmax_step(s, vt, m_ref, acc_ref):
    reps = s.shape[1] // LANE
    m_prev = m_ref[...]
    m_next = jnp.maximum(m_prev, jnp.max(s, axis=-1, keepdims=True))
    p = jnp.exp2(s - jnp.concatenate([m_next] * reps, axis=1))
    alpha = jnp.exp2(m_prev - m_next)
    acc_ref[...] = alpha * acc_ref[...] + _dot(p.astype(BF16), vt)
    m_ref[...] = m_next


def _sel_kernel(q_ref, k_ref, v_ref, sel_ref, o_ref, qa_ref, m_ref, acc_ref, sa_ref, sb_ref):
    qb = pl.program_id(2)
    tq = SEL_Q_TILE
    tk = SEL_KV_TILE
    rows = NSA_REP * tq
    mask = sel_ref[0, 0]
    for r in range(NSA_REP):
        qa_ref[r * tq:(r + 1) * tq, 0:LANE] = mask
        qa_ref[r * tq:(r + 1) * tq, LANE:LANE + NSA_DH] = q_ref[0, r]
        qa_ref[r * tq:(r + 1) * tq, LANE + NSA_DH:2 * LANE] = jnp.zeros((tq, NSA_DH), BF16)
    m_ref[...] = jnp.full(m_ref.shape, -jnp.inf, F32)
    acc_ref[...] = jnp.zeros(acc_ref.shape, F32)
    last = (qb * tq) // tk

    def scores(j):
        k0 = pl.multiple_of(j * tk, tk)
        return _dot_nt(qa_ref[...], k_ref[0, 0, pl.ds(k0, tk), :])

    def values(j):
        return v_ref[0, 0, pl.ds(pl.multiple_of(j * tk, tk), tk), :]

    def causal_step(s):
        row = lax.broadcasted_iota(jnp.int32, (rows, tk), 0)
        kpos = last * tk + lax.broadcasted_iota(jnp.int32, (rows, tk), 1)
        s = jnp.where(kpos <= qb * tq + (row & (tq - 1)), s, NEG_BIG)
        _online_softmax_step(s, values(last), m_ref, acc_ref)

    sa_ref[...] = scores(0)

    def pair(i, carry):
        sb_ref[...] = scores(2 * i + 1)
        _online_softmax_step(sa_ref[...], values(2 * i), m_ref, acc_ref)
        sa_ref[...] = scores(2 * i + 2)
        _online_softmax_step(sb_ref[...], values(2 * i + 1), m_ref, acc_ref)
        return carry

    lax.fori_loop(0, last // 2, pair, 0)

    @pl.when(last % 2 == 0)
    def _():
        causal_step(sa_ref[...])

    @pl.when(last % 2 == 1)
    def _():
        sb_ref[...] = scores(last)
        _online_softmax_step(sa_ref[...], values(last - 1), m_ref, acc_ref)
        causal_step(sb_ref[...])

    acc = acc_ref[...]
    o = acc[:, 0:NSA_DH] / acc[:, NSA_DH:NSA_DH + 1]
    o_ref[0] = o.reshape(NSA_REP, tq, NSA_DH).astype(o_ref.dtype)


def _sel_attention(q, k_sel, v_sel, sel_mask, t_len):
    b_sz = q.shape[0]
    tq = SEL_Q_TILE
    nq = t_len // tq
    rows = NSA_REP * tq
    return pl.pallas_call(
        _sel_kernel,
        grid=(b_sz, NSA_GROUPS, nq),
        in_specs=[pl.BlockSpec((1, NSA_REP, tq, NSA_DH), lambda b, g, i: (b, g, i, 0)),
                  pl.BlockSpec((1, 1, t_len, 2 * LANE), lambda b, g, i: (b, g, 0, 0)),
                  pl.BlockSpec((1, 1, t_len, LANE), lambda b, g, i: (b, g, 0, 0)),
                  pl.BlockSpec((1, 1, tq, LANE), lambda b, g, i: (b, g, i, 0))],
        out_specs=pl.BlockSpec((1, NSA_REP, tq, NSA_DH), lambda b, g, i: (b, g, i, 0)),
        out_shape=jax.ShapeDtypeStruct((b_sz, NSA_HEADS, t_len, NSA_DH), BF16),
        scratch_shapes=[pltpu.VMEM((rows, 2 * LANE), BF16), pltpu.VMEM((rows, LANE), F32),
                        pltpu.VMEM((rows, LANE), F32), pltpu.VMEM((rows, SEL_KV_TILE), F32),
                        pltpu.VMEM((rows, SEL_KV_TILE), F32)],
        compiler_params=_cparams("parallel", "parallel", "arbitrary"),
        name="sel_attention",
    )(q, k_sel, v_sel, sel_mask)


def _win_body(step, q_ref, k_ref, v_ref, o_ref, interior):
    tq = Q_BLOCK
    span = WINDOW + tq
    rows = NSA_REP * tq
    blocks = range(WIN_Q_BLOCKS)
    qbs = [step * WIN_Q_BLOCKS + x for x in blocks]
    starts = [pl.multiple_of(jnp.maximum(qb * tq - WINDOW, 0), tq) for qb in qbs]
    ss = [_dot_nt(q_ref[0, :, x * tq:(x + 1) * tq, :].reshape(rows, NSA_DH),
                  k_ref[0, 0, pl.ds(starts[x], span), :]) for x in blocks]
    if interior:
        i_row = lax.broadcasted_iota(jnp.int32, (rows, tq), 0) & (tq - 1)
        col = lax.broadcasted_iota(jnp.int32, (rows, tq), 1)
        ss = [jnp.concatenate([jnp.where(col > i_row, s[:, 0:tq], NEG_BIG), s[:, tq:WINDOW],
                               jnp.where(col <= i_row, s[:, WINDOW:span], NEG_BIG)], axis=1) for s in ss]
    else:
        i_row = lax.broadcasted_iota(jnp.int32, (rows, span), 0) & (tq - 1)
        col = lax.broadcasted_iota(jnp.int32, (rows, span), 1)
        masked = []
        for x in blocks:
            t_q = qbs[x] * tq + i_row
            kpos = starts[x] + col
            masked.append(jnp.where((kpos <= t_q) & (kpos > t_q - WINDOW), ss[x], NEG_BIG))
        ss = masked
    ms = [jnp.max(s, axis=-1, keepdims=True) for s in ss]
    ps = [jnp.exp2(s - m).astype(BF16) for s, m in zip(ss, ms)]
    accs = [_dot(ps[x], v_ref[0, 0, pl.ds(starts[x], span), :]) for x in blocks]
    for x in blocks:
        o = accs[x][:, 0:NSA_DH] / accs[x][:, NSA_DH:NSA_DH + 1]
        o_ref[0, :, x * tq:(x + 1) * tq, :] = o.reshape(NSA_REP, tq, NSA_DH).astype(o_ref.dtype)


def _win_kernel(q_ref, k_ref, v_ref, o_ref):
    step = pl.program_id(2)
    edge_steps = -(-(WINDOW // Q_BLOCK) // WIN_Q_BLOCKS)
    pl.when(step < edge_steps)(functools.partial(_win_body, step, q_ref, k_ref, v_ref, o_ref, False))
    pl.when(step >= edge_steps)(functools.partial(_win_body, step, q_ref, k_ref, v_ref, o_ref, True))


def _win_attention(q, k_win, v_win, t_len):
    b_sz = q.shape[0]
    tq = Q_BLOCK * WIN_Q_BLOCKS
    nq = t_len // tq
    return pl.pallas_call(
        _win_kernel,
        grid=(b_sz, NSA_GROUPS, nq),
        in_specs=[pl.BlockSpec((1, NSA_REP, tq, NSA_DH), lambda b, g, i: (b, g, i, 0)),
                  pl.BlockSpec((1, 1, t_len, NSA_DH), lambda b, g, i: (b, g, 0, 0)),
                  pl.BlockSpec((1, 1, t_len, LANE), lambda b, g, i: (b, g, 0, 0))],
        out_specs=pl.BlockSpec((1, NSA_REP, tq, NSA_DH), lambda b, g, i: (b, g, i, 0)),
        out_shape=jax.ShapeDtypeStruct((b_sz, NSA_HEADS, t_len, NSA_DH), BF16),
        compiler_params=_cparams("parallel", "parallel", "parallel"),
        name="win_attention",
    )(q, k_win, v_win)


def _merge_kernel(oc_ref, os_ref, ow_ref, zc_ref, zs_ref, zw_ref, gt_ref, ge_ref,
                  w_ref, x_ref, lw_ref, lb_ref, o_ref):
    gates = _sigmoid(gt_ref[...])
    g2 = jnp.concatenate(_split2(gates), axis=1)
    acc = None
    for n, (o_r, z_r) in enumerate(((oc_ref, zc_ref), (os_ref, zs_ref), (ow_ref, zw_ref))):
        gate = _dot(g2, ge_ref[n])
        o = jnp.concatenate([o_r[0, h] for h in range(NSA_HEADS)], axis=-1).astype(F32)
        term = gate * o * _silu(z_r[...])
        acc = term if acc is None else acc + term
    y = _dot(acc.astype(BF16), w_ref[...])
    r = DEEPNORM_ALPHA * x_ref[...] + y
    o_ref[...] = _layer_norm_rows(r, lw_ref[...], lb_ref[...])


def _merge_out(o_cmp, o_sel, o_win, proj, tail, gate_expand, w_out, x_res, ln_w, ln_b, t_len, tm):
    m, d = x_res.shape
    width = NSA_HEADS * NSA_DH
    nt = t_len // tm
    ospec = pl.BlockSpec((1, NSA_HEADS, tm, NSA_DH), lambda i: (i // nt, 0, i % nt, 0))
    zspec = lambda col: pl.BlockSpec((tm, width), lambda i: (i, col))
    fixed2 = lambda i: (0, 0)
    return pl.pallas_call(
        _merge_kernel,
        grid=(m // tm,),
        in_specs=[ospec, ospec, ospec, zspec(0), zspec(1), zspec(2),
                  pl.BlockSpec((tm, LANE), lambda i: (i, 0)),
                  pl.BlockSpec((3, 2 * LANE, width), lambda i: (0, 0, 0)),
                  pl.BlockSpec((width, d), fixed2),
                  pl.BlockSpec((tm, d), lambda i: (i, 0)),
                  pl.BlockSpec((1, d), fixed2), pl.BlockSpec((1, d), fixed2)],
        out_specs=pl.BlockSpec((tm, d), lambda i: (i, 0)),
        out_shape=jax.ShapeDtypeStruct((m, d), F32),
        compiler_params=_cparams("parallel"),
        name="merge_out_proj_layernorm",
    )(o_cmp, o_sel, o_win, proj, proj, proj, tail, gate_expand, w_out, x_res,
      ln_w.reshape(1, d), ln_b.reshape(1, d))


def _pad_cols(w, n):
    return jnp.pad(w, ((0, 0), (0, n - w.shape[1])))


def kernel(x, a_w_in, a_conv_w, a_a_log, a_dt_bias, a_norm_w, a_w_out, a_ln_w, a_ln_b,
           s_w_kv, s_pe_k, s_pe_v, s_w1_k, s_w2_k, s_w1_v, s_w2_v,
           b_w_in, b_w_out, b_ln_w, b_ln_b):
    b_sz, t_len, d = x.shape
    m = b_sz * t_len
    assert t_len % SEL_KV_TILE == 0 and t_len >= WINDOW + Q_BLOCK and d % LANE == 0
    assert t_len % (Q_BLOCK * WIN_Q_BLOCKS) == 0 and t_len % (Q_BLOCK * CMP_Q_BLOCKS) == 0
    assert t_len // SEL_BLOCK <= LANE
    assert a_w_in.shape[0] == 1 and b_w_in.shape[0] == 1
    tm_mm = min(1024, m)
    tm_row = min(512, t_len)

    x2 = x.reshape(m, d)
    x16 = x2.astype(BF16)

    dn_width = DN_HEADS * DN_HEAD
    w_in = a_w_in[0]
    qkv = _gdn_qkv(x16, w_in[:, :3 * dn_width].astype(BF16), a_conv_w[0], t_len, tm_row)
    z_gdn = _matmul(x16, w_in[:, 3 * dn_width:4 * dn_width].astype(BF16), F32, tm_mm, 1024)
    tail = _matmul(x16, _pad_cols(w_in[:, 4 * dn_width:], LANE).astype(BF16), F32, tm_mm, LANE)
    gates_col = _gdn_gates(tail, a_a_log[0], a_dt_bias[0], tm_mm)
    nc = t_len // GDN_CHUNK
    gates_row = gates_col[:, :2 * DN_HEADS].reshape(b_sz, nc, GDN_CHUNK, 2 * DN_HEADS).transpose(0, 1, 3, 2)
    o_gdn = _gdn_chunks(qkv, z_gdn, gates_col, gates_row, a_norm_w[0], b_sz, t_len)
    h1, h1_16 = _out_proj_ln(o_gdn, a_w_out[0].astype(BF16), x2, a_ln_w[0], a_ln_b[0], tm_row)

    pos = jnp.arange(t_len, dtype=jnp.int32)
    cos_t, sin_t = _rope_tables(pos)
    kv_cmp_raw, k_sel, v_sel, k_win, v_win = _kv_proj(h1_16, s_w_kv.astype(BF16), cos_t, sin_t,
                                                      b_sz, t_len, tm_row)

    n_chunk = t_len // CMP_STRIDE
    halves = CMP_BLOCK // CMP_STRIDE
    pe = jnp.stack([s_pe_k, s_pe_v]).astype(F32).reshape(2, halves, CMP_STRIDE, 1, NSA_DH)
    pe = jnp.broadcast_to(pe, (2, halves, CMP_STRIDE, 2, NSA_DH)).reshape(2, halves, CMP_STRIDE * LANE)
    w1 = jnp.stack([s_w1_k, s_w1_v]).reshape(2, halves, CMP_STRIDE, NSA_DH, CMP_HIDDEN)
    w1 = jnp.einsum('ab,shtdk->shtadbk', jnp.eye(2, dtype=w1.dtype), w1)
    w1 = w1.reshape(2, halves, CMP_STRIDE * LANE, 2 * CMP_HIDDEN).astype(BF16)
    w2 = jnp.stack([_pad_cols(s_w2_k, LANE), _pad_cols(s_w2_v, LANE)]).astype(BF16)
    cmp_end = jnp.arange(n_chunk, dtype=jnp.int32) * CMP_STRIDE + CMP_BLOCK - 1
    cos_c, sin_c = _rope_tables(cmp_end)
    kv_cmp = _compress(kv_cmp_raw, pe, w1, w2, cos_c, sin_c, b_sz, t_len)
    k_cmp, v_cmp = kv_cmp[0], kv_cmp[1]

    nsa_width = NSA_HEADS * NSA_DH
    w_in_b = b_w_in[0]
    q = _q_proj(h1_16, w_in_b[:, :nsa_width].astype(BF16), cos_t, sin_t, b_sz, t_len, tm_row)
    z_b = _matmul(h1_16, w_in_b[:, nsa_width:4 * nsa_width].astype(BF16), F32, tm_mm, 1024)
    tail_b = _matmul(h1_16, _pad_cols(w_in_b[:, 4 * nsa_width:], LANE).astype(BF16), F32, tm_mm, LANE)

    n_sel = t_len // SEL_BLOCK
    c_start = jnp.arange(n_chunk, dtype=jnp.int32)[:, None] * CMP_STRIDE
    s_start = jnp.arange(n_sel, dtype=jnp.int32)[None, :] * SEL_BLOCK
    ov = jnp.clip(jnp.minimum(c_start + CMP_BLOCK, s_start + SEL_BLOCK) - jnp.maximum(c_start, s_start), 0, None)
    overlap = (ov.astype(F32) / CMP_BLOCK).astype(BF16)
    overlap_t = jnp.pad(overlap.T, ((0, LANE - n_sel), (0, 0)))

    o_cmp, sel_mask = _cmp_attention(q, k_cmp, v_cmp, overlap_t, t_len)
    o_sel = _sel_attention(q, k_sel, v_sel, sel_mask, t_len)
    o_win = _win_attention(q, k_win, v_win, t_len)

    lane_id = (jnp.arange(2 * LANE, dtype=jnp.int32) % LANE)[None, :, None]
    col_head = (jnp.arange(NSA_HEADS * NSA_DH, dtype=jnp.int32) // NSA_DH)[None, None, :]
    branch = jnp.arange(3, dtype=jnp.int32)[:, None, None]
    gate_expand = (lane_id == branch * NSA_HEADS + col_head).astype(BF16)
    out = _merge_out(o_cmp, o_sel, o_win, z_b, tail_b, gate_expand, b_w_out[0].astype(BF16),
                     h1, b_ln_w[0], b_ln_b[0], t_len, tm_row)
    return out.reshape(b_sz, t_len, d)
```

```python
import functools

import jax
import jax.numpy as jnp
from jax import lax
from jax.experimental import pallas as pl
from jax.experimental.pallas import tpu as pltpu

F32 = jnp.float32
BF16 = jnp.bfloat16

DN_HEADS = 8
DN_HEAD = 128
DN_CONV = 4
NSA_HEADS = 16
NSA_GROUPS = 4
NSA_REP = NSA_HEADS // NSA_GROUPS
NSA_DH = 64
CMP_BLOCK = 32
CMP_STRIDE = 16
CMP_HIDDEN = 128
SEL_BLOCK = 64
SEL_TOPK = 16
WINDOW = 512
Q_BLOCK = 128
FORCED_SCORE = 1.0e4
ROPE_THETA = 10000.0
NORM_EPS = 1e-6
DEPTH = 2
DEEPNORM_ALPHA = (2.0 * DEPTH) ** 0.25

GDN_CHUNK = 128
INV_BASE = 16
GDN_SEQS = 2
LANE = 128
SUBLANE = 8
PROJ_ROWS = 1024
PROJ_COLS = 1024
ROW_TILE = 512
SEL_KV_TILE = 512
SEL_Q_TILE = 512
CMP_Q_BLOCKS = 8
WIN_Q_BLOCKS = 8
NEG_BIG = -1e30
LOG2_E = 1.4426950408889634
VMEM_LIMIT = 56 * 1024 * 1024


def _cparams(*sem):
    return pltpu.CompilerParams(dimension_semantics=sem, vmem_limit_bytes=VMEM_LIMIT)


def _dot(a, b):
    return jnp.dot(a, b, preferred_element_type=F32)


def _dot_nt(a, b):
    return lax.dot_general(a, b, (((1,), (1,)), ((), ())), preferred_element_type=F32)


def _dot_tn(a, b):
    return lax.dot_general(a, b, (((0,), (0,)), ((), ())), preferred_element_type=F32)


def _split2(a):
    hi = a.astype(BF16)
    lo = (a - hi.astype(F32)).astype(BF16)
    return hi, lo


def _split3(a):
    hi = a.astype(BF16)
    r = a - hi.astype(F32)
    mid = r.astype(BF16)
    lo = (r - mid.astype(F32)).astype(BF16)
    return hi, mid, lo


def _mm(a, b):
    return _dot(a.astype(BF16), b.astype(BF16))


def _blk(idx, size):
    return idx >> (size.bit_length() - 1)


def _silu(x):
    return x / (1.0 + jnp.exp(-x))


def _sigmoid(x):
    return 1.0 / (1.0 + jnp.exp(-x))


def _mm_kernel(x_ref, w_ref, o_ref):
    o_ref[...] = _dot(x_ref[...], w_ref[...])


def _matmul(x, w):
    m, k = x.shape
    n = w.shape[1]
    tm = min(PROJ_ROWS, m)
    tn = min(PROJ_COLS, n)
    return pl.pallas_call(
        _mm_kernel,
        grid=(m // tm, n // tn),
        in_specs=[pl.BlockSpec((tm, k), lambda i, j: (i, 0)),
                  pl.BlockSpec((k, tn), lambda i, j: (0, j))],
        out_specs=pl.BlockSpec((tm, tn), lambda i, j: (i, j)),
        out_shape=jax.ShapeDtypeStruct((m, n), F32),
        compiler_params=_cparams("parallel", "parallel"),
        name="proj_matmul",
    )(x, w)


def _layer_norm_rows(r, w, b):
    mu = jnp.mean(r, axis=-1, keepdims=True)
    d = r - mu
    var = jnp.mean(d * d, axis=-1, keepdims=True)
    return d * lax.rsqrt(var + NORM_EPS) * w + b


def _out_ln_kernel(a_ref, w_ref, x_ref, lw_ref, lb_ref, o_ref, ob_ref):
    y = _dot(a_ref[...], w_ref[...])
    r = DEEPNORM_ALPHA * x_ref[...] + y
    out = _layer_norm_rows(r, lw_ref[...], lb_ref[...])
    o_ref[...] = out
    ob_ref[...] = out.astype(BF16)


def _out_proj_ln(a, w, x_res, ln_w, ln_b, tm):
    m, k = a.shape
    n = w.shape[1]
    row = lambda i: (i, 0)
    fixed = lambda i: (0, 0)
    return pl.pallas_call(
        _out_ln_kernel,
        grid=(m // tm,),
        in_specs=[pl.BlockSpec((tm, k), row), pl.BlockSpec((k, n), fixed),
                  pl.BlockSpec((tm, n), row), pl.BlockSpec((1, n), fixed),
                  pl.BlockSpec((1, n), fixed)],
        out_specs=[pl.BlockSpec((tm, n), row), pl.BlockSpec((tm, n), row)],
        out_shape=[jax.ShapeDtypeStruct((m, n), F32), jax.ShapeDtypeStruct((m, n), BF16)],
        compiler_params=_cparams("parallel"),
        name="out_proj_layernorm",
    )(a, w, x_res, ln_w.reshape(1, n), ln_b.reshape(1, n))


def _gdn_qkv_kernel(x_ref, w_ref, cw_ref, o_ref, halo_ref, *, tm, t_len):
    j = pl.program_id(0)
    i = pl.program_id(1)
    @pl.when((i * tm) % t_len == 0)
    def _():
        halo_ref[...] = jnp.zeros(halo_ref.shape, F32)

    p = _dot(x_ref[...], w_ref[...])
    halo = halo_ref[...]
    halo_ref[...] = p[tm - SUBLANE:tm, :]
    cw = cw_ref[...]
    row = lax.broadcasted_iota(jnp.int32, halo.shape, 0)
    y = p * cw[DN_CONV - 1:DN_CONV, :]
    for back in range(1, DN_CONV):
        shifted = pltpu.roll(p, back, axis=0)
        head = jnp.where(row < back, pltpu.roll(halo, back, axis=0), shifted[0:SUBLANE, :])
        shifted = jnp.concatenate([head, shifted[SUBLANE:, :]], axis=0)
        y = y + shifted * cw[DN_CONV - 1 - back:DN_CONV - back, :]
    y = _silu(y)
    qscale = jnp.where(j == 0, DN_HEAD ** -0.5, 1.0)
    for h in range(DN_HEADS):
        sl = slice(h * DN_HEAD, (h + 1) * DN_HEAD)
        yh = y[:, sl]
        ss = jnp.sum(yh * yh, axis=-1, keepdims=True)
        scale = jnp.where(j < 2, lax.rsqrt(ss + NORM_EPS) * qscale, 1.0)
        o_ref[:, sl] = yh * scale


def _gdn_qkv(x16, w_qkv, conv_w, t_len, tm):
    m, d = x16.shape
    width = DN_HEADS * DN_HEAD
    kern = functools.partial(_gdn_qkv_kernel, tm=tm, t_len=t_len)
    return pl.pallas_call(
        kern,
        grid=(3, m // tm),
        in_specs=[pl.BlockSpec((tm, d), lambda j, i: (i, 0)),
                  pl.BlockSpec((d, width), lambda j, i: (0, j)),
                  pl.BlockSpec((DN_CONV, width), lambda j, i: (0, j))],
        out_specs=pl.BlockSpec((tm, width), lambda j, i: (i, j)),
        out_shape=jax.ShapeDtypeStruct((m, 3 * width), F32),
        scratch_shapes=[pltpu.VMEM((SUBLANE, width), F32)],
        compiler_params=_cparams("parallel", "arbitrary"),
        name="gdn_qkv_proj_conv",
    )(x16, w_qkv, conv_w)


def _gdn_gate_kernel(t_ref, al_ref, dt_ref, o_ref):
    x = t_ref[...]
    lane = lax.broadcasted_iota(jnp.int32, x.shape, 1)
    beta = _sigmoid(x)
    z = x + dt_ref[...]
    softplus = jnp.maximum(z, 0.0) + jnp.log(1.0 + jnp.exp(-jnp.abs(z)))
    g = -jnp.exp(al_ref[...]) * softplus
    o_ref[...] = jnp.where(lane < DN_HEADS, beta, jnp.where(lane < 2 * DN_HEADS, g, 0.0))


def _gdn_gates(tail, a_log, dt_bias, tm):
    m = tail.shape[0]
    pad = lambda v: jnp.zeros((1, LANE), F32).at[0, DN_HEADS:2 * DN_HEADS].set(v.astype(F32))
    return pl.pallas_call(
        _gdn_gate_kernel,
        grid=(m // tm,),
        in_specs=[pl.BlockSpec((tm, LANE), lambda i: (i, 0)),
                  pl.BlockSpec((1, LANE), lambda i: (0, 0)),
                  pl.BlockSpec((1, LANE), lambda i: (0, 0))],
        out_specs=pl.BlockSpec((tm, LANE), lambda i: (i, 0)),
        out_shape=jax.ShapeDtypeStruct((m, LANE), F32),
        compiler_params=_cparams("parallel"),
        name="gdn_gates",
    )(tail, pad(a_log), pad(dt_bias))


def _inv_unit_lower(lows, ii, jj):
    c = lows[0].shape[0]
    eye = (ii == jj).astype(F32)
    same = _blk(ii, INV_BASE) == _blk(jj, INV_BASE)
    ds = [jnp.where(same, low, 0.0) for low in lows]
    xs = [eye - d for d in ds]
    ps = ds
    span = 2
    while span < INV_BASE:
        ps = [_mm(p, p) for p in ps]
        xs = [_mm(x, eye + p) for x, p in zip(xs, ps)]
        span *= 2
    s = INV_BASE
    while s < c:
        off = (_blk(ii, 2 * s) == _blk(jj, 2 * s)) & (_blk(ii, s) != _blk(jj, s))
        ts = [_mm(x, jnp.where(off, low, 0.0)) for x, low in zip(xs, lows)]
        xs = [x - _mm(t, x) for x, t in zip(xs, ts)]
        s *= 2
    return xs


def _gdn_chunk_kernel(q_ref, k_ref, v_ref, z_ref, gc_ref, gr_ref, nw_ref, o_ref, s_ref):
    c = GDN_CHUNK

    @pl.when(pl.program_id(1) == 0)
    def _():
        s_ref[...] = jnp.zeros_like(s_ref)

    ii = lax.broadcasted_iota(jnp.int32, (c, c), 0)
    jj = lax.broadcasted_iota(jnp.int32, (c, c), 1)
    incl = ii >= jj
    strict = ii > jj
    ltri = incl.astype(BF16)
    utri = (ii <= jj).astype(BF16)
    nb = q_ref.shape[0]
    gcols, gcs_cols, gcs_rows = [], [], []
    for bb in range(nb):
        gcol = gc_ref[bb]
        gc3 = _split3(gcol)
        gr3 = _split3(gr_ref[bb, 0])
        gcols.append(gcol)
        gcs_cols.append(_dot(ltri, gc3[0]) + (_dot(ltri, gc3[1]) + _dot(ltri, gc3[2])))
        gcs_rows.append(_dot(gr3[0], utri) + (_dot(gr3[1], utri) + _dot(gr3[2], utri)))
    nw = nw_ref[...]

    chains = [(bb, h) for bb in range(nb) for h in range(DN_HEADS)]
    tile = lambda ref, bb, h: ref[bb, :, h * DN_HEAD:(h + 1) * DN_HEAD]
    betas = [gcols[bb][:, h:h + 1] for bb, h in chains]
    gccs = [gcs_cols[bb][:, DN_HEADS + h:DN_HEADS + h + 1] for bb, h in chains]
    gcrs = [gcs_rows[bb][DN_HEADS + h:DN_HEADS + h + 1, :] for bb, h in chains]
    decays = [jnp.where(incl, jnp.exp(jnp.where(incl, gcc - gcr, 0.0)), 0.0)
              for gcc, gcr in zip(gccs, gcrs)]
    kbs = [tile(k_ref, bb, h) * beta for (bb, h), beta in zip(chains, betas)]
    k16s = [tile(k_ref, bb, h).astype(BF16) for bb, h in chains]
    lows = [jnp.where(strict, _dot_nt(kb.astype(BF16), k16) * decay, 0.0)
            for kb, k16, decay in zip(kbs, k16s, decays)]
    attns = [jnp.where(incl, _dot_nt(tile(q_ref, bb, h).astype(BF16), k16) * decay, 0.0).astype(BF16)
             for (bb, h), k16, decay in zip(chains, k16s, decays)]
    tinvs = _inv_unit_lower(lows, ii, jj)
    egs = [jnp.exp(gcc) for gcc in gccs]
    sols = [_mm(tinv, jnp.concatenate([tile(v_ref, bb, h) * beta, kb * eg], axis=1))
            for tinv, (bb, h), beta, kb, eg in zip(tinvs, chains, betas, kbs, egs)]
    g_lasts = [gcc[c - 1:c, :] for gcc in gccs]
    kds = [(tile(k_ref, bb, h) * jnp.exp(g_last - gcc)).astype(BF16)
           for (bb, h), g_last, gcc in zip(chains, g_lasts, gccs)]
    qgs = [(tile(q_ref, bb, h) * eg).astype(BF16) for (bb, h), eg in zip(chains, egs)]
    states = [s_ref[bb * DN_HEADS + h] for bb, h in chains]
    s16s = [state.astype(BF16) for state in states]
    vn16s = [(sol[:, :DN_HEAD] - _dot(sol[:, DN_HEAD:].astype(BF16), s16)).astype(BF16)
             for sol, s16 in zip(sols, s16s)]
    outs = [_dot(qg, s16) + _dot(attn, vn16)
            for qg, s16, attn, vn16 in zip(qgs, s16s, attns, vn16s)]
    for n, (bb, h) in enumerate(chains):
        s_ref[bb * DN_HEADS + h] = states[n] * jnp.exp(g_lasts[n]) + _dot_tn(kds[n], vn16s[n])
    for n, (bb, h) in enumerate(chains):
        o = outs[n]
        on = o * lax.rsqrt(jnp.mean(o * o, axis=-1, keepdims=True) + NORM_EPS) * nw
        o_ref[bb, :, h * DN_HEAD:(h + 1) * DN_HEAD] = (on * _silu(tile(z_ref, bb, h))).astype(o_ref.dtype)


def _gdn_chunks(qkv, z, gates_col, gates_row, norm_w, b_sz, t_len):
    c = GDN_CHUNK
    nc = t_len // c
    width = DN_HEADS * DN_HEAD
    nb = GDN_SEQS if b_sz % GDN_SEQS == 0 else 1
    seq = lambda a: a.reshape(b_sz, t_len, a.shape[-1])
    tok = lambda col: (lambda b, i: (b, i, col))
    out = pl.pallas_call(
        _gdn_chunk_kernel,
        grid=(b_sz // nb, nc),
        in_specs=[pl.BlockSpec((nb, c, width), tok(0)), pl.BlockSpec((nb, c, width), tok(1)),
                  pl.BlockSpec((nb, c, width), tok(2)), pl.BlockSpec((nb, c, width), tok(0)),
                  pl.BlockSpec((nb, c, LANE), tok(0)),
                  pl.BlockSpec((nb, 1, 2 * DN_HEADS, c), lambda b, i: (b, i, 0, 0)),
                  pl.BlockSpec((1, DN_HEAD), lambda b, i: (0, 0))],
        out_specs=pl.BlockSpec((nb, c, width), tok(0)),
        out_shape=jax.ShapeDtypeStruct((b_sz, t_len, width), BF16),
        scratch_shapes=[pltpu.VMEM((nb * DN_HEADS, DN_HEAD, DN_HEAD), F32)],
        compiler_params=_cparams("parallel", "arbitrary"),
        name="gdn_delta_rule",
    )(seq(qkv), seq(qkv), seq(qkv), seq(z), seq(gates_col), gates_row,
      norm_w.reshape(1, DN_HEAD).astype(F32))
    return out.reshape(b_sz * t_len, width)


def _rope_tables(pos):
    half = NSA_DH // 2
    inv_freq = ROPE_THETA ** (-jnp.arange(half, dtype=F32) / half)
    ang = pos.astype(F32)[:, None] * inv_freq[None, :]
    cos, sin = jnp.cos(ang), jnp.sin(ang)
    cos_t = jnp.concatenate([cos, cos, cos, cos], axis=-1)
    sin_t = jnp.concatenate([-sin, sin, -sin, sin], axis=-1)
    return cos_t, sin_t


def _rope128(x, cos_t, sin_t):
    lane = lax.broadcasted_iota(jnp.int32, x.shape, 1)
    lower = (lane & (NSA_DH - 1)) < (NSA_DH // 2)
    fwd = pltpu.roll(x, LANE - NSA_DH // 2, axis=1)
    bwd = pltpu.roll(x, NSA_DH // 2, axis=1)
    return x * cos_t + jnp.where(lower, fwd, bwd) * sin_t


def _kv_proj_kernel(x_ref, w_ref, cos_ref, sin_ref, cmp_ref, ks_ref, vs_ref, kw_ref, vw_ref, *, tm, nt):
    gw = NSA_GROUPS * NSA_DH
    kv = _dot(x_ref[...], w_ref[...])
    cmp_ref[...] = kv[:, 0:2 * gw]
    cos_t = cos_ref[...]
    sin_t = sin_ref[...]
    t_pos = (pl.program_id(0) % nt) * tm + lax.broadcasted_iota(jnp.int32, (tm, LANE), 0)
    lane = lax.broadcasted_iota(jnp.int32, (tm, LANE), 1)
    onehot = (_blk(t_pos, SEL_BLOCK) == lane).astype(BF16)
    ones_col = (lane == NSA_DH).astype(BF16)
    zeros_half = jnp.zeros((tm, NSA_DH), BF16)
    for pair in range(NSA_GROUPS // 2):
        cols = lambda src: kv[:, src * gw + pair * LANE: src * gw + (pair + 1) * LANE]
        k_sel = _rope128(cols(2), cos_t, sin_t).astype(BF16)
        k_win = _rope128(cols(4), cos_t, sin_t).astype(BF16)
        v_sel = cols(3).astype(BF16)
        v_win = cols(5).astype(BF16)
        for half in range(2):
            g = 2 * pair + half
            sl = slice(half * NSA_DH, (half + 1) * NSA_DH)
            ks_ref[0, g, :, 0:LANE] = onehot
            ks_ref[0, g, :, LANE:LANE + NSA_DH] = k_sel[:, sl]
            ks_ref[0, g, :, LANE + NSA_DH:2 * LANE] = zeros_half
            kw_ref[0, g] = k_win[:, sl]
            for val, dst in ((v_sel, vs_ref), (v_win, vw_ref)):
                dst[0, g] = ones_col
                dst[0, g, :, 0:NSA_DH] = val[:, sl]


def _kv_proj(h16, w_kv, cos_t, sin_t, b_sz, t_len, tm):
    nt = t_len // tm
    m, d = h16.shape
    gw = NSA_GROUPS * NSA_DH
    shape = lambda w: jax.ShapeDtypeStruct((b_sz, NSA_GROUPS, t_len, w), BF16)
    ospec = lambda w: pl.BlockSpec((1, NSA_GROUPS, tm, w), lambda i: (i // nt, 0, i % nt, 0))
    return pl.pallas_call(
        functools.partial(_kv_proj_kernel, tm=tm, nt=nt),
        grid=(b_sz * nt,),
        in_specs=[pl.BlockSpec((tm, d), lambda i: (i, 0)),
                  pl.BlockSpec((d, 6 * gw), lambda i: (0, 0)),
                  pl.BlockSpec((tm, LANE), lambda i: (i % nt, 0)),
                  pl.BlockSpec((tm, LANE), lambda i: (i % nt, 0))],
        out_specs=[pl.BlockSpec((tm, 2 * gw), lambda i: (i, 0)),
                   ospec(2 * LANE), ospec(LANE), ospec(NSA_DH), ospec(LANE)],
        out_shape=[jax.ShapeDtypeStruct((m, 2 * gw), F32),
                   shape(2 * LANE), shape(LANE), shape(NSA_DH), shape(LANE)],
        compiler_params=_cparams("parallel"),
        name="kv_proj_rope_layout",
    )(h16, w_kv, cos_t, sin_t)


def _q_proj_kernel(x_ref, w_ref, cos_ref, sin_ref, o_ref):
    q = _dot(x_ref[...], w_ref[...])
    cos_t = cos_ref[...]
    sin_t = sin_ref[...]
    scale = NSA_DH ** -0.5 * LOG2_E
    for pair in range(NSA_HEADS // 2):
        x = _rope128(q[:, pair * LANE:(pair + 1) * LANE], cos_t, sin_t) * scale
        o_ref[0, 2 * pair] = x[:, :NSA_DH].astype(BF16)
        o_ref[0, 2 * pair + 1] = x[:, NSA_DH:].astype(BF16)


def _q_proj(h16, w_q, cos_t, sin_t, b_sz, t_len, tm):
    nt = t_len // tm
    m, d = h16.shape
    width = NSA_HEADS * NSA_DH
    return pl.pallas_call(
        _q_proj_kernel,
        grid=(b_sz * nt,),
        in_specs=[pl.BlockSpec((tm, d), lambda i: (i, 0)),
                  pl.BlockSpec((d, width), lambda i: (0, 0)),
                  pl.BlockSpec((tm, LANE), lambda i: (i % nt, 0)),
                  pl.BlockSpec((tm, LANE), lambda i: (i % nt, 0))],
        out_specs=pl.BlockSpec((1, NSA_HEADS, tm, NSA_DH), lambda i: (i // nt, 0, i % nt, 0)),
        out_shape=jax.ShapeDtypeStruct((b_sz, NSA_HEADS, t_len, NSA_DH), BF16),
        compiler_params=_cparams("parallel"),
        name="q_proj_rope_layout",
    )(h16, w_q, cos_t, sin_t)


def _compress_kernel(x_ref, pe_ref, w1_ref, w2_ref, cos_ref, sin_ref, o_ref, *, n_chunk):
    is_key = pl.program_id(0) == 0
    chunk = jnp.concatenate([x_ref[pl.ds(tau, n_chunk, stride=CMP_STRIDE), :]
                             for tau in range(CMP_STRIDE)], axis=1)
    a = _dot((chunk + pe_ref[0, 0:1, :]).astype(BF16), w1_ref[0, 0])
    b = _dot((chunk + pe_ref[0, 1:2, :]).astype(BF16), w1_ref[0, 1])
    hid = a + pltpu.roll(b, n_chunk - 1, axis=0)
    act = _silu(hid).astype(BF16)
    for half in range(2):
        y = _dot(act[:, half * CMP_HIDDEN:(half + 1) * CMP_HIDDEN], w2_ref[0])
        y = jnp.where(is_key, _rope128(y, cos_ref[...], sin_ref[...]), y)
        o_ref[0, 0, half] = y[:, :NSA_DH].astype(BF16)


def _compress(kv_raw, pe, w1, w2, cos_c, sin_c, b_sz, t_len):
    n_chunk = t_len // CMP_STRIDE
    pairs = NSA_GROUPS // 2
    flat = CMP_STRIDE * LANE
    return pl.pallas_call(
        functools.partial(_compress_kernel, n_chunk=n_chunk),
        grid=(2, b_sz, pairs),
        in_specs=[pl.BlockSpec((t_len, LANE), lambda s, b, p: (b, s * pairs + p)),
                  pl.BlockSpec((1, 2, flat), lambda s, b, p: (s, 0, 0)),
                  pl.BlockSpec((1, 2, flat, 2 * CMP_HIDDEN), lambda s, b, p: (s, 0, 0, 0)),
                  pl.BlockSpec((1, CMP_HIDDEN, LANE), lambda s, b, p: (s, 0, 0)),
                  pl.BlockSpec((n_chunk, LANE), lambda s, b, p: (0, 0)),
                  pl.BlockSpec((n_chunk, LANE), lambda s, b, p: (0, 0))],
        out_specs=pl.BlockSpec((1, 1, 2, n_chunk, NSA_DH), lambda s, b, p: (s, b, p, 0, 0)),
        out_shape=jax.ShapeDtypeStruct((2, b_sz, NSA_GROUPS, n_chunk, NSA_DH), BF16),
        compiler_params=_cparams("parallel", "parallel", "parallel"),
        name="compress_blocks",
    )(kv_raw, pe, w1, w2, cos_c, sin_c)


def _cmp_kernel(q_ref, k_ref, v_ref, ovt_ref, o_ref, sel_ref, *, n_sel):
    step = pl.program_id(2)
    n_cmp = k_ref.shape[2]
    steps_per_group = LANE * CMP_STRIDE // (Q_BLOCK * CMP_Q_BLOCKS)
    n_var = max(1, n_cmp // LANE)
    for var in range(n_var):
        n_col = min(n_cmp, LANE * (var + 1))
        n_blk = min(n_sel, n_col * CMP_STRIDE // SEL_BLOCK)
        pl.when(step // steps_per_group == var)(functools.partial(
            _cmp_body, step, q_ref, k_ref, v_ref, ovt_ref, o_ref, sel_ref, n_col, n_blk))


def _cmp_body(step, q_ref, k_ref, v_ref, ovt_ref, o_ref, sel_ref, n_cmp, n_sel):
    tq = Q_BLOCK
    rows = NSA_REP * tq
    n_pad = -(-n_sel // SUBLANE) * SUBLANE
    blocks = range(CMP_Q_BLOCKS)
    t0s = [(step * CMP_Q_BLOCKS + x) * tq for x in blocks]
    kc = k_ref[0, 0, 0:n_cmp, :]
    ss = [_dot_nt(q_ref[0, :, x * tq:(x + 1) * tq, :].reshape(rows, NSA_DH), kc) for x in blocks]
    lag = (lax.broadcasted_iota(jnp.int32, (tq, n_cmp), 1) * CMP_STRIDE + (CMP_BLOCK - 1)
           - lax.broadcasted_iota(jnp.int32, (tq, n_cmp), 0))
    ss = [s + jnp.concatenate([jnp.where(lag <= t0, 0.0, NEG_BIG)] * NSA_REP, axis=0)
          for s, t0 in zip(ss, t0s)]
    ms = [jnp.max(s, axis=-1, keepdims=True) for s in ss]
    es = [jnp.exp2(s - m) for s, m in zip(ss, ms)]
    ls = [jnp.sum(e, axis=-1, keepdims=True) for e in es]
    i_col = lax.broadcasted_iota(jnp.int32, (rows, 1), 0) & (tq - 1)
    invs = [jnp.where(t0 + i_col >= CMP_BLOCK - 1, 1.0 / l, 0.0) for t0, l in zip(t0s, ls)]
    ps = [e * inv for e, inv in zip(es, invs)]
    vc = v_ref[0, 0, 0:n_cmp, :]
    for x in blocks:
        o = _dot(ps[x].astype(BF16), vc)
        o_ref[0, :, x * tq:(x + 1) * tq, :] = o.reshape(NSA_REP, tq, NSA_DH).astype(o_ref.dtype)

    ovt = ovt_ref[0:n_pad, 0:n_cmp]
    imps = []
    for p in ps:
        psum = p[0:tq]
        for r in range(1, NSA_REP):
            psum = psum + p[r * tq:(r + 1) * tq]
        ph, plo = _split2(psum)
        imps.append(_dot_nt(ovt, ph) + _dot_nt(ovt, plo))

    i_q = lax.broadcasted_iota(jnp.int32, (n_pad, tq), 1)
    blk = lax.broadcasted_iota(jnp.int32, (n_pad, tq), 0)
    blk_f = blk.astype(F32)
    scores = []
    for t0, imp in zip(t0s, imps):
        t_q = t0 + i_q
        cur = _blk(t_q, SEL_BLOCK)
        forced = (blk == 0) | (blk == cur) | (blk == cur - 1)
        score = jnp.where(blk * SEL_BLOCK <= t_q, jnp.where(forced, FORCED_SCORE, imp), -1.0)
        if n_pad > n_sel:
            score = jnp.where(blk < n_sel, score, -2.0)
        scores.append(score)
    taken = -float(2 ** 127)
    for _ in range(min(SEL_TOPK, n_sel)):
        mxs = [jnp.max(score, axis=0, keepdims=True) for score in scores]
        firsts = [jnp.min(jnp.where(score == mx, blk_f, float(LANE)), axis=0, keepdims=True)
                  for score, mx in zip(scores, mxs)]
        scores = [jnp.where(blk_f == first, taken, score)
                  for score, first in zip(scores, firsts)]
    for x in blocks:
        chosen = jnp.where(scores[x] == taken, 0.0, NEG_BIG)
        if n_pad < LANE:
            chosen = jnp.concatenate([chosen, jnp.full((LANE - n_pad, tq), NEG_BIG, F32)], axis=0)
        sel_ref[0, 0, x * tq:(x + 1) * tq, :] = chosen.T.astype(sel_ref.dtype)


def _cmp_attention(q, k_cmp, v_cmp, overlap_t, t_len):
    b_sz = q.shape[0]
    n_cmp = k_cmp.shape[2]
    tq = Q_BLOCK * CMP_Q_BLOCKS
    nq = t_len // tq
    return pl.pallas_call(
        functools.partial(_cmp_kernel, n_sel=t_len // SEL_BLOCK),
        grid=(b_sz, NSA_GROUPS, nq),
        in_specs=[pl.BlockSpec((1, NSA_REP, tq, NSA_DH), lambda b, g, i: (b, g, i, 0)),
                  pl.BlockSpec((1, 1, n_cmp, NSA_DH), lambda b, g, i: (b, g, 0, 0)),
                  pl.BlockSpec((1, 1, n_cmp, NSA_DH), lambda b, g, i: (b, g, 0, 0)),
                  pl.BlockSpec((LANE, n_cmp), lambda b, g, i: (0, 0))],
        out_specs=[pl.BlockSpec((1, NSA_REP, tq, NSA_DH), lambda b, g, i: (b, g, i, 0)),
                   pl.BlockSpec((1, 1, tq, LANE), lambda b, g, i: (b, g, i, 0))],
        out_shape=[jax.ShapeDtypeStruct((b_sz, NSA_HEADS, t_len, NSA_DH), BF16),
                   jax.ShapeDtypeStruct((b_sz, NSA_GROUPS, t_len, LANE), BF16)],
        compiler_params=_cparams("parallel", "parallel", "parallel"),
        name="cmp_attention_topk",
    )(q, k_cmp, v_cmp, overlap_t)


def _online_softmax_step(s, vt, m_ref, acc_ref):
    reps = s.shape[1] // LANE
    m_prev = m_ref[...]
    m_next = jnp.maximum(m_prev, jnp.max(s, axis=-1, keepdims=True))
    p = jnp.exp2(s - jnp.concatenate([m_next] * reps, axis=1))
    alpha = jnp.exp2(m_prev - m_next)
    acc_ref[...] = alpha * acc_ref[...] + _dot(p.astype(BF16), vt)
    m_ref[...] = m_next


def _sel_kernel(q_ref, k_ref, v_ref, sel_ref, o_ref, qa_ref, m_ref, acc_ref, sa_ref, sb_ref):
    qb = pl.program_id(2)
    tq = SEL_Q_TILE
    tk = SEL_KV_TILE
    rows = NSA_REP * tq
    mask = sel_ref[0, 0]
    for r in range(NSA_REP):
        qa_ref[r * tq:(r + 1) * tq, 0:LANE] = mask
        qa_ref[r * tq:(r + 1) * tq, LANE:LANE + NSA_DH] = q_ref[0, r]
        qa_ref[r * tq:(r + 1) * tq, LANE + NSA_DH:2 * LANE] = jnp.zeros((tq, NSA_DH), BF16)
    m_ref[...] = jnp.full(m_ref.shape, -jnp.inf, F32)
    acc_ref[...] = jnp.zeros(acc_ref.shape, F32)
    last = (qb * tq) // tk

    def scores(j):
        k0 = pl.multiple_of(j * tk, tk)
        return _dot_nt(qa_ref[...], k_ref[0, 0, pl.ds(k0, tk), :])

    def values(j):
        return v_ref[0, 0, pl.ds(pl.multiple_of(j * tk, tk), tk), :]

    def causal_step(s):
        row = lax.broadcasted_iota(jnp.int32, (rows, tk), 0)
        kpos = last * tk + lax.broadcasted_iota(jnp.int32, (rows, tk), 1)
        s = jnp.where(kpos <= qb * tq + (row & (tq - 1)), s, NEG_BIG)
        _online_softmax_step(s, values(last), m_ref, acc_ref)

    sa_ref[...] = scores(0)

    def pair(i, carry):
        sb_ref[...] = scores(2 * i + 1)
        _online_softmax_step(sa_ref[...], values(2 * i), m_ref, acc_ref)
        sa_ref[...] = scores(2 * i + 2)
        _online_softmax_step(sb_ref[...], values(2 * i + 1), m_ref, acc_ref)
        return carry

    lax.fori_loop(0, last // 2, pair, 0)

    @pl.when(last % 2 == 0)
    def _():
        causal_step(sa_ref[...])

    @pl.when(last % 2 == 1)
    def _():
        sb_ref[...] = scores(last)
        _online_softmax_step(sa_ref[...], values(last - 1), m_ref, acc_ref)
        causal_step(sb_ref[...])

    acc = acc_ref[...]
    o = acc[:, 0:NSA_DH] / acc[:, NSA_DH:NSA_DH + 1]
    o_ref[0] = o.reshape(NSA_REP, tq, NSA_DH).astype(o_ref.dtype)


def _sel_attention(q, k_sel, v_sel, sel_mask, t_len):
    b_sz = q.shape[0]
    tq = SEL_Q_TILE
    nq = t_len // tq
    rows = NSA_REP * tq
    return pl.pallas_call(
        _sel_kernel,
        grid=(b_sz, NSA_GROUPS, nq),
        in_specs=[pl.BlockSpec((1, NSA_REP, tq, NSA_DH), lambda b, g, i: (b, g, i, 0)),
                  pl.BlockSpec((1, 1, t_len, 2 * LANE), lambda b, g, i: (b, g, 0, 0)),
                  pl.BlockSpec((1, 1, t_len, LANE), lambda b, g, i: (b, g, 0, 0)),
                  pl.BlockSpec((1, 1, tq, LANE), lambda b, g, i: (b, g, i, 0))],
        out_specs=pl.BlockSpec((1, NSA_REP, tq, NSA_DH), lambda b, g, i: (b, g, i, 0)),
        out_shape=jax.ShapeDtypeStruct((b_sz, NSA_HEADS, t_len, NSA_DH), BF16),
        scratch_shapes=[pltpu.VMEM((rows, 2 * LANE), BF16), pltpu.VMEM((rows, LANE), F32),
                        pltpu.VMEM((rows, LANE), F32), pltpu.VMEM((rows, SEL_KV_TILE), F32),
                        pltpu.VMEM((rows, SEL_KV_TILE), F32)],
        compiler_params=_cparams("parallel", "parallel", "arbitrary"),
        name="sel_attention",
    )(q, k_sel, v_sel, sel_mask)


def _win_body(step, q_ref, k_ref, v_ref, o_ref, interior):
    tq = Q_BLOCK
    span = WINDOW + tq
    rows = NSA_REP * tq
    blocks = range(WIN_Q_BLOCKS)
    qbs = [step * WIN_Q_BLOCKS + x for x in blocks]
    starts = [pl.multiple_of(jnp.maximum(qb * tq - WINDOW, 0), tq) for qb in qbs]
    ss = [_dot_nt(q_ref[0, :, x * tq:(x + 1) * tq, :].reshape(rows, NSA_DH),
                  k_ref[0, 0, pl.ds(starts[x], span), :]) for x in blocks]
    if interior:
        i_row = lax.broadcasted_iota(jnp.int32, (rows, tq), 0) & (tq - 1)
        col = lax.broadcasted_iota(jnp.int32, (rows, tq), 1)
        ss = [jnp.concatenate([jnp.where(col > i_row, s[:, 0:tq], NEG_BIG), s[:, tq:WINDOW],
                               jnp.where(col <= i_row, s[:, WINDOW:span], NEG_BIG)], axis=1) for s in ss]
    else:
        i_row = lax.broadcasted_iota(jnp.int32, (rows, span), 0) & (tq - 1)
        col = lax.broadcasted_iota(jnp.int32, (rows, span), 1)
        masked = []
        for x in blocks:
            t_q = qbs[x] * tq + i_row
            kpos = starts[x] + col
            masked.append(jnp.where((kpos <= t_q) & (kpos > t_q - WINDOW), ss[x], NEG_BIG))
        ss = masked
    ms = [jnp.max(s, axis=-1, keepdims=True) for s in ss]
    ps = [jnp.exp2(s - m).astype(BF16) for s, m in zip(ss, ms)]
    accs = [_dot(ps[x], v_ref[0, 0, pl.ds(starts[x], span), :]) for x in blocks]
    for x in blocks:
        o = accs[x][:, 0:NSA_DH] / accs[x][:, NSA_DH:NSA_DH + 1]
        o_ref[0, :, x * tq:(x + 1) * tq, :] = o.reshape(NSA_REP, tq, NSA_DH).astype(o_ref.dtype)


def _win_kernel(q_ref, k_ref, v_ref, o_ref):
    step = pl.program_id(2)
    edge_steps = -(-(WINDOW // Q_BLOCK) // WIN_Q_BLOCKS)
    pl.when(step < edge_steps)(functools.partial(_win_body, step, q_ref, k_ref, v_ref, o_ref, False))
    pl.when(step >= edge_steps)(functools.partial(_win_body, step, q_ref, k_ref, v_ref, o_ref, True))


def _win_attention(q, k_win, v_win, t_len):
    b_sz = q.shape[0]
    tq = Q_BLOCK * WIN_Q_BLOCKS
    nq = t_len // tq
    return pl.pallas_call(
        _win_kernel,
        grid=(b_sz, NSA_GROUPS, nq),
        in_specs=[pl.BlockSpec((1, NSA_REP, tq, NSA_DH), lambda b, g, i: (b, g, i, 0)),
                  pl.BlockSpec((1, 1, t_len, NSA_DH), lambda b, g, i: (b, g, 0, 0)),
                  pl.BlockSpec((1, 1, t_len, LANE), lambda b, g, i: (b, g, 0, 0))],
        out_specs=pl.BlockSpec((1, NSA_REP, tq, NSA_DH), lambda b, g, i: (b, g, i, 0)),
        out_shape=jax.ShapeDtypeStruct((b_sz, NSA_HEADS, t_len, NSA_DH), BF16),
        compiler_params=_cparams("parallel", "parallel", "parallel"),
        name="win_attention",
    )(q, k_win, v_win)


def _merge_kernel(oc_ref, os_ref, ow_ref, zc_ref, zs_ref, zw_ref, gt_ref, ge_ref,
                  w_ref, x_ref, lw_ref, lb_ref, o_ref):
    gates = _sigmoid(gt_ref[...])
    g2 = jnp.concatenate(_split2(gates), axis=1)
    acc = None
    for n, (o_r, z_r) in enumerate(((oc_ref, zc_ref), (os_ref, zs_ref), (ow_ref, zw_ref))):
        gate = _dot(g2, ge_ref[n])
        o = jnp.concatenate([o_r[0, h] for h in range(NSA_HEADS)], axis=-1).astype(F32)
        term = gate * o * _silu(z_r[...])
        acc = term if acc is None else acc + term
    y = _dot(acc.astype(BF16), w_ref[...])
    r = DEEPNORM_ALPHA * x_ref[...] + y
    o_ref[...] = _layer_norm_rows(r, lw_ref[...], lb_ref[...])


def _merge_out(o_cmp, o_sel, o_win, z, tail, gate_expand, w_out, x_res, ln_w, ln_b, t_len, tm):
    m, d = x_res.shape
    width = NSA_HEADS * NSA_DH
    nt = t_len // tm
    ospec = pl.BlockSpec((1, NSA_HEADS, tm, NSA_DH), lambda i: (i // nt, 0, i % nt, 0))
    zspec = lambda col: pl.BlockSpec((tm, width), lambda i: (i, col))
    fixed2 = lambda i: (0, 0)
    return pl.pallas_call(
        _merge_kernel,
        grid=(m // tm,),
        in_specs=[ospec, ospec, ospec, zspec(0), zspec(1), zspec(2),
                  pl.BlockSpec((tm, LANE), lambda i: (i, 0)),
                  pl.BlockSpec((3, 2 * LANE, width), lambda i: (0, 0, 0)),
                  pl.BlockSpec((width, d), fixed2),
                  pl.BlockSpec((tm, d), lambda i: (i, 0)),
                  pl.BlockSpec((1, d), fixed2), pl.BlockSpec((1, d), fixed2)],
        out_specs=pl.BlockSpec((tm, d), lambda i: (i, 0)),
        out_shape=jax.ShapeDtypeStruct((m, d), F32),
        compiler_params=_cparams("parallel"),
        name="merge_out_proj_layernorm",
    )(o_cmp, o_sel, o_win, z, z, z, tail, gate_expand, w_out, x_res,
      ln_w.reshape(1, d), ln_b.reshape(1, d))


def _pad_cols(w, n):
    return jnp.pad(w, ((0, 0), (0, n - w.shape[1])))


def kernel(x, a_w_in, a_conv_w, a_a_log, a_dt_bias, a_norm_w, a_w_out, a_ln_w, a_ln_b,
           s_w_kv, s_pe_k, s_pe_v, s_w1_k, s_w2_k, s_w1_v, s_w2_v,
           b_w_in, b_w_out, b_ln_w, b_ln_b):
    b_sz, t_len, d = x.shape
    m = b_sz * t_len
    assert t_len % SEL_KV_TILE == 0 and t_len >= WINDOW + Q_BLOCK and d % LANE == 0
    assert t_len % (Q_BLOCK * WIN_Q_BLOCKS) == 0 and t_len % (Q_BLOCK * CMP_Q_BLOCKS) == 0
    assert t_len // SEL_BLOCK <= LANE
    assert a_w_in.shape[0] == 1 and b_w_in.shape[0] == 1
    tm_row = min(ROW_TILE, t_len)

    x2 = x.reshape(m, d)
    x16 = x2.astype(BF16)

    dn_width = DN_HEADS * DN_HEAD
    w_in = a_w_in[0]
    qkv = _gdn_qkv(x16, w_in[:, :3 * dn_width].astype(BF16), a_conv_w[0], t_len, tm_row)
    z_gdn = _matmul(x16, w_in[:, 3 * dn_width:4 * dn_width].astype(BF16))
    tail = _matmul(x16, _pad_cols(w_in[:, 4 * dn_width:], LANE).astype(BF16))
    gates_col = _gdn_gates(tail, a_a_log[0], a_dt_bias[0], min(PROJ_ROWS, m))
    nc = t_len // GDN_CHUNK
    gates_row = gates_col[:, :2 * DN_HEADS].reshape(b_sz, nc, GDN_CHUNK, 2 * DN_HEADS).transpose(0, 1, 3, 2)
    o_gdn = _gdn_chunks(qkv, z_gdn, gates_col, gates_row, a_norm_w[0], b_sz, t_len)
    h1, h1_16 = _out_proj_ln(o_gdn, a_w_out[0].astype(BF16), x2, a_ln_w[0], a_ln_b[0], tm_row)

    pos = jnp.arange(t_len, dtype=jnp.int32)
    cos_t, sin_t = _rope_tables(pos)
    kv_cmp_raw, k_sel, v_sel, k_win, v_win = _kv_proj(h1_16, s_w_kv.astype(BF16), cos_t, sin_t,
                                                      b_sz, t_len, tm_row)

    n_chunk = t_len // CMP_STRIDE
    halves = CMP_BLOCK // CMP_STRIDE
    pe = jnp.stack([s_pe_k, s_pe_v]).astype(F32).reshape(2, halves, CMP_STRIDE, 1, NSA_DH)
    pe = jnp.broadcast_to(pe, (2, halves, CMP_STRIDE, 2, NSA_DH)).reshape(2, halves, CMP_STRIDE * LANE)
    w1 = jnp.stack([s_w1_k, s_w1_v]).reshape(2, halves, CMP_STRIDE, NSA_DH, CMP_HIDDEN)
    w1 = jnp.einsum('ab,shtdk->shtadbk', jnp.eye(2, dtype=w1.dtype), w1)
    w1 = w1.reshape(2, halves, CMP_STRIDE * LANE, 2 * CMP_HIDDEN).astype(BF16)
    w2 = jnp.stack([_pad_cols(s_w2_k, LANE), _pad_cols(s_w2_v, LANE)]).astype(BF16)
    cmp_end = jnp.arange(n_chunk, dtype=jnp.int32) * CMP_STRIDE + CMP_BLOCK - 1
    cos_c, sin_c = _rope_tables(cmp_end)
    kv_cmp = _compress(kv_cmp_raw, pe, w1, w2, cos_c, sin_c, b_sz, t_len)
    k_cmp, v_cmp = kv_cmp[0], kv_cmp[1]

    nsa_width = NSA_HEADS * NSA_DH
    w_in_b = b_w_in[0]
    q = _q_proj(h1_16, w_in_b[:, :nsa_width].astype(BF16), cos_t, sin_t, b_sz, t_len, tm_row)
    z_b = _matmul(h1_16, w_in_b[:, nsa_width:4 * nsa_width].astype(BF16))
    tail_b = _matmul(h1_16, _pad_cols(w_in_b[:, 4 * nsa_width:], LANE).astype(BF16))

    n_sel = t_len // SEL_BLOCK
    c_start = jnp.arange(n_chunk, dtype=jnp.int32)[:, None] * CMP_STRIDE
    s_start = jnp.arange(n_sel, dtype=jnp.int32)[None, :] * SEL_BLOCK
    ov = jnp.clip(jnp.minimum(c_start + CMP_BLOCK, s_start + SEL_BLOCK) - jnp.maximum(c_start, s_start), 0, None)
    overlap = (ov.astype(F32) / CMP_BLOCK).astype(BF16)
    overlap_t = jnp.pad(overlap.T, ((0, LANE - n_sel), (0, 0)))

    o_cmp, sel_mask = _cmp_attention(q, k_cmp, v_cmp, overlap_t, t_len)
    o_sel = _sel_attention(q, k_sel, v_sel, sel_mask, t_len)
    o_win = _win_attention(q, k_win, v_win, t_len)

    lane_id = (jnp.arange(2 * LANE, dtype=jnp.int32) % LANE)[None, :, None]
    col_head = (jnp.arange(NSA_HEADS * NSA_DH, dtype=jnp.int32) // NSA_DH)[None, None, :]
    branch = jnp.arange(3, dtype=jnp.int32)[:, None, None]
    gate_expand = (lane_id == branch * NSA_HEADS + col_head).astype(BF16)
    out = _merge_out(o_cmp, o_sel, o_win, z_b, tail_b, gate_expand, b_w_out[0].astype(BF16),
                     h1, b_ln_w[0], b_ln_b[0], t_len, tm_row)
    return out.reshape(b_sz, t_len, d)
```

```python
import functools

import jax
import jax.numpy as jnp
from jax import lax
from jax.experimental import pallas as pl
from jax.experimental.pallas import tpu as pltpu

F32 = jnp.float32
BF16 = jnp.bfloat16

DN_HEADS = 8
DN_HEAD = 128
DN_CONV = 4
NSA_HEADS = 16
NSA_GROUPS = 4
NSA_REP = NSA_HEADS // NSA_GROUPS
NSA_DH = 64
CMP_BLOCK = 32
CMP_STRIDE = 16
CMP_HIDDEN = 128
SEL_BLOCK = 64
SEL_TOPK = 16
WINDOW = 512
Q_BLOCK = 128
FORCED_SCORE = 1.0e4
ROPE_THETA = 10000.0
NORM_EPS = 1e-6
DEPTH = 2
DEEPNORM_ALPHA = (2.0 * DEPTH) ** 0.25

GDN_CHUNK = 128
INV_BASE = 16
GDN_SEQS = 2
LANE = 128
SUBLANE = 8
PROJ_ROWS = 1024
ROW_TILE = 512
SEL_KV_TILE = 512
SEL_Q_TILE = 512
CMP_Q_BLOCKS = 8
WIN_Q_BLOCKS = 8
NEG_BIG = -1e30
LOG2_E = 1.4426950408889634
VMEM_LIMIT = 56 * 1024 * 1024


def _cparams(*sem):
    return pltpu.CompilerParams(dimension_semantics=sem, vmem_limit_bytes=VMEM_LIMIT)


def _dot(a, b):
    return jnp.dot(a, b, preferred_element_type=F32)


def _dot_nt(a, b):
    return lax.dot_general(a, b, (((1,), (1,)), ((), ())), preferred_element_type=F32)


def _dot_tn(a, b):
    return lax.dot_general(a, b, (((0,), (0,)), ((), ())), preferred_element_type=F32)


def _split2(a):
    hi = a.astype(BF16)
    lo = (a - hi.astype(F32)).astype(BF16)
    return hi, lo


def _split3(a):
    hi = a.astype(BF16)
    r = a - hi.astype(F32)
    mid = r.astype(BF16)
    lo = (r - mid.astype(F32)).astype(BF16)
    return hi, mid, lo


def _mm(a, b):
    return _dot(a.astype(BF16), b.astype(BF16))


def _blk(idx, size):
    return idx >> (size.bit_length() - 1)


def _silu(x):
    return x / (1.0 + jnp.exp(-x))


def _sigmoid(x):
    return 1.0 / (1.0 + jnp.exp(-x))


def _layer_norm_rows(r, w, b):
    mu = jnp.mean(r, axis=-1, keepdims=True)
    d = r - mu
    var = jnp.mean(d * d, axis=-1, keepdims=True)
    return d * lax.rsqrt(var + NORM_EPS) * w + b


def _out_ln_kernel(a_ref, w_ref, x_ref, lw_ref, lb_ref, o_ref, ob_ref):
    y = _dot(a_ref[...], w_ref[...])
    r = DEEPNORM_ALPHA * x_ref[...] + y
    out = _layer_norm_rows(r, lw_ref[...], lb_ref[...])
    o_ref[...] = out
    ob_ref[...] = out.astype(BF16)


def _out_proj_ln(a, w, x_res, ln_w, ln_b, tm):
    m, k = a.shape
    n = w.shape[1]
    row = lambda i: (i, 0)
    fixed = lambda i: (0, 0)
    return pl.pallas_call(
        _out_ln_kernel,
        grid=(m // tm,),
        in_specs=[pl.BlockSpec((tm, k), row), pl.BlockSpec((k, n), fixed),
                  pl.BlockSpec((tm, n), row), pl.BlockSpec((1, n), fixed),
                  pl.BlockSpec((1, n), fixed)],
        out_specs=[pl.BlockSpec((tm, n), row), pl.BlockSpec((tm, n), row)],
        out_shape=[jax.ShapeDtypeStruct((m, n), F32), jax.ShapeDtypeStruct((m, n), BF16)],
        compiler_params=_cparams("parallel"),
        name="out_proj_layernorm",
    )(a, w, x_res, ln_w.reshape(1, n), ln_b.reshape(1, n))


def _gdn_qkv_kernel(x_ref, w_ref, cw_ref, o_ref, halo_ref, *, tm, t_len):
    j = pl.program_id(0)
    i = pl.program_id(1)
    @pl.when((i * tm) % t_len == 0)
    def _():
        halo_ref[...] = jnp.zeros(halo_ref.shape, F32)

    p = _dot(x_ref[...], w_ref[...])
    halo = halo_ref[...]
    halo_ref[...] = p[tm - SUBLANE:tm, :]
    cw = cw_ref[...]
    row = lax.broadcasted_iota(jnp.int32, halo.shape, 0)
    y = p * cw[DN_CONV - 1:DN_CONV, :]
    for back in range(1, DN_CONV):
        shifted = pltpu.roll(p, back, axis=0)
        head = jnp.where(row < back, pltpu.roll(halo, back, axis=0), shifted[0:SUBLANE, :])
        shifted = jnp.concatenate([head, shifted[SUBLANE:, :]], axis=0)
        y = y + shifted * cw[DN_CONV - 1 - back:DN_CONV - back, :]
    y = _silu(y)
    qscale = jnp.where(j == 0, DN_HEAD ** -0.5, 1.0)
    for h in range(DN_HEADS):
        sl = slice(h * DN_HEAD, (h + 1) * DN_HEAD)
        yh = y[:, sl]
        ss = jnp.sum(yh * yh, axis=-1, keepdims=True)
        scale = jnp.where(j < 2, lax.rsqrt(ss + NORM_EPS) * qscale, 1.0)
        o_ref[:, sl] = yh * scale


def _gdn_qkv(x16, w_qkv, conv_w, t_len, tm):
    m, d = x16.shape
    width = DN_HEADS * DN_HEAD
    kern = functools.partial(_gdn_qkv_kernel, tm=tm, t_len=t_len)
    return pl.pallas_call(
        kern,
        grid=(3, m // tm),
        in_specs=[pl.BlockSpec((tm, d), lambda j, i: (i, 0)),
                  pl.BlockSpec((d, width), lambda j, i: (0, j)),
                  pl.BlockSpec((DN_CONV, width), lambda j, i: (0, j))],
        out_specs=pl.BlockSpec((tm, width), lambda j, i: (i, j)),
        out_shape=jax.ShapeDtypeStruct((m, 3 * width), F32),
        scratch_shapes=[pltpu.VMEM((SUBLANE, width), F32)],
        compiler_params=_cparams("parallel", "arbitrary"),
        name="gdn_qkv_proj_conv",
    )(x16, w_qkv, conv_w)


def _gdn_zgate_kernel(x_ref, w_ref, al_ref, dt_ref, z_ref, g_ref):
    width = DN_HEADS * DN_HEAD
    proj = _dot(x_ref[...], w_ref[...])
    z_ref[...] = proj[:, 0:width]
    x = proj[:, width:]
    lane = lax.broadcasted_iota(jnp.int32, x.shape, 1)
    beta = _sigmoid(x)
    z = x + dt_ref[...]
    softplus = jnp.maximum(z, 0.0) + jnp.log(1.0 + jnp.exp(-jnp.abs(z)))
    g = -jnp.exp(al_ref[...]) * softplus
    g_ref[...] = jnp.where(lane < DN_HEADS, beta, jnp.where(lane < 2 * DN_HEADS, g, 0.0))


def _gdn_zgates(x16, w_z_logits, a_log, dt_bias, tm):
    m, d = x16.shape
    width = DN_HEADS * DN_HEAD
    pad = lambda v: jnp.zeros((1, LANE), F32).at[0, DN_HEADS:2 * DN_HEADS].set(v.astype(F32))
    return pl.pallas_call(
        _gdn_zgate_kernel,
        grid=(m // tm,),
        in_specs=[pl.BlockSpec((tm, d), lambda i: (i, 0)),
                  pl.BlockSpec((d, width + LANE), lambda i: (0, 0)),
                  pl.BlockSpec((1, LANE), lambda i: (0, 0)),
                  pl.BlockSpec((1, LANE), lambda i: (0, 0))],
        out_specs=[pl.BlockSpec((tm, width), lambda i: (i, 0)),
                   pl.BlockSpec((tm, LANE), lambda i: (i, 0))],
        out_shape=[jax.ShapeDtypeStruct((m, width), F32), jax.ShapeDtypeStruct((m, LANE), F32)],
        compiler_params=_cparams("parallel"),
        name="gdn_z_gates",
    )(x16, w_z_logits, pad(a_log), pad(dt_bias))


def _inv_unit_lower(lows, ii, jj):
    c = lows[0].shape[0]
    eye = (ii == jj).astype(F32)
    same = _blk(ii, INV_BASE) == _blk(jj, INV_BASE)
    ds = [jnp.where(same, low, 0.0) for low in lows]
    xs = [eye - d for d in ds]
    ps = ds
    span = 2
    while span < INV_BASE:
        ps = [_mm(p, p) for p in ps]
        xs = [_mm(x, eye + p) for x, p in zip(xs, ps)]
        span *= 2
    s = INV_BASE
    while s < c:
        off = (_blk(ii, 2 * s) == _blk(jj, 2 * s)) & (_blk(ii, s) != _blk(jj, s))
        ts = [_mm(x, jnp.where(off, low, 0.0)) for x, low in zip(xs, lows)]
        xs = [x - _mm(t, x) for x, t in zip(xs, ts)]
        s *= 2
    return xs


def _gdn_chunk_kernel(q_ref, k_ref, v_ref, z_ref, gc_ref, gr_ref, nw_ref, o_ref, s_ref):
    c = GDN_CHUNK

    @pl.when(pl.program_id(1) == 0)
    def _():
        s_ref[...] = jnp.zeros_like(s_ref)

    ii = lax.broadcasted_iota(jnp.int32, (c, c), 0)
    jj = lax.broadcasted_iota(jnp.int32, (c, c), 1)
    incl = ii >= jj
    strict = ii > jj
    ltri = incl.astype(BF16)
    utri = (ii <= jj).astype(BF16)
    nb = q_ref.shape[0]
    gcols, gcs_cols, gcs_rows = [], [], []
    for bb in range(nb):
        gcol = gc_ref[bb]
        gc3 = _split3(gcol)
        gr3 = _split3(gr_ref[bb, 0])
        gcols.append(gcol)
        gcs_cols.append(_dot(ltri, gc3[0]) + (_dot(ltri, gc3[1]) + _dot(ltri, gc3[2])))
        gcs_rows.append(_dot(gr3[0], utri) + (_dot(gr3[1], utri) + _dot(gr3[2], utri)))
    nw = nw_ref[...]

    chains = [(bb, h) for bb in range(nb) for h in range(DN_HEADS)]
    tile = lambda ref, bb, h: ref[bb, :, h * DN_HEAD:(h + 1) * DN_HEAD]
    betas = [gcols[bb][:, h:h + 1] for bb, h in chains]
    gccs = [gcs_cols[bb][:, DN_HEADS + h:DN_HEADS + h + 1] for bb, h in chains]
    gcrs = [gcs_rows[bb][DN_HEADS + h:DN_HEADS + h + 1, :] for bb, h in chains]
    decays = [jnp.where(incl, jnp.exp(jnp.where(incl, gcc - gcr, 0.0)), 0.0)
              for gcc, gcr in zip(gccs, gcrs)]
    kbs = [tile(k_ref, bb, h) * beta for (bb, h), beta in zip(chains, betas)]
    k16s = [tile(k_ref, bb, h).astype(BF16) for bb, h in chains]
    lows = [jnp.where(strict, _dot_nt(kb.astype(BF16), k16) * decay, 0.0)
            for kb, k16, decay in zip(kbs, k16s, decays)]
    attns = [jnp.where(incl, _dot_nt(tile(q_ref, bb, h).astype(BF16), k16) * decay, 0.0).astype(BF16)
             for (bb, h), k16, decay in zip(chains, k16s, decays)]
    tinvs = _inv_unit_lower(lows, ii, jj)
    egs = [jnp.exp(gcc) for gcc in gccs]
    sols = [_mm(tinv, jnp.concatenate([tile(v_ref, bb, h) * beta, kb * eg], axis=1))
            for tinv, (bb, h), beta, kb, eg in zip(tinvs, chains, betas, kbs, egs)]
    g_lasts = [gcc[c - 1:c, :] for gcc in gccs]
    kds = [(tile(k_ref, bb, h) * jnp.exp(g_last - gcc)).astype(BF16)
           for (bb, h), g_last, gcc in zip(chains, g_lasts, gccs)]
    qgs = [(tile(q_ref, bb, h) * eg).astype(BF16) for (bb, h), eg in zip(chains, egs)]
    states = [s_ref[bb * DN_HEADS + h] for bb, h in chains]
    s16s = [state.astype(BF16) for state in states]
    vn16s = [(sol[:, :DN_HEAD] - _dot(sol[:, DN_HEAD:].astype(BF16), s16)).astype(BF16)
             for sol, s16 in zip(sols, s16s)]
    outs = [_dot(qg, s16) + _dot(attn, vn16)
            for qg, s16, attn, vn16 in zip(qgs, s16s, attns, vn16s)]
    for n, (bb, h) in enumerate(chains):
        s_ref[bb * DN_HEADS + h] = states[n] * jnp.exp(g_lasts[n]) + _dot_tn(kds[n], vn16s[n])
    for n, (bb, h) in enumerate(chains):
        o = outs[n]
        on = o * lax.rsqrt(jnp.mean(o * o, axis=-1, keepdims=True) + NORM_EPS) * nw
        o_ref[bb, :, h * DN_HEAD:(h + 1) * DN_HEAD] = (on * _silu(tile(z_ref, bb, h))).astype(o_ref.dtype)


def _gdn_chunks(qkv, z, gates_col, gates_row, norm_w, b_sz, t_len):
    c = GDN_CHUNK
    nc = t_len // c
    width = DN_HEADS * DN_HEAD
    nb = GDN_SEQS if b_sz % GDN_SEQS == 0 else 1
    seq = lambda a: a.reshape(b_sz, t_len, a.shape[-1])
    tok = lambda col: (lambda b, i: (b, i, col))
    out = pl.pallas_call(
        _gdn_chunk_kernel,
        grid=(b_sz // nb, nc),
        in_specs=[pl.BlockSpec((nb, c, width), tok(0)), pl.BlockSpec((nb, c, width), tok(1)),
                  pl.BlockSpec((nb, c, width), tok(2)), pl.BlockSpec((nb, c, width), tok(0)),
                  pl.BlockSpec((nb, c, LANE), tok(0)),
                  pl.BlockSpec((nb, 1, 2 * DN_HEADS, c), lambda b, i: (b, i, 0, 0)),
                  pl.BlockSpec((1, DN_HEAD), lambda b, i: (0, 0))],
        out_specs=pl.BlockSpec((nb, c, width), tok(0)),
        out_shape=jax.ShapeDtypeStruct((b_sz, t_len, width), BF16),
        scratch_shapes=[pltpu.VMEM((nb * DN_HEADS, DN_HEAD, DN_HEAD), F32)],
        compiler_params=_cparams("parallel", "arbitrary"),
        name="gdn_delta_rule",
    )(seq(qkv), seq(qkv), seq(qkv), seq(z), seq(gates_col), gates_row,
      norm_w.reshape(1, DN_HEAD).astype(F32))
    return out.reshape(b_sz * t_len, width)


def _rope_tables(pos):
    half = NSA_DH // 2
    inv_freq = ROPE_THETA ** (-jnp.arange(half, dtype=F32) / half)
    ang = pos.astype(F32)[:, None] * inv_freq[None, :]
    cos, sin = jnp.cos(ang), jnp.sin(ang)
    cos_t = jnp.concatenate([cos, cos, cos, cos], axis=-1)
    sin_t = jnp.concatenate([-sin, sin, -sin, sin], axis=-1)
    return cos_t, sin_t


def _rope128(x, cos_t, sin_t):
    lane = lax.broadcasted_iota(jnp.int32, x.shape, 1)
    lower = (lane & (NSA_DH - 1)) < (NSA_DH // 2)
    fwd = pltpu.roll(x, LANE - NSA_DH // 2, axis=1)
    bwd = pltpu.roll(x, NSA_DH // 2, axis=1)
    return x * cos_t + jnp.where(lower, fwd, bwd) * sin_t


def _kv_proj_kernel(x_ref, w_ref, cos_ref, sin_ref, cmp_ref, ks_ref, vs_ref, kw_ref, vw_ref, *, tm, nt):
    gw = NSA_GROUPS * NSA_DH
    kv = _dot(x_ref[...], w_ref[...])
    cmp_ref[...] = kv[:, 0:2 * gw]
    cos_t = cos_ref[...]
    sin_t = sin_ref[...]
    t_pos = (pl.program_id(0) % nt) * tm + lax.broadcasted_iota(jnp.int32, (tm, LANE), 0)
    lane = lax.broadcasted_iota(jnp.int32, (tm, LANE), 1)
    onehot = (_blk(t_pos, SEL_BLOCK) == lane).astype(BF16)
    ones_col = (lane == NSA_DH).astype(BF16)
    zeros_half = jnp.zeros((tm, NSA_DH), BF16)
    for pair in range(NSA_GROUPS // 2):
        cols = lambda src: kv[:, src * gw + pair * LANE: src * gw + (pair + 1) * LANE]
        k_sel = _rope128(cols(2), cos_t, sin_t).astype(BF16)
        k_win = _rope128(cols(4), cos_t, sin_t).astype(BF16)
        v_sel = cols(3).astype(BF16)
        v_win = cols(5).astype(BF16)
        for half in range(2):
            g = 2 * pair + half
            sl = slice(half * NSA_DH, (half + 1) * NSA_DH)
            ks_ref[0, g, :, 0:LANE] = onehot
            ks_ref[0, g, :, LANE:LANE + NSA_DH] = k_sel[:, sl]
            ks_ref[0, g, :, LANE + NSA_DH:2 * LANE] = zeros_half
            kw_ref[0, g] = k_win[:, sl]
            for val, dst in ((v_sel, vs_ref), (v_win, vw_ref)):
                dst[0, g] = ones_col
                dst[0, g, :, 0:NSA_DH] = val[:, sl]


def _kv_proj(h16, w_kv, cos_t, sin_t, b_sz, t_len, tm):
    nt = t_len // tm
    m, d = h16.shape
    gw = NSA_GROUPS * NSA_DH
    shape = lambda w: jax.ShapeDtypeStruct((b_sz, NSA_GROUPS, t_len, w), BF16)
    ospec = lambda w: pl.BlockSpec((1, NSA_GROUPS, tm, w), lambda i: (i // nt, 0, i % nt, 0))
    return pl.pallas_call(
        functools.partial(_kv_proj_kernel, tm=tm, nt=nt),
        grid=(b_sz * nt,),
        in_specs=[pl.BlockSpec((tm, d), lambda i: (i, 0)),
                  pl.BlockSpec((d, 6 * gw), lambda i: (0, 0)),
                  pl.BlockSpec((tm, LANE), lambda i: (i % nt, 0)),
                  pl.BlockSpec((tm, LANE), lambda i: (i % nt, 0))],
        out_specs=[pl.BlockSpec((tm, 2 * gw), lambda i: (i, 0)),
                   ospec(2 * LANE), ospec(LANE), ospec(NSA_DH), ospec(LANE)],
        out_shape=[jax.ShapeDtypeStruct((m, 2 * gw), F32),
                   shape(2 * LANE), shape(LANE), shape(NSA_DH), shape(LANE)],
        compiler_params=_cparams("parallel"),
        name="kv_proj_rope_layout",
    )(h16, w_kv, cos_t, sin_t)


def _nsa_proj_kernel(x_ref, w_ref, cos_ref, sin_ref, q_ref, z_ref, t_ref):
    width = NSA_HEADS * NSA_DH
    proj = _dot(x_ref[...], w_ref[...])
    z_ref[...] = proj[:, width:4 * width]
    t_ref[...] = proj[:, 4 * width:]
    cos_t = cos_ref[...]
    sin_t = sin_ref[...]
    scale = NSA_DH ** -0.5 * LOG2_E
    for pair in range(NSA_HEADS // 2):
        x = _rope128(proj[:, pair * LANE:(pair + 1) * LANE], cos_t, sin_t) * scale
        q_ref[0, 2 * pair] = x[:, :NSA_DH].astype(BF16)
        q_ref[0, 2 * pair + 1] = x[:, NSA_DH:].astype(BF16)


def _nsa_proj(h16, w_all, cos_t, sin_t, b_sz, t_len, tm):
    nt = t_len // tm
    m, d = h16.shape
    width = NSA_HEADS * NSA_DH
    return pl.pallas_call(
        _nsa_proj_kernel,
        grid=(b_sz * nt,),
        in_specs=[pl.BlockSpec((tm, d), lambda i: (i, 0)),
                  pl.BlockSpec((d, 4 * width + LANE), lambda i: (0, 0)),
                  pl.BlockSpec((tm, LANE), lambda i: (i % nt, 0)),
                  pl.BlockSpec((tm, LANE), lambda i: (i % nt, 0))],
        out_specs=[pl.BlockSpec((1, NSA_HEADS, tm, NSA_DH), lambda i: (i // nt, 0, i % nt, 0)),
                   pl.BlockSpec((tm, 3 * width), lambda i: (i, 0)),
                   pl.BlockSpec((tm, LANE), lambda i: (i, 0))],
        out_shape=[jax.ShapeDtypeStruct((b_sz, NSA_HEADS, t_len, NSA_DH), BF16),
                   jax.ShapeDtypeStruct((m, 3 * width), F32),
                   jax.ShapeDtypeStruct((m, LANE), F32)],
        compiler_params=_cparams("parallel"),
        name="nsa_proj_rope_layout",
    )(h16, w_all, cos_t, sin_t)


def _compress_kernel(x_ref, pe_ref, w1_ref, w2_ref, cos_ref, sin_ref, o_ref, *, n_chunk):
    is_key = pl.program_id(0) == 0
    chunk = jnp.concatenate([x_ref[pl.ds(tau, n_chunk, stride=CMP_STRIDE), :]
                             for tau in range(CMP_STRIDE)], axis=1)
    a = _dot((chunk + pe_ref[0, 0:1, :]).astype(BF16), w1_ref[0, 0])
    b = _dot((chunk + pe_ref[0, 1:2, :]).astype(BF16), w1_ref[0, 1])
    hid = a + pltpu.roll(b, n_chunk - 1, axis=0)
    act = _silu(hid).astype(BF16)
    for half in range(2):
        y = _dot(act[:, half * CMP_HIDDEN:(half + 1) * CMP_HIDDEN], w2_ref[0])
        y = jnp.where(is_key, _rope128(y, cos_ref[...], sin_ref[...]), y)
        o_ref[0, 0, half] = y[:, :NSA_DH].astype(BF16)


def _compress(kv_raw, pe, w1, w2, cos_c, sin_c, b_sz, t_len):
    n_chunk = t_len // CMP_STRIDE
    pairs = NSA_GROUPS // 2
    flat = CMP_STRIDE * LANE
    return pl.pallas_call(
        functools.partial(_compress_kernel, n_chunk=n_chunk),
        grid=(2, b_sz, pairs),
        in_specs=[pl.BlockSpec((t_len, LANE), lambda s, b, p: (b, s * pairs + p)),
                  pl.BlockSpec((1, 2, flat), lambda s, b, p: (s, 0, 0)),
                  pl.BlockSpec((1, 2, flat, 2 * CMP_HIDDEN), lambda s, b, p: (s, 0, 0, 0)),
                  pl.BlockSpec((1, CMP_HIDDEN, LANE), lambda s, b, p: (s, 0, 0)),
                  pl.BlockSpec((n_chunk, LANE), lambda s, b, p: (0, 0)),
                  pl.BlockSpec((n_chunk, LANE), lambda s, b, p: (0, 0))],
        out_specs=pl.BlockSpec((1, 1, 2, n_chunk, NSA_DH), lambda s, b, p: (s, b, p, 0, 0)),
        out_shape=jax.ShapeDtypeStruct((2, b_sz, NSA_GROUPS, n_chunk, NSA_DH), BF16),
        compiler_params=_cparams("parallel", "parallel", "parallel"),
        name="compress_blocks",
    )(kv_raw, pe, w1, w2, cos_c, sin_c)


def _cmp_kernel(q_ref, k_ref, v_ref, ovt_ref, o_ref, sel_ref, *, n_sel):
    step = pl.program_id(2)
    n_cmp = k_ref.shape[2]
    steps_per_group = LANE * CMP_STRIDE // (Q_BLOCK * CMP_Q_BLOCKS)
    n_var = max(1, n_cmp // LANE)
    for var in range(n_var):
        n_col = min(n_cmp, LANE * (var + 1))
        n_blk = min(n_sel, n_col * CMP_STRIDE // SEL_BLOCK)
        pl.when(step // steps_per_group == var)(functools.partial(
            _cmp_body, step, q_ref, k_ref, v_ref, ovt_ref, o_ref, sel_ref, n_col, n_blk))


def _cmp_body(step, q_ref, k_ref, v_ref, ovt_ref, o_ref, sel_ref, n_cmp, n_sel):
    tq = Q_BLOCK
    rows = NSA_REP * tq
    n_pad = -(-n_sel // SUBLANE) * SUBLANE
    blocks = range(CMP_Q_BLOCKS)
    t0s = [(step * CMP_Q_BLOCKS + x) * tq for x in blocks]
    kc = k_ref[0, 0, 0:n_cmp, :]
    ss = [_dot_nt(q_ref[0, :, x * tq:(x + 1) * tq, :].reshape(rows, NSA_DH), kc) for x in blocks]
    lag = (lax.broadcasted_iota(jnp.int32, (tq, n_cmp), 1) * CMP_STRIDE + (CMP_BLOCK - 1)
           - lax.broadcasted_iota(jnp.int32, (tq, n_cmp), 0))
    ss = [s + jnp.concatenate([jnp.where(lag <= t0, 0.0, NEG_BIG)] * NSA_REP, axis=0)
          for s, t0 in zip(ss, t0s)]
    ms = [jnp.max(s, axis=-1, keepdims=True) for s in ss]
    es = [jnp.exp2(s - m) for s, m in zip(ss, ms)]
    ls = [jnp.sum(e, axis=-1, keepdims=True) for e in es]
    i_col = lax.broadcasted_iota(jnp.int32, (rows, 1), 0) & (tq - 1)
    invs = [jnp.where(t0 + i_col >= CMP_BLOCK - 1, 1.0 / l, 0.0) for t0, l in zip(t0s, ls)]
    ps = [e * inv for e, inv in zip(es, invs)]
    vc = v_ref[0, 0, 0:n_cmp, :]
    for x in blocks:
        o = _dot(ps[x].astype(BF16), vc)
        o_ref[0, :, x * tq:(x + 1) * tq, :] = o.reshape(NSA_REP, tq, NSA_DH).astype(o_ref.dtype)

    ovt = ovt_ref[0:n_pad, 0:n_cmp]
    imps = []
    for p in ps:
        psum = p[0:tq]
        for r in range(1, NSA_REP):
            psum = psum + p[r * tq:(r + 1) * tq]
        ph, plo = _split2(psum)
        imps.append(_dot_nt(ovt, ph) + _dot_nt(ovt, plo))

    i_q = lax.broadcasted_iota(jnp.int32, (n_pad, tq), 1)
    blk = lax.broadcasted_iota(jnp.int32, (n_pad, tq), 0)
    blk_f = blk.astype(F32)
    scores = []
    for t0, imp in zip(t0s, imps):
        t_q = t0 + i_q
        cur = _blk(t_q, SEL_BLOCK)
        forced = (blk == 0) | (blk == cur) | (blk == cur - 1)
        score = jnp.where(blk * SEL_BLOCK <= t_q, jnp.where(forced, FORCED_SCORE, imp), -1.0)
        if n_pad > n_sel:
            score = jnp.where(blk < n_sel, score, -2.0)
        scores.append(score)
    taken = -float(2 ** 127)
    for _ in range(min(SEL_TOPK, n_sel)):
        mxs = [jnp.max(score, axis=0, keepdims=True) for score in scores]
        firsts = [jnp.min(jnp.where(score == mx, blk_f, float(LANE)), axis=0, keepdims=True)
                  for score, mx in zip(scores, mxs)]
        scores = [jnp.where(blk_f == first, taken, score)
                  for score, first in zip(scores, firsts)]
    for x in blocks:
        chosen = jnp.where(scores[x] == taken, 0.0, NEG_BIG)
        if n_pad < LANE:
            chosen = jnp.concatenate([chosen, jnp.full((LANE - n_pad, tq), NEG_BIG, F32)], axis=0)
        sel_ref[0, 0, x * tq:(x + 1) * tq, :] = chosen.T.astype(sel_ref.dtype)


def _cmp_attention(q, k_cmp, v_cmp, overlap_t, t_len):
    b_sz = q.shape[0]
    n_cmp = k_cmp.shape[2]
    tq = Q_BLOCK * CMP_Q_BLOCKS
    nq = t_len // tq
    return pl.pallas_call(
        functools.partial(_cmp_kernel, n_sel=t_len // SEL_BLOCK),
        grid=(b_sz, NSA_GROUPS, nq),
        in_specs=[pl.BlockSpec((1, NSA_REP, tq, NSA_DH), lambda b, g, i: (b, g, i, 0)),
                  pl.BlockSpec((1, 1, n_cmp, NSA_DH), lambda b, g, i: (b, g, 0, 0)),
                  pl.BlockSpec((1, 1, n_cmp, NSA_DH), lambda b, g, i: (b, g, 0, 0)),
                  pl.BlockSpec((LANE, n_cmp), lambda b, g, i: (0, 0))],
        out_specs=[pl.BlockSpec((1, NSA_REP, tq, NSA_DH), lambda b, g, i: (b, g, i, 0)),
                   pl.BlockSpec((1, 1, tq, LANE), lambda b, g, i: (b, g, i, 0))],
        out_shape=[jax.ShapeDtypeStruct((b_sz, NSA_HEADS, t_len, NSA_DH), BF16),
                   jax.ShapeDtypeStruct((b_sz, NSA_GROUPS, t_len, LANE), BF16)],
        compiler_params=_cparams("parallel", "parallel", "parallel"),
        name="cmp_attention_topk",
    )(q, k_cmp, v_cmp, overlap_t)


def _online_softmax_step(s, vt, m_ref, acc_ref):
    reps = s.shape[1] // LANE
    m_prev = m_ref[...]
    m_next = jnp.maximum(m_prev, jnp.max(s, axis=-1, keepdims=True))
    p = jnp.exp2(s - jnp.concatenate([m_next] * reps, axis=1))
    alpha = jnp.exp2(m_prev - m_next)
    acc_ref[...] = alpha * acc_ref[...] + _dot(p.astype(BF16), vt)
    m_ref[...] = m_next


def _sel_kernel(q_ref, k_ref, v_ref, sel_ref, o_ref, qa_ref, m_ref, acc_ref, sa_ref, sb_ref):
    qb = pl.program_id(2)
    tq = SEL_Q_TILE
    tk = SEL_KV_TILE
    rows = NSA_REP * tq
    mask = sel_ref[0, 0]
    for r in range(NSA_REP):
        qa_ref[r * tq:(r + 1) * tq, 0:LANE] = mask
        qa_ref[r * tq:(r + 1) * tq, LANE:LANE + NSA_DH] = q_ref[0, r]
        qa_ref[r * tq:(r + 1) * tq, LANE + NSA_DH:2 * LANE] = jnp.zeros((tq, NSA_DH), BF16)
    m_ref[...] = jnp.full(m_ref.shape, -jnp.inf, F32)
    acc_ref[...] = jnp.zeros(acc_ref.shape, F32)
    last = (qb * tq) // tk

    def scores(j):
        k0 = pl.multiple_of(j * tk, tk)
        return _dot_nt(qa_ref[...], k_ref[0, 0, pl.ds(k0, tk), :])

    def values(j):
        return v_ref[0, 0, pl.ds(pl.multiple_of(j * tk, tk), tk), :]

    def causal_step(s):
        row = lax.broadcasted_iota(jnp.int32, (rows, tk), 0)
        kpos = last * tk + lax.broadcasted_iota(jnp.int32, (rows, tk), 1)
        s = jnp.where(kpos <= qb * tq + (row & (tq - 1)), s, NEG_BIG)
        _online_softmax_step(s, values(last), m_ref, acc_ref)

    sa_ref[...] = scores(0)

    def pair(i, carry):
        sb_ref[...] = scores(2 * i + 1)
        _online_softmax_step(sa_ref[...], values(2 * i), m_ref, acc_ref)
        sa_ref[...] = scores(2 * i + 2)
        _online_softmax_step(sb_ref[...], values(2 * i + 1), m_ref, acc_ref)
        return carry

    lax.fori_loop(0, last // 2, pair, 0)

    @pl.when(last % 2 == 0)
    def _():
        causal_step(sa_ref[...])

    @pl.when(last % 2 == 1)
    def _():
        sb_ref[...] = scores(last)
        _online_softmax_step(sa_ref[...], values(last - 1), m_ref, acc_ref)
        causal_step(sb_ref[...])

    acc = acc_ref[...]
    o = acc[:, 0:NSA_DH] / acc[:, NSA_DH:NSA_DH + 1]
    o_ref[0] = o.reshape(NSA_REP, tq, NSA_DH).astype(o_ref.dtype)


def _sel_attention(q, k_sel, v_sel, sel_mask, t_len):
    b_sz = q.shape[0]
    tq = SEL_Q_TILE
    nq = t_len // tq
    rows = NSA_REP * tq
    return pl.pallas_call(
        _sel_kernel,
        grid=(b_sz, NSA_GROUPS, nq),
        in_specs=[pl.BlockSpec((1, NSA_REP, tq, NSA_DH), lambda b, g, i: (b, g, i, 0)),
                  pl.BlockSpec((1, 1, t_len, 2 * LANE), lambda b, g, i: (b, g, 0, 0)),
                  pl.BlockSpec((1, 1, t_len, LANE), lambda b, g, i: (b, g, 0, 0)),
                  pl.BlockSpec((1, 1, tq, LANE), lambda b, g, i: (b, g, i, 0))],
        out_specs=pl.BlockSpec((1, NSA_REP, tq, NSA_DH), lambda b, g, i: (b, g, i, 0)),
        out_shape=jax.ShapeDtypeStruct((b_sz, NSA_HEADS, t_len, NSA_DH), BF16),
        scratch_shapes=[pltpu.VMEM((rows, 2 * LANE), BF16), pltpu.VMEM((rows, LANE), F32),
                        pltpu.VMEM((rows, LANE), F32), pltpu.VMEM((rows, SEL_KV_TILE), F32),
                        pltpu.VMEM((rows, SEL_KV_TILE), F32)],
        compiler_params=_cparams("parallel", "parallel", "arbitrary"),
        name="sel_attention",
    )(q, k_sel, v_sel, sel_mask)


def _win_body(step, q_ref, k_ref, v_ref, o_ref, interior):
    tq = Q_BLOCK
    span = WINDOW + tq
    rows = NSA_REP * tq
    blocks = range(WIN_Q_BLOCKS)
    qbs = [step * WIN_Q_BLOCKS + x for x in blocks]
    starts = [pl.multiple_of(jnp.maximum(qb * tq - WINDOW, 0), tq) for qb in qbs]
    ss = [_dot_nt(q_ref[0, :, x * tq:(x + 1) * tq, :].reshape(rows, NSA_DH),
                  k_ref[0, 0, pl.ds(starts[x], span), :]) for x in blocks]
    if interior:
        i_row = lax.broadcasted_iota(jnp.int32, (rows, tq), 0) & (tq - 1)
        col = lax.broadcasted_iota(jnp.int32, (rows, tq), 1)
        ss = [jnp.concatenate([jnp.where(col > i_row, s[:, 0:tq], NEG_BIG), s[:, tq:WINDOW],
                               jnp.where(col <= i_row, s[:, WINDOW:span], NEG_BIG)], axis=1) for s in ss]
    else:
        i_row = lax.broadcasted_iota(jnp.int32, (rows, span), 0) & (tq - 1)
        col = lax.broadcasted_iota(jnp.int32, (rows, span), 1)
        masked = []
        for x in blocks:
            t_q = qbs[x] * tq + i_row
            kpos = starts[x] + col
            masked.append(jnp.where((kpos <= t_q) & (kpos > t_q - WINDOW), ss[x], NEG_BIG))
        ss = masked
    ms = [jnp.max(s, axis=-1, keepdims=True) for s in ss]
    ps = [jnp.exp2(s - m).astype(BF16) for s, m in zip(ss, ms)]
    accs = [_dot(ps[x], v_ref[0, 0, pl.ds(starts[x], span), :]) for x in blocks]
    for x in blocks:
        o = accs[x][:, 0:NSA_DH] / accs[x][:, NSA_DH:NSA_DH + 1]
        o_ref[0, :, x * tq:(x + 1) * tq, :] = o.reshape(NSA_REP, tq, NSA_DH).astype(o_ref.dtype)


def _win_kernel(q_ref, k_ref, v_ref, o_ref):
    step = pl.program_id(2)
    edge_steps = -(-(WINDOW // Q_BLOCK) // WIN_Q_BLOCKS)
    pl.when(step < edge_steps)(functools.partial(_win_body, step, q_ref, k_ref, v_ref, o_ref, False))
    pl.when(step >= edge_steps)(functools.partial(_win_body, step, q_ref, k_ref, v_ref, o_ref, True))


def _win_attention(q, k_win, v_win, t_len):
    b_sz = q.shape[0]
    tq = Q_BLOCK * WIN_Q_BLOCKS
    nq = t_len // tq
    return pl.pallas_call(
        _win_kernel,
        grid=(b_sz, NSA_GROUPS, nq),
        in_specs=[pl.BlockSpec((1, NSA_REP, tq, NSA_DH), lambda b, g, i: (b, g, i, 0)),
                  pl.BlockSpec((1, 1, t_len, NSA_DH), lambda b, g, i: (b, g, 0, 0)),
                  pl.BlockSpec((1, 1, t_len, LANE), lambda b, g, i: (b, g, 0, 0))],
        out_specs=pl.BlockSpec((1, NSA_REP, tq, NSA_DH), lambda b, g, i: (b, g, i, 0)),
        out_shape=jax.ShapeDtypeStruct((b_sz, NSA_HEADS, t_len, NSA_DH), BF16),
        compiler_params=_cparams("parallel", "parallel", "parallel"),
        name="win_attention",
    )(q, k_win, v_win)


def _merge_kernel(oc_ref, os_ref, ow_ref, zc_ref, zs_ref, zw_ref, gt_ref, ge_ref,
                  w_ref, x_ref, lw_ref, lb_ref, o_ref):
    gates = _sigmoid(gt_ref[...])
    g2 = jnp.concatenate(_split2(gates), axis=1)
    acc = None
    for n, (o_r, z_r) in enumerate(((oc_ref, zc_ref), (os_ref, zs_ref), (ow_ref, zw_ref))):
        gate = _dot(g2, ge_ref[n])
        o = jnp.concatenate([o_r[0, h] for h in range(NSA_HEADS)], axis=-1).astype(F32)
        term = gate * o * _silu(z_r[...])
        acc = term if acc is None else acc + term
    y = _dot(acc.astype(BF16), w_ref[...])
    r = DEEPNORM_ALPHA * x_ref[...] + y
    o_ref[...] = _layer_norm_rows(r, lw_ref[...], lb_ref[...])


def _merge_out(o_cmp, o_sel, o_win, z, tail, gate_expand, w_out, x_res, ln_w, ln_b, t_len, tm):
    m, d = x_res.shape
    width = NSA_HEADS * NSA_DH
    nt = t_len // tm
    ospec = pl.BlockSpec((1, NSA_HEADS, tm, NSA_DH), lambda i: (i // nt, 0, i % nt, 0))
    zspec = lambda col: pl.BlockSpec((tm, width), lambda i: (i, col))
    fixed2 = lambda i: (0, 0)
    return pl.pallas_call(
        _merge_kernel,
        grid=(m // tm,),
        in_specs=[ospec, ospec, ospec, zspec(0), zspec(1), zspec(2),
                  pl.BlockSpec((tm, LANE), lambda i: (i, 0)),
                  pl.BlockSpec((3, 2 * LANE, width), lambda i: (0, 0, 0)),
                  pl.BlockSpec((width, d), fixed2),
                  pl.BlockSpec((tm, d), lambda i: (i, 0)),
                  pl.BlockSpec((1, d), fixed2), pl.BlockSpec((1, d), fixed2)],
        out_specs=pl.BlockSpec((tm, d), lambda i: (i, 0)),
        out_shape=jax.ShapeDtypeStruct((m, d), F32),
        compiler_params=_cparams("parallel"),
        name="merge_out_proj_layernorm",
    )(o_cmp, o_sel, o_win, z, z, z, tail, gate_expand, w_out, x_res,
      ln_w.reshape(1, d), ln_b.reshape(1, d))


def _pad_cols(w, n):
    return jnp.pad(w, ((0, 0), (0, n - w.shape[1])))


def kernel(x, a_w_in, a_conv_w, a_a_log, a_dt_bias, a_norm_w, a_w_out, a_ln_w, a_ln_b,
           s_w_kv, s_pe_k, s_pe_v, s_w1_k, s_w2_k, s_w1_v, s_w2_v,
           b_w_in, b_w_out, b_ln_w, b_ln_b):
    b_sz, t_len, d = x.shape
    m = b_sz * t_len
    assert t_len % SEL_KV_TILE == 0 and t_len >= WINDOW + Q_BLOCK and d % LANE == 0
    assert t_len % (Q_BLOCK * WIN_Q_BLOCKS) == 0 and t_len % (Q_BLOCK * CMP_Q_BLOCKS) == 0
    assert t_len // SEL_BLOCK <= LANE
    assert a_w_in.shape[0] == 1 and b_w_in.shape[0] == 1
    tm_row = min(ROW_TILE, t_len)

    x2 = x.reshape(m, d)
    x16 = x2.astype(BF16)

    dn_width = DN_HEADS * DN_HEAD
    w_in = a_w_in[0]
    qkv = _gdn_qkv(x16, w_in[:, :3 * dn_width].astype(BF16), a_conv_w[0], t_len, tm_row)
    z_gdn, gates_col = _gdn_zgates(x16, _pad_cols(w_in[:, 3 * dn_width:], dn_width + LANE).astype(BF16),
                                   a_a_log[0], a_dt_bias[0], min(PROJ_ROWS, m))
    nc = t_len // GDN_CHUNK
    gates_row = gates_col[:, :2 * DN_HEADS].reshape(b_sz, nc, GDN_CHUNK, 2 * DN_HEADS).transpose(0, 1, 3, 2)
    o_gdn = _gdn_chunks(qkv, z_gdn, gates_col, gates_row, a_norm_w[0], b_sz, t_len)
    h1, h1_16 = _out_proj_ln(o_gdn, a_w_out[0].astype(BF16), x2, a_ln_w[0], a_ln_b[0], tm_row)

    pos = jnp.arange(t_len, dtype=jnp.int32)
    cos_t, sin_t = _rope_tables(pos)
    kv_cmp_raw, k_sel, v_sel, k_win, v_win = _kv_proj(h1_16, s_w_kv.astype(BF16), cos_t, sin_t,
                                                      b_sz, t_len, tm_row)

    n_chunk = t_len // CMP_STRIDE
    halves = CMP_BLOCK // CMP_STRIDE
    pe = jnp.stack([s_pe_k, s_pe_v]).astype(F32).reshape(2, halves, CMP_STRIDE, 1, NSA_DH)
    pe = jnp.broadcast_to(pe, (2, halves, CMP_STRIDE, 2, NSA_DH)).reshape(2, halves, CMP_STRIDE * LANE)
    w1 = jnp.stack([s_w1_k, s_w1_v]).reshape(2, halves, CMP_STRIDE, NSA_DH, CMP_HIDDEN)
    w1 = jnp.einsum('ab,shtdk->shtadbk', jnp.eye(2, dtype=w1.dtype), w1)
    w1 = w1.reshape(2, halves, CMP_STRIDE * LANE, 2 * CMP_HIDDEN).astype(BF16)
    w2 = jnp.stack([_pad_cols(s_w2_k, LANE), _pad_cols(s_w2_v, LANE)]).astype(BF16)
    cmp_end = jnp.arange(n_chunk, dtype=jnp.int32) * CMP_STRIDE + CMP_BLOCK - 1
    cos_c, sin_c = _rope_tables(cmp_end)
    kv_cmp = _compress(kv_cmp_raw, pe, w1, w2, cos_c, sin_c, b_sz, t_len)
    k_cmp, v_cmp = kv_cmp[0], kv_cmp[1]

    nsa_width = NSA_HEADS * NSA_DH
    w_in_b = b_w_in[0]
    q, z_b, tail_b = _nsa_proj(h1_16, _pad_cols(w_in_b, 4 * nsa_width + LANE).astype(BF16), cos_t, sin_t,
                               b_sz, t_len, tm_row)

    n_sel = t_len // SEL_BLOCK
    c_start = jnp.arange(n_chunk, dtype=jnp.int32)[:, None] * CMP_STRIDE
    s_start = jnp.arange(n_sel, dtype=jnp.int32)[None, :] * SEL_BLOCK
    ov = jnp.clip(jnp.minimum(c_start + CMP_BLOCK, s_start + SEL_BLOCK) - jnp.maximum(c_start, s_start), 0, None)
    overlap = (ov.astype(F32) / CMP_BLOCK).astype(BF16)
    overlap_t = jnp.pad(overlap.T, ((0, LANE - n_sel), (0, 0)))

    o_cmp, sel_mask = _cmp_attention(q, k_cmp, v_cmp, overlap_t, t_len)
    o_sel = _sel_attention(q, k_sel, v_sel, sel_mask, t_len)
    o_win = _win_attention(q, k_win, v_win, t_len)

    lane_id = (jnp.arange(2 * LANE, dtype=jnp.int32) % LANE)[None, :, None]
    col_head = (jnp.arange(NSA_HEADS * NSA_DH, dtype=jnp.int32) // NSA_DH)[None, None, :]
    branch = jnp.arange(3, dtype=jnp.int32)[:, None, None]
    gate_expand = (lane_id == branch * NSA_HEADS + col_head).astype(BF16)
    out = _merge_out(o_cmp, o_sel, o_win, z_b, tail_b, gate_expand, b_w_out[0].astype(BF16),
                     h1, b_ln_w[0], b_ln_b[0], t_len, tm_row)
    return out.reshape(b_sz, t_len, d)
```

```python
import functools

import jax
import jax.numpy as jnp
from jax import lax
from jax.experimental import pallas as pl
from jax.experimental.pallas import tpu as pltpu

F32 = jnp.float32
BF16 = jnp.bfloat16

DN_HEADS = 8
DN_HEAD = 128
DN_CONV = 4
NSA_HEADS = 16
NSA_GROUPS = 4
NSA_REP = NSA_HEADS // NSA_GROUPS
NSA_DH = 64
CMP_BLOCK = 32
CMP_STRIDE = 16
CMP_HIDDEN = 128
SEL_BLOCK = 64
SEL_TOPK = 16
WINDOW = 512
Q_BLOCK = 128
FORCED_SCORE = 1.0e4
ROPE_THETA = 10000.0
NORM_EPS = 1e-6
DEPTH = 2
DEEPNORM_ALPHA = (2.0 * DEPTH) ** 0.25

GDN_CHUNK = 128
INV_BASE = 16
GDN_SEQS = 2
LANE = 128
SUBLANE = 8
PROJ_ROWS = 1024
ROW_TILE = 512
SEL_KV_TILE = 512
SEL_Q_TILE = 512
CMP_Q_BLOCKS = 8
WIN_Q_BLOCKS = 8
NEG_BIG = -1e30
LOG2_E = 1.4426950408889634
VMEM_LIMIT = 56 * 1024 * 1024


def _cparams(*sem):
    return pltpu.CompilerParams(dimension_semantics=sem, vmem_limit_bytes=VMEM_LIMIT)


def _dot(a, b):
    return jnp.dot(a, b, preferred_element_type=F32)


def _dot_nt(a, b):
    return lax.dot_general(a, b, (((1,), (1,)), ((), ())), preferred_element_type=F32)


def _dot_tn(a, b):
    return lax.dot_general(a, b, (((0,), (0,)), ((), ())), preferred_element_type=F32)


def _split2(a):
    hi = a.astype(BF16)
    lo = (a - hi.astype(F32)).astype(BF16)
    return hi, lo


def _split3(a):
    hi = a.astype(BF16)
    r = a - hi.astype(F32)
    mid = r.astype(BF16)
    lo = (r - mid.astype(F32)).astype(BF16)
    return hi, mid, lo


def _mm(a, b):
    return _dot(a.astype(BF16), b.astype(BF16))


def _blk(idx, size):
    return idx >> (size.bit_length() - 1)


def _silu(x):
    return x / (1.0 + jnp.exp(-x))


def _sigmoid(x):
    return 1.0 / (1.0 + jnp.exp(-x))


def _layer_norm_rows(r, w, b):
    mu = jnp.mean(r, axis=-1, keepdims=True)
    d = r - mu
    var = jnp.mean(d * d, axis=-1, keepdims=True)
    return d * lax.rsqrt(var + NORM_EPS) * w + b


def _out_ln_kernel(a_ref, w_ref, x_ref, lw_ref, lb_ref, o_ref, ob_ref):
    y = _dot(a_ref[...], w_ref[...])
    r = DEEPNORM_ALPHA * x_ref[...] + y
    out = _layer_norm_rows(r, lw_ref[...], lb_ref[...])
    o_ref[...] = out
    ob_ref[...] = out.astype(BF16)


def _out_proj_ln(a, w, x_res, ln_w, ln_b, tm):
    m, k = a.shape
    n = w.shape[1]
    row = lambda i: (i, 0)
    fixed = lambda i: (0, 0)
    return pl.pallas_call(
        _out_ln_kernel,
        grid=(m // tm,),
        in_specs=[pl.BlockSpec((tm, k), row), pl.BlockSpec((k, n), fixed),
                  pl.BlockSpec((tm, n), row), pl.BlockSpec((1, n), fixed),
                  pl.BlockSpec((1, n), fixed)],
        out_specs=[pl.BlockSpec((tm, n), row), pl.BlockSpec((tm, n), row)],
        out_shape=[jax.ShapeDtypeStruct((m, n), F32), jax.ShapeDtypeStruct((m, n), BF16)],
        compiler_params=_cparams("parallel"),
        name="out_proj_layernorm",
    )(a, w, x_res, ln_w.reshape(1, n), ln_b.reshape(1, n))


def _gdn_qkv_kernel(x_ref, w_ref, cw_ref, o_ref, halo_ref, *, tm, t_len):
    j = pl.program_id(0)
    i = pl.program_id(1)
    @pl.when((i * tm) % t_len == 0)
    def _():
        halo_ref[...] = jnp.zeros(halo_ref.shape, F32)

    p = _dot(x_ref[...], w_ref[...])
    halo = halo_ref[...]
    halo_ref[...] = p[tm - SUBLANE:tm, :]
    cw = cw_ref[...]
    row = lax.broadcasted_iota(jnp.int32, halo.shape, 0)
    y = p * cw[DN_CONV - 1:DN_CONV, :]
    for back in range(1, DN_CONV):
        shifted = pltpu.roll(p, back, axis=0)
        head = jnp.where(row < back, pltpu.roll(halo, back, axis=0), shifted[0:SUBLANE, :])
        shifted = jnp.concatenate([head, shifted[SUBLANE:, :]], axis=0)
        y = y + shifted * cw[DN_CONV - 1 - back:DN_CONV - back, :]
    y = _silu(y)
    qscale = jnp.where(j == 0, DN_HEAD ** -0.5, 1.0)
    for h in range(DN_HEADS):
        sl = slice(h * DN_HEAD, (h + 1) * DN_HEAD)
        yh = y[:, sl]
        ss = jnp.sum(yh * yh, axis=-1, keepdims=True)
        scale = jnp.where(j < 2, lax.rsqrt(ss + NORM_EPS) * qscale, 1.0)
        o_ref[:, sl] = yh * scale


def _gdn_qkv(x16, w_qkv, conv_w, t_len, tm):
    m, d = x16.shape
    width = DN_HEADS * DN_HEAD
    kern = functools.partial(_gdn_qkv_kernel, tm=tm, t_len=t_len)
    return pl.pallas_call(
        kern,
        grid=(3, m // tm),
        in_specs=[pl.BlockSpec((tm, d), lambda j, i: (i, 0)),
                  pl.BlockSpec((d, width), lambda j, i: (0, j)),
                  pl.BlockSpec((DN_CONV, width), lambda j, i: (0, j))],
        out_specs=pl.BlockSpec((tm, width), lambda j, i: (i, j)),
        out_shape=jax.ShapeDtypeStruct((m, 3 * width), F32),
        scratch_shapes=[pltpu.VMEM((SUBLANE, width), F32)],
        compiler_params=_cparams("parallel", "arbitrary"),
        name="gdn_qkv_proj_conv",
    )(x16, w_qkv, conv_w)


def _gdn_zgate_kernel(x_ref, w_ref, al_ref, dt_ref, z_ref, g_ref):
    width = DN_HEADS * DN_HEAD
    proj = _dot(x_ref[...], w_ref[...])
    z_ref[...] = proj[:, 0:width]
    x = proj[:, width:]
    lane = lax.broadcasted_iota(jnp.int32, x.shape, 1)
    beta = _sigmoid(x)
    z = x + dt_ref[...]
    softplus = jnp.maximum(z, 0.0) + jnp.log(1.0 + jnp.exp(-jnp.abs(z)))
    g = -jnp.exp(al_ref[...]) * softplus
    g_ref[...] = jnp.where(lane < DN_HEADS, beta, jnp.where(lane < 2 * DN_HEADS, g, 0.0))


def _gdn_zgates(x16, w_z_logits, a_log, dt_bias, tm):
    m, d = x16.shape
    width = DN_HEADS * DN_HEAD
    pad = lambda v: jnp.zeros((1, LANE), F32).at[0, DN_HEADS:2 * DN_HEADS].set(v.astype(F32))
    return pl.pallas_call(
        _gdn_zgate_kernel,
        grid=(m // tm,),
        in_specs=[pl.BlockSpec((tm, d), lambda i: (i, 0)),
                  pl.BlockSpec((d, width + LANE), lambda i: (0, 0)),
                  pl.BlockSpec((1, LANE), lambda i: (0, 0)),
                  pl.BlockSpec((1, LANE), lambda i: (0, 0))],
        out_specs=[pl.BlockSpec((tm, width), lambda i: (i, 0)),
                   pl.BlockSpec((tm, LANE), lambda i: (i, 0))],
        out_shape=[jax.ShapeDtypeStruct((m, width), F32), jax.ShapeDtypeStruct((m, LANE), F32)],
        compiler_params=_cparams("parallel"),
        name="gdn_z_gates",
    )(x16, w_z_logits, pad(a_log), pad(dt_bias))


def _inv_unit_lower(lows, ii, jj):
    c = lows[0].shape[0]
    eye = (ii == jj).astype(F32)
    same = _blk(ii, INV_BASE) == _blk(jj, INV_BASE)
    ds = [jnp.where(same, low, 0.0) for low in lows]
    xs = [eye - d for d in ds]
    ps = ds
    span = 2
    while span < INV_BASE:
        ps = [_mm(p, p) for p in ps]
        xs = [_mm(x, eye + p) for x, p in zip(xs, ps)]
        span *= 2
    s = INV_BASE
    while s < c:
        off = (_blk(ii, 2 * s) == _blk(jj, 2 * s)) & (_blk(ii, s) != _blk(jj, s))
        ts = [_mm(x, jnp.where(off, low, 0.0)) for x, low in zip(xs, lows)]
        xs = [x - _mm(t, x) for x, t in zip(xs, ts)]
        s *= 2
    return xs


def _gdn_chunk_kernel(q_ref, k_ref, v_ref, z_ref, gc_ref, gr_ref, nw_ref, o_ref, s_ref):
    c = GDN_CHUNK

    @pl.when(pl.program_id(1) == 0)
    def _():
        s_ref[...] = jnp.zeros_like(s_ref)

    ii = lax.broadcasted_iota(jnp.int32, (c, c), 0)
    jj = lax.broadcasted_iota(jnp.int32, (c, c), 1)
    incl = ii >= jj
    strict = ii > jj
    ltri = incl.astype(BF16)
    utri = (ii <= jj).astype(BF16)
    nb = q_ref.shape[0]
    gcols, gcs_cols, gcs_rows = [], [], []
    for bb in range(nb):
        gcol = gc_ref[bb]
        gc3 = _split3(gcol)
        gr3 = _split3(gr_ref[bb, 0])
        gcols.append(gcol)
        gcs_cols.append(_dot(ltri, gc3[0]) + (_dot(ltri, gc3[1]) + _dot(ltri, gc3[2])))
        gcs_rows.append(_dot(gr3[0], utri) + (_dot(gr3[1], utri) + _dot(gr3[2], utri)))
    nw = nw_ref[...]

    chains = [(bb, h) for bb in range(nb) for h in range(DN_HEADS)]
    tile = lambda ref, bb, h: ref[bb, :, h * DN_HEAD:(h + 1) * DN_HEAD]
    betas = [gcols[bb][:, h:h + 1] for bb, h in chains]
    gccs = [gcs_cols[bb][:, DN_HEADS + h:DN_HEADS + h + 1] for bb, h in chains]
    gcrs = [gcs_rows[bb][DN_HEADS + h:DN_HEADS + h + 1, :] for bb, h in chains]
    decays = [jnp.where(incl, jnp.exp(jnp.where(incl, gcc - gcr, 0.0)), 0.0)
              for gcc, gcr in zip(gccs, gcrs)]
    kbs = [tile(k_ref, bb, h) * beta for (bb, h), beta in zip(chains, betas)]
    k16s = [tile(k_ref, bb, h).astype(BF16) for bb, h in chains]
    lows = [jnp.where(strict, _dot_nt(kb.astype(BF16), k16) * decay, 0.0)
            for kb, k16, decay in zip(kbs, k16s, decays)]
    attns = [jnp.where(incl, _dot_nt(tile(q_ref, bb, h).astype(BF16), k16) * decay, 0.0).astype(BF16)
             for (bb, h), k16, decay in zip(chains, k16s, decays)]
    tinvs = _inv_unit_lower(lows, ii, jj)
    egs = [jnp.exp(gcc) for gcc in gccs]
    sols = [_mm(tinv, jnp.concatenate([tile(v_ref, bb, h) * beta, kb * eg], axis=1))
            for tinv, (bb, h), beta, kb, eg in zip(tinvs, chains, betas, kbs, egs)]
    g_lasts = [gcc[c - 1:c, :] for gcc in gccs]
    kds = [(tile(k_ref, bb, h) * jnp.exp(g_last - gcc)).astype(BF16)
           for (bb, h), g_last, gcc in zip(chains, g_lasts, gccs)]
    qgs = [(tile(q_ref, bb, h) * eg).astype(BF16) for (bb, h), eg in zip(chains, egs)]
    states = [s_ref[bb * DN_HEADS + h] for bb, h in chains]
    s16s = [state.astype(BF16) for state in states]
    vn16s = [(sol[:, :DN_HEAD] - _dot(sol[:, DN_HEAD:].astype(BF16), s16)).astype(BF16)
             for sol, s16 in zip(sols, s16s)]
    outs = [_dot(qg, s16) + _dot(attn, vn16)
            for qg, s16, attn, vn16 in zip(qgs, s16s, attns, vn16s)]
    for n, (bb, h) in enumerate(chains):
        s_ref[bb * DN_HEADS + h] = states[n] * jnp.exp(g_lasts[n]) + _dot_tn(kds[n], vn16s[n])
    for n, (bb, h) in enumerate(chains):
        o = outs[n]
        on = o * lax.rsqrt(jnp.mean(o * o, axis=-1, keepdims=True) + NORM_EPS) * nw
        o_ref[bb, :, h * DN_HEAD:(h + 1) * DN_HEAD] = (on * _silu(tile(z_ref, bb, h))).astype(o_ref.dtype)


def _gdn_chunks(qkv, z, gates_col, gates_row, norm_w, b_sz, t_len):
    c = GDN_CHUNK
    nc = t_len // c
    width = DN_HEADS * DN_HEAD
    nb = GDN_SEQS if b_sz % GDN_SEQS == 0 else 1
    seq = lambda a: a.reshape(b_sz, t_len, a.shape[-1])
    tok = lambda col: (lambda b, i: (b, i, col))
    out = pl.pallas_call(
        _gdn_chunk_kernel,
        grid=(b_sz // nb, nc),
        in_specs=[pl.BlockSpec((nb, c, width), tok(0)), pl.BlockSpec((nb, c, width), tok(1)),
                  pl.BlockSpec((nb, c, width), tok(2)), pl.BlockSpec((nb, c, width), tok(0)),
                  pl.BlockSpec((nb, c, LANE), tok(0)),
                  pl.BlockSpec((nb, 1, 2 * DN_HEADS, c), lambda b, i: (b, i, 0, 0)),
                  pl.BlockSpec((1, DN_HEAD), lambda b, i: (0, 0))],
        out_specs=pl.BlockSpec((nb, c, width), tok(0)),
        out_shape=jax.ShapeDtypeStruct((b_sz, t_len, width), BF16),
        scratch_shapes=[pltpu.VMEM((nb * DN_HEADS, DN_HEAD, DN_HEAD), F32)],
        compiler_params=_cparams("parallel", "arbitrary"),
        name="gdn_delta_rule",
    )(seq(qkv), seq(qkv), seq(qkv), seq(z), seq(gates_col), gates_row,
      norm_w.reshape(1, DN_HEAD).astype(F32))
    return out.reshape(b_sz * t_len, width)


def _rope_tables(pos):
    half = NSA_DH // 2
    inv_freq = ROPE_THETA ** (-jnp.arange(half, dtype=F32) / half)
    ang = pos.astype(F32)[:, None] * inv_freq[None, :]
    cos, sin = jnp.cos(ang), jnp.sin(ang)
    cos_t = jnp.concatenate([cos, cos, cos, cos], axis=-1)
    sin_t = jnp.concatenate([-sin, sin, -sin, sin], axis=-1)
    return cos_t, sin_t


def _rope128(x, cos_t, sin_t):
    lane = lax.broadcasted_iota(jnp.int32, x.shape, 1)
    lower = (lane & (NSA_DH - 1)) < (NSA_DH // 2)
    fwd = pltpu.roll(x, LANE - NSA_DH // 2, axis=1)
    bwd = pltpu.roll(x, NSA_DH // 2, axis=1)
    return x * cos_t + jnp.where(lower, fwd, bwd) * sin_t


def _kv_proj_kernel(x_ref, w_ref, cos_ref, sin_ref, cmp_ref, ks_ref, vs_ref, kw_ref, vw_ref, *, tm, nt):
    gw = NSA_GROUPS * NSA_DH
    kv = _dot(x_ref[...], w_ref[...])
    cmp_ref[...] = kv[:, 0:2 * gw]
    cos_t = cos_ref[...]
    sin_t = sin_ref[...]
    t_pos = (pl.program_id(0) % nt) * tm + lax.broadcasted_iota(jnp.int32, (tm, LANE), 0)
    lane = lax.broadcasted_iota(jnp.int32, (tm, LANE), 1)
    onehot = (_blk(t_pos, SEL_BLOCK) == lane).astype(BF16)
    ones_col = (lane == NSA_DH).astype(BF16)
    zeros_half = jnp.zeros((tm, NSA_DH), BF16)
    for pair in range(NSA_GROUPS // 2):
        cols = lambda src: kv[:, src * gw + pair * LANE: src * gw + (pair + 1) * LANE]
        k_sel = _rope128(cols(2), cos_t, sin_t).astype(BF16)
        k_win = _rope128(cols(4), cos_t, sin_t).astype(BF16)
        v_sel = cols(3).astype(BF16)
        v_win = cols(5).astype(BF16)
        for half in range(2):
            g = 2 * pair + half
            sl = slice(half * NSA_DH, (half + 1) * NSA_DH)
            ks_ref[0, g, :, 0:LANE] = onehot
            ks_ref[0, g, :, LANE:LANE + NSA_DH] = k_sel[:, sl]
            ks_ref[0, g, :, LANE + NSA_DH:2 * LANE] = zeros_half
            kw_ref[0, g] = k_win[:, sl]
            for val, dst in ((v_sel, vs_ref), (v_win, vw_ref)):
                dst[0, g] = ones_col
                dst[0, g, :, 0:NSA_DH] = val[:, sl]


def _nsa_proj_kernel(x_ref, w_ref, cos_ref, sin_ref, q_ref, z_ref, t_ref):
    width = NSA_HEADS * NSA_DH
    proj = _dot(x_ref[...], w_ref[...])
    z_ref[...] = proj[:, width:4 * width]
    t_ref[...] = proj[:, 4 * width:]
    cos_t = cos_ref[...]
    sin_t = sin_ref[...]
    scale = NSA_DH ** -0.5 * LOG2_E
    for pair in range(NSA_HEADS // 2):
        x = _rope128(proj[:, pair * LANE:(pair + 1) * LANE], cos_t, sin_t) * scale
        q_ref[0, 2 * pair] = x[:, :NSA_DH].astype(BF16)
        q_ref[0, 2 * pair + 1] = x[:, NSA_DH:].astype(BF16)


def _layer1_proj_kernel(x_ref, wkv_ref, wq_ref, cos_ref, sin_ref, cmp_ref, ks_ref, vs_ref, kw_ref, vw_ref,
                        q_ref, z_ref, t_ref, *, tm, nt):
    _kv_proj_kernel(x_ref, wkv_ref, cos_ref, sin_ref, cmp_ref, ks_ref, vs_ref, kw_ref, vw_ref, tm=tm, nt=nt)
    _nsa_proj_kernel(x_ref, wq_ref, cos_ref, sin_ref, q_ref, z_ref, t_ref)


def _layer1_proj(h16, w_kv, w_all, cos_t, sin_t, b_sz, t_len, tm):
    nt = t_len // tm
    m, d = h16.shape
    gw = NSA_GROUPS * NSA_DH
    width = NSA_HEADS * NSA_DH
    kv_shape = lambda w: jax.ShapeDtypeStruct((b_sz, NSA_GROUPS, t_len, w), BF16)
    kv_spec = lambda w: pl.BlockSpec((1, NSA_GROUPS, tm, w), lambda i: (i // nt, 0, i % nt, 0))
    rows = lambda w: pl.BlockSpec((tm, w), lambda i: (i, 0))
    table = pl.BlockSpec((tm, LANE), lambda i: (i % nt, 0))
    return pl.pallas_call(
        functools.partial(_layer1_proj_kernel, tm=tm, nt=nt),
        grid=(b_sz * nt,),
        in_specs=[rows(d),
                  pl.BlockSpec((d, 6 * gw), lambda i: (0, 0)),
                  pl.BlockSpec((d, 4 * width + LANE), lambda i: (0, 0)),
                  table, table],
        out_specs=[rows(2 * gw), kv_spec(2 * LANE), kv_spec(LANE), kv_spec(NSA_DH), kv_spec(LANE),
                   pl.BlockSpec((1, NSA_HEADS, tm, NSA_DH), lambda i: (i // nt, 0, i % nt, 0)),
                   rows(3 * width), rows(LANE)],
        out_shape=[jax.ShapeDtypeStruct((m, 2 * gw), F32),
                   kv_shape(2 * LANE), kv_shape(LANE), kv_shape(NSA_DH), kv_shape(LANE),
                   jax.ShapeDtypeStruct((b_sz, NSA_HEADS, t_len, NSA_DH), BF16),
                   jax.ShapeDtypeStruct((m, 3 * width), F32),
                   jax.ShapeDtypeStruct((m, LANE), F32)],
        compiler_params=_cparams("parallel"),
        name="layer1_proj_rope_layout",
    )(h16, w_kv, w_all, cos_t, sin_t)


def _compress_kernel(x_ref, pe_ref, w1_ref, w2_ref, cos_ref, sin_ref, o_ref, *, n_chunk):
    is_key = pl.program_id(0) == 0
    chunk = jnp.concatenate([x_ref[pl.ds(tau, n_chunk, stride=CMP_STRIDE), :]
                             for tau in range(CMP_STRIDE)], axis=1)
    a = _dot((chunk + pe_ref[0, 0:1, :]).astype(BF16), w1_ref[0, 0])
    b = _dot((chunk + pe_ref[0, 1:2, :]).astype(BF16), w1_ref[0, 1])
    hid = a + pltpu.roll(b, n_chunk - 1, axis=0)
    act = _silu(hid).astype(BF16)
    for half in range(2):
        y = _dot(act[:, half * CMP_HIDDEN:(half + 1) * CMP_HIDDEN], w2_ref[0])
        y = jnp.where(is_key, _rope128(y, cos_ref[...], sin_ref[...]), y)
        o_ref[0, 0, half] = y[:, :NSA_DH].astype(BF16)


def _compress(kv_raw, pe, w1, w2, cos_c, sin_c, b_sz, t_len):
    n_chunk = t_len // CMP_STRIDE
    pairs = NSA_GROUPS // 2
    flat = CMP_STRIDE * LANE
    return pl.pallas_call(
        functools.partial(_compress_kernel, n_chunk=n_chunk),
        grid=(2, b_sz, pairs),
        in_specs=[pl.BlockSpec((t_len, LANE), lambda s, b, p: (b, s * pairs + p)),
                  pl.BlockSpec((1, 2, flat), lambda s, b, p: (s, 0, 0)),
                  pl.BlockSpec((1, 2, flat, 2 * CMP_HIDDEN), lambda s, b, p: (s, 0, 0, 0)),
                  pl.BlockSpec((1, CMP_HIDDEN, LANE), lambda s, b, p: (s, 0, 0)),
                  pl.BlockSpec((n_chunk, LANE), lambda s, b, p: (0, 0)),
                  pl.BlockSpec((n_chunk, LANE), lambda s, b, p: (0, 0))],
        out_specs=pl.BlockSpec((1, 1, 2, n_chunk, NSA_DH), lambda s, b, p: (s, b, p, 0, 0)),
        out_shape=jax.ShapeDtypeStruct((2, b_sz, NSA_GROUPS, n_chunk, NSA_DH), BF16),
        compiler_params=_cparams("parallel", "parallel", "parallel"),
        name="compress_blocks",
    )(kv_raw, pe, w1, w2, cos_c, sin_c)


def _cmp_kernel(q_ref, k_ref, v_ref, ovt_ref, o_ref, sel_ref, *, n_sel):
    step = pl.program_id(2)
    n_cmp = k_ref.shape[2]
    steps_per_group = LANE * CMP_STRIDE // (Q_BLOCK * CMP_Q_BLOCKS)
    n_var = max(1, n_cmp // LANE)
    for var in range(n_var):
        n_col = min(n_cmp, LANE * (var + 1))
        n_blk = min(n_sel, n_col * CMP_STRIDE // SEL_BLOCK)
        pl.when(step // steps_per_group == var)(functools.partial(
            _cmp_body, step, q_ref, k_ref, v_ref, ovt_ref, o_ref, sel_ref, n_col, n_blk))


def _cmp_body(step, q_ref, k_ref, v_ref, ovt_ref, o_ref, sel_ref, n_cmp, n_sel):
    tq = Q_BLOCK
    rows = NSA_REP * tq
    n_pad = -(-n_sel // SUBLANE) * SUBLANE
    blocks = range(CMP_Q_BLOCKS)
    t0s = [(step * CMP_Q_BLOCKS + x) * tq for x in blocks]
    kc = k_ref[0, 0, 0:n_cmp, :]
    ss = [_dot_nt(q_ref[0, :, x * tq:(x + 1) * tq, :].reshape(rows, NSA_DH), kc) for x in blocks]
    lag = (lax.broadcasted_iota(jnp.int32, (tq, n_cmp), 1) * CMP_STRIDE + (CMP_BLOCK - 1)
           - lax.broadcasted_iota(jnp.int32, (tq, n_cmp), 0))
    ss = [s + jnp.concatenate([jnp.where(lag <= t0, 0.0, NEG_BIG)] * NSA_REP, axis=0)
          for s, t0 in zip(ss, t0s)]
    ms = [jnp.max(s, axis=-1, keepdims=True) for s in ss]
    es = [jnp.exp2(s - m) for s, m in zip(ss, ms)]
    ls = [jnp.sum(e, axis=-1, keepdims=True) for e in es]
    i_col = lax.broadcasted_iota(jnp.int32, (rows, 1), 0) & (tq - 1)
    invs = [jnp.where(t0 + i_col >= CMP_BLOCK - 1, 1.0 / l, 0.0) for t0, l in zip(t0s, ls)]
    ps = [e * inv for e, inv in zip(es, invs)]
    vc = v_ref[0, 0, 0:n_cmp, :]
    for x in blocks:
        o = _dot(ps[x].astype(BF16), vc)
        o_ref[0, :, x * tq:(x + 1) * tq, :] = o.reshape(NSA_REP, tq, NSA_DH).astype(o_ref.dtype)

    ovt = ovt_ref[0:n_pad, 0:n_cmp]
    imps = []
    for p in ps:
        psum = p[0:tq]
        for r in range(1, NSA_REP):
            psum = psum + p[r * tq:(r + 1) * tq]
        ph, plo = _split2(psum)
        imps.append(_dot_nt(ovt, ph) + _dot_nt(ovt, plo))

    i_q = lax.broadcasted_iota(jnp.int32, (n_pad, tq), 1)
    blk = lax.broadcasted_iota(jnp.int32, (n_pad, tq), 0)
    blk_f = blk.astype(F32)
    scores = []
    for t0, imp in zip(t0s, imps):
        t_q = t0 + i_q
        cur = _blk(t_q, SEL_BLOCK)
        forced = (blk == 0) | (blk == cur) | (blk == cur - 1)
        score = jnp.where(blk * SEL_BLOCK <= t_q, jnp.where(forced, FORCED_SCORE, imp), -1.0)
        if n_pad > n_sel:
            score = jnp.where(blk < n_sel, score, -2.0)
        scores.append(score)
    taken = -float(2 ** 127)
    for _ in range(min(SEL_TOPK, n_sel)):
        mxs = [jnp.max(score, axis=0, keepdims=True) for score in scores]
        firsts = [jnp.min(jnp.where(score == mx, blk_f, float(LANE)), axis=0, keepdims=True)
                  for score, mx in zip(scores, mxs)]
        scores = [jnp.where(blk_f == first, taken, score)
                  for score, first in zip(scores, firsts)]
    for x in blocks:
        chosen = jnp.where(scores[x] == taken, 0.0, NEG_BIG)
        if n_pad < LANE:
            chosen = jnp.concatenate([chosen, jnp.full((LANE - n_pad, tq), NEG_BIG, F32)], axis=0)
        sel_ref[0, 0, x * tq:(x + 1) * tq, :] = chosen.T.astype(sel_ref.dtype)


def _cmp_attention(q, k_cmp, v_cmp, overlap_t, t_len):
    b_sz = q.shape[0]
    n_cmp = k_cmp.shape[2]
    tq = Q_BLOCK * CMP_Q_BLOCKS
    nq = t_len // tq
    return pl.pallas_call(
        functools.partial(_cmp_kernel, n_sel=t_len // SEL_BLOCK),
        grid=(b_sz, NSA_GROUPS, nq),
        in_specs=[pl.BlockSpec((1, NSA_REP, tq, NSA_DH), lambda b, g, i: (b, g, i, 0)),
                  pl.BlockSpec((1, 1, n_cmp, NSA_DH), lambda b, g, i: (b, g, 0, 0)),
                  pl.BlockSpec((1, 1, n_cmp, NSA_DH), lambda b, g, i: (b, g, 0, 0)),
                  pl.BlockSpec((LANE, n_cmp), lambda b, g, i: (0, 0))],
        out_specs=[pl.BlockSpec((1, NSA_REP, tq, NSA_DH), lambda b, g, i: (b, g, i, 0)),
                   pl.BlockSpec((1, 1, tq, LANE), lambda b, g, i: (b, g, i, 0))],
        out_shape=[jax.ShapeDtypeStruct((b_sz, NSA_HEADS, t_len, NSA_DH), BF16),
                   jax.ShapeDtypeStruct((b_sz, NSA_GROUPS, t_len, LANE), BF16)],
        compiler_params=_cparams("parallel", "parallel", "parallel"),
        name="cmp_attention_topk",
    )(q, k_cmp, v_cmp, overlap_t)


def _online_softmax_step(s, vt, m_ref, acc_ref):
    reps = s.shape[1] // LANE
    m_prev = m_ref[...]
    m_next = jnp.maximum(m_prev, jnp.max(s, axis=-1, keepdims=True))
    p = jnp.exp2(s - jnp.concatenate([m_next] * reps, axis=1))
    alpha = jnp.exp2(m_prev - m_next)
    acc_ref[...] = alpha * acc_ref[...] + _dot(p.astype(BF16), vt)
    m_ref[...] = m_next


def _sel_kernel(q_ref, k_ref, v_ref, sel_ref, o_ref, qa_ref, m_ref, acc_ref, sa_ref, sb_ref):
    qb = pl.program_id(2)
    tq = SEL_Q_TILE
    tk = SEL_KV_TILE
    rows = NSA_REP * tq
    mask = sel_ref[0, 0]
    for r in range(NSA_REP):
        qa_ref[r * tq:(r + 1) * tq, 0:LANE] = mask
        qa_ref[r * tq:(r + 1) * tq, LANE:LANE + NSA_DH] = q_ref[0, r]
        qa_ref[r * tq:(r + 1) * tq, LANE + NSA_DH:2 * LANE] = jnp.zeros((tq, NSA_DH), BF16)
    m_ref[...] = jnp.full(m_ref.shape, -jnp.inf, F32)
    acc_ref[...] = jnp.zeros(acc_ref.shape, F32)
    last = (qb * tq) // tk

    def scores(j):
        k0 = pl.multiple_of(j * tk, tk)
        return _dot_nt(qa_ref[...], k_ref[0, 0, pl.ds(k0, tk), :])

    def values(j):
        return v_ref[0, 0, pl.ds(pl.multiple_of(j * tk, tk), tk), :]

    def causal_step(s):
        row = lax.broadcasted_iota(jnp.int32, (rows, tk), 0)
        kpos = last * tk + lax.broadcasted_iota(jnp.int32, (rows, tk), 1)
        s = jnp.where(kpos <= qb * tq + (row & (tq - 1)), s, NEG_BIG)
        _online_softmax_step(s, values(last), m_ref, acc_ref)

    sa_ref[...] = scores(0)

    def pair(i, carry):
        sb_ref[...] = scores(2 * i + 1)
        _online_softmax_step(sa_ref[...], values(2 * i), m_ref, acc_ref)
        sa_ref[...] = scores(2 * i + 2)
        _online_softmax_step(sb_ref[...], values(2 * i + 1), m_ref, acc_ref)
        return carry

    lax.fori_loop(0, last // 2, pair, 0)

    @pl.when(last % 2 == 0)
    def _():
        causal_step(sa_ref[...])

    @pl.when(last % 2 == 1)
    def _():
        sb_ref[...] = scores(last)
        _online_softmax_step(sa_ref[...], values(last - 1), m_ref, acc_ref)
        causal_step(sb_ref[...])

    acc = acc_ref[...]
    o = acc[:, 0:NSA_DH] / acc[:, NSA_DH:NSA_DH + 1]
    o_ref[0] = o.reshape(NSA_REP, tq, NSA_DH).astype(o_ref.dtype)


def _sel_attention(q, k_sel, v_sel, sel_mask, t_len):
    b_sz = q.shape[0]
    tq = SEL_Q_TILE
    nq = t_len // tq
    rows = NSA_REP * tq
    return pl.pallas_call(
        _sel_kernel,
        grid=(b_sz, NSA_GROUPS, nq),
        in_specs=[pl.BlockSpec((1, NSA_REP, tq, NSA_DH), lambda b, g, i: (b, g, i, 0)),
                  pl.BlockSpec((1, 1, t_len, 2 * LANE), lambda b, g, i: (b, g, 0, 0)),
                  pl.BlockSpec((1, 1, t_len, LANE), lambda b, g, i: (b, g, 0, 0)),
                  pl.BlockSpec((1, 1, tq, LANE), lambda b, g, i: (b, g, i, 0))],
        out_specs=pl.BlockSpec((1, NSA_REP, tq, NSA_DH), lambda b, g, i: (b, g, i, 0)),
        out_shape=jax.ShapeDtypeStruct((b_sz, NSA_HEADS, t_len, NSA_DH), BF16),
        scratch_shapes=[pltpu.VMEM((rows, 2 * LANE), BF16), pltpu.VMEM((rows, LANE), F32),
                        pltpu.VMEM((rows, LANE), F32), pltpu.VMEM((rows, SEL_KV_TILE), F32),
                        pltpu.VMEM((rows, SEL_KV_TILE), F32)],
        compiler_params=_cparams("parallel", "parallel", "arbitrary"),
        name="sel_attention",
    )(q, k_sel, v_sel, sel_mask)


def _win_body(step, q_ref, k_ref, v_ref, o_ref, interior):
    tq = Q_BLOCK
    span = WINDOW + tq
    rows = NSA_REP * tq
    blocks = range(WIN_Q_BLOCKS)
    qbs = [step * WIN_Q_BLOCKS + x for x in blocks]
    starts = [pl.multiple_of(jnp.maximum(qb * tq - WINDOW, 0), tq) for qb in qbs]
    ss = [_dot_nt(q_ref[0, :, x * tq:(x + 1) * tq, :].reshape(rows, NSA_DH),
                  k_ref[0, 0, pl.ds(starts[x], span), :]) for x in blocks]
    if interior:
        i_row = lax.broadcasted_iota(jnp.int32, (rows, tq), 0) & (tq - 1)
        col = lax.broadcasted_iota(jnp.int32, (rows, tq), 1)
        ss = [jnp.concatenate([jnp.where(col > i_row, s[:, 0:tq], NEG_BIG), s[:, tq:WINDOW],
                               jnp.where(col <= i_row, s[:, WINDOW:span], NEG_BIG)], axis=1) for s in ss]
    else:
        i_row = lax.broadcasted_iota(jnp.int32, (rows, span), 0) & (tq - 1)
        col = lax.broadcasted_iota(jnp.int32, (rows, span), 1)
        masked = []
        for x in blocks:
            t_q = qbs[x] * tq + i_row
            kpos = starts[x] + col
            masked.append(jnp.where((kpos <= t_q) & (kpos > t_q - WINDOW), ss[x], NEG_BIG))
        ss = masked
    ms = [jnp.max(s, axis=-1, keepdims=True) for s in ss]
    ps = [jnp.exp2(s - m).astype(BF16) for s, m in zip(ss, ms)]
    accs = [_dot(ps[x], v_ref[0, 0, pl.ds(starts[x], span), :]) for x in blocks]
    for x in blocks:
        o = accs[x][:, 0:NSA_DH] / accs[x][:, NSA_DH:NSA_DH + 1]
        o_ref[0, :, x * tq:(x + 1) * tq, :] = o.reshape(NSA_REP, tq, NSA_DH).astype(o_ref.dtype)


def _win_kernel(q_ref, k_ref, v_ref, o_ref):
    step = pl.program_id(2)
    edge_steps = -(-(WINDOW // Q_BLOCK) // WIN_Q_BLOCKS)
    pl.when(step < edge_steps)(functools.partial(_win_body, step, q_ref, k_ref, v_ref, o_ref, False))
    pl.when(step >= edge_steps)(functools.partial(_win_body, step, q_ref, k_ref, v_ref, o_ref, True))


def _win_attention(q, k_win, v_win, t_len):
    b_sz = q.shape[0]
    tq = Q_BLOCK * WIN_Q_BLOCKS
    nq = t_len // tq
    return pl.pallas_call(
        _win_kernel,
        grid=(b_sz, NSA_GROUPS, nq),
        in_specs=[pl.BlockSpec((1, NSA_REP, tq, NSA_DH), lambda b, g, i: (b, g, i, 0)),
                  pl.BlockSpec((1, 1, t_len, NSA_DH), lambda b, g, i: (b, g, 0, 0)),
                  pl.BlockSpec((1, 1, t_len, LANE), lambda b, g, i: (b, g, 0, 0))],
        out_specs=pl.BlockSpec((1, NSA_REP, tq, NSA_DH), lambda b, g, i: (b, g, i, 0)),
        out_shape=jax.ShapeDtypeStruct((b_sz, NSA_HEADS, t_len, NSA_DH), BF16),
        compiler_params=_cparams("parallel", "parallel", "parallel"),
        name="win_attention",
    )(q, k_win, v_win)


def _merge_kernel(oc_ref, os_ref, ow_ref, zc_ref, zs_ref, zw_ref, gt_ref, ge_ref,
                  w_ref, x_ref, lw_ref, lb_ref, o_ref):
    gates = _sigmoid(gt_ref[...])
    g2 = jnp.concatenate(_split2(gates), axis=1)
    acc = None
    for n, (o_r, z_r) in enumerate(((oc_ref, zc_ref), (os_ref, zs_ref), (ow_ref, zw_ref))):
        gate = _dot(g2, ge_ref[n])
        o = jnp.concatenate([o_r[0, h] for h in range(NSA_HEADS)], axis=-1).astype(F32)
        term = gate * o * _silu(z_r[...])
        acc = term if acc is None else acc + term
    y = _dot(acc.astype(BF16), w_ref[...])
    r = DEEPNORM_ALPHA * x_ref[...] + y
    o_ref[...] = _layer_norm_rows(r, lw_ref[...], lb_ref[...])


def _merge_out(o_cmp, o_sel, o_win, z, tail, gate_expand, w_out, x_res, ln_w, ln_b, t_len, tm):
    m, d = x_res.shape
    width = NSA_HEADS * NSA_DH
    nt = t_len // tm
    ospec = pl.BlockSpec((1, NSA_HEADS, tm, NSA_DH), lambda i: (i // nt, 0, i % nt, 0))
    zspec = lambda col: pl.BlockSpec((tm, width), lambda i: (i, col))
    fixed2 = lambda i: (0, 0)
    return pl.pallas_call(
        _merge_kernel,
        grid=(m // tm,),
        in_specs=[ospec, ospec, ospec, zspec(0), zspec(1), zspec(2),
                  pl.BlockSpec((tm, LANE), lambda i: (i, 0)),
                  pl.BlockSpec((3, 2 * LANE, width), lambda i: (0, 0, 0)),
                  pl.BlockSpec((width, d), fixed2),
                  pl.BlockSpec((tm, d), lambda i: (i, 0)),
                  pl.BlockSpec((1, d), fixed2), pl.BlockSpec((1, d), fixed2)],
        out_specs=pl.BlockSpec((tm, d), lambda i: (i, 0)),
        out_shape=jax.ShapeDtypeStruct((m, d), F32),
        compiler_params=_cparams("parallel"),
        name="merge_out_proj_layernorm",
    )(o_cmp, o_sel, o_win, z, z, z, tail, gate_expand, w_out, x_res,
      ln_w.reshape(1, d), ln_b.reshape(1, d))


def _pad_cols(w, n):
    return jnp.pad(w, ((0, 0), (0, n - w.shape[1])))


def kernel(x, a_w_in, a_conv_w, a_a_log, a_dt_bias, a_norm_w, a_w_out, a_ln_w, a_ln_b,
           s_w_kv, s_pe_k, s_pe_v, s_w1_k, s_w2_k, s_w1_v, s_w2_v,
           b_w_in, b_w_out, b_ln_w, b_ln_b):
    b_sz, t_len, d = x.shape
    m = b_sz * t_len
    assert t_len % SEL_KV_TILE == 0 and t_len >= WINDOW + Q_BLOCK and d % LANE == 0
    assert t_len % (Q_BLOCK * WIN_Q_BLOCKS) == 0 and t_len % (Q_BLOCK * CMP_Q_BLOCKS) == 0
    assert t_len // SEL_BLOCK <= LANE
    assert a_w_in.shape[0] == 1 and b_w_in.shape[0] == 1
    tm_row = min(ROW_TILE, t_len)

    x2 = x.reshape(m, d)
    x16 = x2.astype(BF16)

    dn_width = DN_HEADS * DN_HEAD
    w_in = a_w_in[0]
    qkv = _gdn_qkv(x16, w_in[:, :3 * dn_width].astype(BF16), a_conv_w[0], t_len, tm_row)
    z_gdn, gates_col = _gdn_zgates(x16, _pad_cols(w_in[:, 3 * dn_width:], dn_width + LANE).astype(BF16),
                                   a_a_log[0], a_dt_bias[0], min(PROJ_ROWS, m))
    nc = t_len // GDN_CHUNK
    gates_row = gates_col[:, :2 * DN_HEADS].reshape(b_sz, nc, GDN_CHUNK, 2 * DN_HEADS).transpose(0, 1, 3, 2)
    o_gdn = _gdn_chunks(qkv, z_gdn, gates_col, gates_row, a_norm_w[0], b_sz, t_len)
    h1, h1_16 = _out_proj_ln(o_gdn, a_w_out[0].astype(BF16), x2, a_ln_w[0], a_ln_b[0], tm_row)

    pos = jnp.arange(t_len, dtype=jnp.int32)
    cos_t, sin_t = _rope_tables(pos)
    nsa_width = NSA_HEADS * NSA_DH
    kv_cmp_raw, k_sel, v_sel, k_win, v_win, q, z_b, tail_b = _layer1_proj(
        h1_16, s_w_kv.astype(BF16), _pad_cols(b_w_in[0], 4 * nsa_width + LANE).astype(BF16),
        cos_t, sin_t, b_sz, t_len, tm_row)

    n_chunk = t_len // CMP_STRIDE
    halves = CMP_BLOCK // CMP_STRIDE
    pe = jnp.stack([s_pe_k, s_pe_v]).astype(F32).reshape(2, halves, CMP_STRIDE, 1, NSA_DH)
    pe = jnp.broadcast_to(pe, (2, halves, CMP_STRIDE, 2, NSA_DH)).reshape(2, halves, CMP_STRIDE * LANE)
    w1 = jnp.stack([s_w1_k, s_w1_v]).reshape(2, halves, CMP_STRIDE, NSA_DH, CMP_HIDDEN)
    w1 = jnp.einsum('ab,shtdk->shtadbk', jnp.eye(2, dtype=w1.dtype), w1)
    w1 = w1.reshape(2, halves, CMP_STRIDE * LANE, 2 * CMP_HIDDEN).astype(BF16)
    w2 = jnp.stack([_pad_cols(s_w2_k, LANE), _pad_cols(s_w2_v, LANE)]).astype(BF16)
    cmp_end = jnp.arange(n_chunk, dtype=jnp.int32) * CMP_STRIDE + CMP_BLOCK - 1
    cos_c, sin_c = _rope_tables(cmp_end)
    kv_cmp = _compress(kv_cmp_raw, pe, w1, w2, cos_c, sin_c, b_sz, t_len)
    k_cmp, v_cmp = kv_cmp[0], kv_cmp[1]

    n_sel = t_len // SEL_BLOCK
    c_start = jnp.arange(n_chunk, dtype=jnp.int32)[:, None] * CMP_STRIDE
    s_start = jnp.arange(n_sel, dtype=jnp.int32)[None, :] * SEL_BLOCK
    ov = jnp.clip(jnp.minimum(c_start + CMP_BLOCK, s_start + SEL_BLOCK) - jnp.maximum(c_start, s_start), 0, None)
    overlap = (ov.astype(F32) / CMP_BLOCK).astype(BF16)
    overlap_t = jnp.pad(overlap.T, ((0, LANE - n_sel), (0, 0)))

    o_cmp, sel_mask = _cmp_attention(q, k_cmp, v_cmp, overlap_t, t_len)
    o_sel = _sel_attention(q, k_sel, v_sel, sel_mask, t_len)
    o_win = _win_attention(q, k_win, v_win, t_len)

    lane_id = (jnp.arange(2 * LANE, dtype=jnp.int32) % LANE)[None, :, None]
    col_head = (jnp.arange(NSA_HEADS * NSA_DH, dtype=jnp.int32) // NSA_DH)[None, None, :]
    branch = jnp.arange(3, dtype=jnp.int32)[:, None, None]
    gate_expand = (lane_id == branch * NSA_HEADS + col_head).astype(BF16)
    out = _merge_out(o_cmp, o_sel, o_win, z_b, tail_b, gate_expand, b_w_out[0].astype(BF16),
                     h1, b_ln_w[0], b_ln_b[0], t_len, tm_row)
    return out.reshape(b_sz, t_len, d)
```

```python
import functools

import jax
import jax.numpy as jnp
from jax import lax
from jax.experimental import pallas as pl
from jax.experimental.pallas import tpu as pltpu

F32 = jnp.float32
BF16 = jnp.bfloat16

DN_HEADS = 8
DN_HEAD = 128
DN_CONV = 4
NSA_HEADS = 16
NSA_GROUPS = 4
NSA_REP = NSA_HEADS // NSA_GROUPS
NSA_DH = 64
CMP_BLOCK = 32
CMP_STRIDE = 16
CMP_HIDDEN = 128
SEL_BLOCK = 64
SEL_TOPK = 16
WINDOW = 512
Q_BLOCK = 128
FORCED_SCORE = 1.0e4
N_FORCED = 3
ROPE_THETA = 10000.0
NORM_EPS = 1e-6
DEPTH = 2
DEEPNORM_ALPHA = (2.0 * DEPTH) ** 0.25

GDN_CHUNK = 128
INV_BASE = 16
GDN_SEQS = 2
LANE = 128
SUBLANE = 8
PROJ_ROWS = 1024
ROW_TILE = 512
SEL_KV_TILE = 512
SEL_Q_TILE = 512
CMP_Q_BLOCKS = 8
WIN_Q_BLOCKS = 8
NEG_BIG = -1e30
LOG2_E = 1.4426950408889634
VMEM_LIMIT = 56 * 1024 * 1024


def _cparams(*sem):
    return pltpu.CompilerParams(dimension_semantics=sem, vmem_limit_bytes=VMEM_LIMIT)


def _dot(a, b):
    return jnp.dot(a, b, preferred_element_type=F32)


def _dot_nt(a, b):
    return lax.dot_general(a, b, (((1,), (1,)), ((), ())), preferred_element_type=F32)


def _dot_tn(a, b):
    return lax.dot_general(a, b, (((0,), (0,)), ((), ())), preferred_element_type=F32)


def _split2(a):
    hi = a.astype(BF16)
    lo = (a - hi.astype(F32)).astype(BF16)
    return hi, lo


def _split3(a):
    hi = a.astype(BF16)
    r = a - hi.astype(F32)
    mid = r.astype(BF16)
    lo = (r - mid.astype(F32)).astype(BF16)
    return hi, mid, lo


def _mm(a, b):
    return _dot(a.astype(BF16), b.astype(BF16))


def _blk(idx, size):
    return idx >> (size.bit_length() - 1)


def _silu(x):
    return x / (1.0 + jnp.exp(-x))


def _sigmoid(x):
    return 1.0 / (1.0 + jnp.exp(-x))


def _layer_norm_rows(r, w, b):
    mu = jnp.mean(r, axis=-1, keepdims=True)
    d = r - mu
    var = jnp.mean(d * d, axis=-1, keepdims=True)
    return d * lax.rsqrt(var + NORM_EPS) * w + b


def _out_ln_kernel(a_ref, w_ref, x_ref, lw_ref, lb_ref, o_ref, ob_ref):
    y = _dot(a_ref[...], w_ref[...])
    r = DEEPNORM_ALPHA * x_ref[...] + y
    out = _layer_norm_rows(r, lw_ref[...], lb_ref[...])
    o_ref[...] = out
    ob_ref[...] = out.astype(BF16)


def _out_proj_ln(a, w, x_res, ln_w, ln_b, tm):
    m, k = a.shape
    n = w.shape[1]
    row = lambda i: (i, 0)
    fixed = lambda i: (0, 0)
    return pl.pallas_call(
        _out_ln_kernel,
        grid=(m // tm,),
        in_specs=[pl.BlockSpec((tm, k), row), pl.BlockSpec((k, n), fixed),
                  pl.BlockSpec((tm, n), row), pl.BlockSpec((1, n), fixed),
                  pl.BlockSpec((1, n), fixed)],
        out_specs=[pl.BlockSpec((tm, n), row), pl.BlockSpec((tm, n), row)],
        out_shape=[jax.ShapeDtypeStruct((m, n), F32), jax.ShapeDtypeStruct((m, n), BF16)],
        compiler_params=_cparams("parallel"),
        name="out_proj_layernorm",
    )(a, w, x_res, ln_w.reshape(1, n), ln_b.reshape(1, n))


def _gdn_qkv_kernel(x_ref, w_ref, cw_ref, o_ref, halo_ref, *, tm, t_len):
    j = pl.program_id(0)
    i = pl.program_id(1)
    @pl.when((i * tm) % t_len == 0)
    def _():
        halo_ref[...] = jnp.zeros(halo_ref.shape, F32)

    p = _dot(x_ref[...], w_ref[...])
    halo = halo_ref[...]
    halo_ref[...] = p[tm - SUBLANE:tm, :]
    cw = cw_ref[...]
    row = lax.broadcasted_iota(jnp.int32, halo.shape, 0)
    y = p * cw[DN_CONV - 1:DN_CONV, :]
    for back in range(1, DN_CONV):
        shifted = pltpu.roll(p, back, axis=0)
        head = jnp.where(row < back, pltpu.roll(halo, back, axis=0), shifted[0:SUBLANE, :])
        shifted = jnp.concatenate([head, shifted[SUBLANE:, :]], axis=0)
        y = y + shifted * cw[DN_CONV - 1 - back:DN_CONV - back, :]
    y = _silu(y)
    qscale = jnp.where(j == 0, DN_HEAD ** -0.5, 1.0)
    for h in range(DN_HEADS):
        sl = slice(h * DN_HEAD, (h + 1) * DN_HEAD)
        yh = y[:, sl]
        ss = jnp.sum(yh * yh, axis=-1, keepdims=True)
        scale = jnp.where(j < 2, lax.rsqrt(ss + NORM_EPS) * qscale, 1.0)
        o_ref[:, sl] = yh * scale


def _gdn_qkv(x16, w_qkv, conv_w, t_len, tm):
    m, d = x16.shape
    width = DN_HEADS * DN_HEAD
    kern = functools.partial(_gdn_qkv_kernel, tm=tm, t_len=t_len)
    return pl.pallas_call(
        kern,
        grid=(3, m // tm),
        in_specs=[pl.BlockSpec((tm, d), lambda j, i: (i, 0)),
                  pl.BlockSpec((d, width), lambda j, i: (0, j)),
                  pl.BlockSpec((DN_CONV, width), lambda j, i: (0, j))],
        out_specs=pl.BlockSpec((tm, width), lambda j, i: (i, j)),
        out_shape=jax.ShapeDtypeStruct((m, 3 * width), F32),
        scratch_shapes=[pltpu.VMEM((SUBLANE, width), F32)],
        compiler_params=_cparams("parallel", "arbitrary"),
        name="gdn_qkv_proj_conv",
    )(x16, w_qkv, conv_w)


def _gdn_zgate_kernel(x_ref, w_ref, al_ref, dt_ref, z_ref, g_ref):
    width = DN_HEADS * DN_HEAD
    proj = _dot(x_ref[...], w_ref[...])
    z_ref[...] = proj[:, 0:width]
    x = proj[:, width:]
    lane = lax.broadcasted_iota(jnp.int32, x.shape, 1)
    beta = _sigmoid(x)
    z = x + dt_ref[...]
    softplus = jnp.maximum(z, 0.0) + jnp.log(1.0 + jnp.exp(-jnp.abs(z)))
    g = -jnp.exp(al_ref[...]) * softplus
    g_ref[...] = jnp.where(lane < DN_HEADS, beta, jnp.where(lane < 2 * DN_HEADS, g, 0.0))


def _gdn_zgates(x16, w_z_logits, a_log, dt_bias, tm):
    m, d = x16.shape
    width = DN_HEADS * DN_HEAD
    pad = lambda v: jnp.zeros((1, LANE), F32).at[0, DN_HEADS:2 * DN_HEADS].set(v.astype(F32))
    return pl.pallas_call(
        _gdn_zgate_kernel,
        grid=(m // tm,),
        in_specs=[pl.BlockSpec((tm, d), lambda i: (i, 0)),
                  pl.BlockSpec((d, width + LANE), lambda i: (0, 0)),
                  pl.BlockSpec((1, LANE), lambda i: (0, 0)),
                  pl.BlockSpec((1, LANE), lambda i: (0, 0))],
        out_specs=[pl.BlockSpec((tm, width), lambda i: (i, 0)),
                   pl.BlockSpec((tm, LANE), lambda i: (i, 0))],
        out_shape=[jax.ShapeDtypeStruct((m, width), F32), jax.ShapeDtypeStruct((m, LANE), F32)],
        compiler_params=_cparams("parallel"),
        name="gdn_z_gates",
    )(x16, w_z_logits, pad(a_log), pad(dt_bias))


def _inv_unit_lower(lows, ii, jj):
    c = lows[0].shape[0]
    eye = (ii == jj).astype(F32)
    same = _blk(ii, INV_BASE) == _blk(jj, INV_BASE)
    ds = [jnp.where(same, low, 0.0) for low in lows]
    xs = [eye - d for d in ds]
    ps = ds
    span = 2
    while span < INV_BASE:
        ps = [_mm(p, p) for p in ps]
        xs = [_mm(x, eye + p) for x, p in zip(xs, ps)]
        span *= 2
    s = INV_BASE
    while s < c:
        off = (_blk(ii, 2 * s) == _blk(jj, 2 * s)) & (_blk(ii, s) != _blk(jj, s))
        ts = [_mm(x, jnp.where(off, low, 0.0)) for x, low in zip(xs, lows)]
        xs = [x - _mm(t, x) for x, t in zip(xs, ts)]
        s *= 2
    return xs


def _gdn_chunk_kernel(q_ref, k_ref, v_ref, z_ref, gc_ref, gr_ref, nw_ref, o_ref, s_ref):
    c = GDN_CHUNK

    @pl.when(pl.program_id(1) == 0)
    def _():
        s_ref[...] = jnp.zeros_like(s_ref)

    ii = lax.broadcasted_iota(jnp.int32, (c, c), 0)
    jj = lax.broadcasted_iota(jnp.int32, (c, c), 1)
    incl = ii >= jj
    strict = ii > jj
    ltri = incl.astype(BF16)
    utri = (ii <= jj).astype(BF16)
    nb = q_ref.shape[0]
    gcols, gcs_cols, gcs_rows = [], [], []
    for bb in range(nb):
        gcol = gc_ref[bb]
        gc3 = _split3(gcol)
        gr3 = _split3(gr_ref[bb, 0])
        gcols.append(gcol)
        gcs_cols.append(_dot(ltri, gc3[0]) + (_dot(ltri, gc3[1]) + _dot(ltri, gc3[2])))
        gcs_rows.append(_dot(gr3[0], utri) + (_dot(gr3[1], utri) + _dot(gr3[2], utri)))
    nw = nw_ref[...]

    chains = [(bb, h) for bb in range(nb) for h in range(DN_HEADS)]
    tile = lambda ref, bb, h: ref[bb, :, h * DN_HEAD:(h + 1) * DN_HEAD]
    betas = [gcols[bb][:, h:h + 1] for bb, h in chains]
    gccs = [gcs_cols[bb][:, DN_HEADS + h:DN_HEADS + h + 1] for bb, h in chains]
    gcrs = [gcs_rows[bb][DN_HEADS + h:DN_HEADS + h + 1, :] for bb, h in chains]
    decays = [jnp.where(incl, jnp.exp(jnp.where(incl, gcc - gcr, 0.0)), 0.0)
              for gcc, gcr in zip(gccs, gcrs)]
    kbs = [tile(k_ref, bb, h) * beta for (bb, h), beta in zip(chains, betas)]
    k16s = [tile(k_ref, bb, h).astype(BF16) for bb, h in chains]
    lows = [jnp.where(strict, _dot_nt(kb.astype(BF16), k16) * decay, 0.0)
            for kb, k16, decay in zip(kbs, k16s, decays)]
    attns = [jnp.where(incl, _dot_nt(tile(q_ref, bb, h).astype(BF16), k16) * decay, 0.0).astype(BF16)
             for (bb, h), k16, decay in zip(chains, k16s, decays)]
    tinvs = _inv_unit_lower(lows, ii, jj)
    egs = [jnp.exp(gcc) for gcc in gccs]
    sols = [_mm(tinv, jnp.concatenate([tile(v_ref, bb, h) * beta, kb * eg], axis=1))
            for tinv, (bb, h), beta, kb, eg in zip(tinvs, chains, betas, kbs, egs)]
    g_lasts = [gcc[c - 1:c, :] for gcc in gccs]
    kds = [(tile(k_ref, bb, h) * jnp.exp(g_last - gcc)).astype(BF16)
           for (bb, h), g_last, gcc in zip(chains, g_lasts, gccs)]
    qgs = [(tile(q_ref, bb, h) * eg).astype(BF16) for (bb, h), eg in zip(chains, egs)]
    states = [s_ref[bb * DN_HEADS + h] for bb, h in chains]
    s16s = [state.astype(BF16) for state in states]
    vn16s = [(sol[:, :DN_HEAD] - _dot(sol[:, DN_HEAD:].astype(BF16), s16)).astype(BF16)
             for sol, s16 in zip(sols, s16s)]
    outs = [_dot(qg, s16) + _dot(attn, vn16)
            for qg, s16, attn, vn16 in zip(qgs, s16s, attns, vn16s)]
    for n, (bb, h) in enumerate(chains):
        s_ref[bb * DN_HEADS + h] = states[n] * jnp.exp(g_lasts[n]) + _dot_tn(kds[n], vn16s[n])
    for n, (bb, h) in enumerate(chains):
        o = outs[n]
        on = o * lax.rsqrt(jnp.mean(o * o, axis=-1, keepdims=True) + NORM_EPS) * nw
        o_ref[bb, :, h * DN_HEAD:(h + 1) * DN_HEAD] = (on * _silu(tile(z_ref, bb, h))).astype(o_ref.dtype)


def _gdn_chunks(qkv, z, gates_col, gates_row, norm_w, b_sz, t_len):
    c = GDN_CHUNK
    nc = t_len // c
    width = DN_HEADS * DN_HEAD
    nb = GDN_SEQS if b_sz % GDN_SEQS == 0 else 1
    seq = lambda a: a.reshape(b_sz, t_len, a.shape[-1])
    tok = lambda col: (lambda b, i: (b, i, col))
    out = pl.pallas_call(
        _gdn_chunk_kernel,
        grid=(b_sz // nb, nc),
        in_specs=[pl.BlockSpec((nb, c, width), tok(0)), pl.BlockSpec((nb, c, width), tok(1)),
                  pl.BlockSpec((nb, c, width), tok(2)), pl.BlockSpec((nb, c, width), tok(0)),
                  pl.BlockSpec((nb, c, LANE), tok(0)),
                  pl.BlockSpec((nb, 1, 2 * DN_HEADS, c), lambda b, i: (b, i, 0, 0)),
                  pl.BlockSpec((1, DN_HEAD), lambda b, i: (0, 0))],
        out_specs=pl.BlockSpec((nb, c, width), tok(0)),
        out_shape=jax.ShapeDtypeStruct((b_sz, t_len, width), BF16),
        scratch_shapes=[pltpu.VMEM((nb * DN_HEADS, DN_HEAD, DN_HEAD), F32)],
        compiler_params=_cparams("parallel", "arbitrary"),
        name="gdn_delta_rule",
    )(seq(qkv), seq(qkv), seq(qkv), seq(z), seq(gates_col), gates_row,
      norm_w.reshape(1, DN_HEAD).astype(F32))
    return out.reshape(b_sz * t_len, width)


def _rope_tables(pos):
    half = NSA_DH // 2
    inv_freq = ROPE_THETA ** (-jnp.arange(half, dtype=F32) / half)
    ang = pos.astype(F32)[:, None] * inv_freq[None, :]
    cos, sin = jnp.cos(ang), jnp.sin(ang)
    cos_t = jnp.concatenate([cos, cos, cos, cos], axis=-1)
    sin_t = jnp.concatenate([-sin, sin, -sin, sin], axis=-1)
    return cos_t, sin_t


def _rope128(x, cos_t, sin_t):
    lane = lax.broadcasted_iota(jnp.int32, x.shape, 1)
    lower = (lane & (NSA_DH - 1)) < (NSA_DH // 2)
    fwd = pltpu.roll(x, LANE - NSA_DH // 2, axis=1)
    bwd = pltpu.roll(x, NSA_DH // 2, axis=1)
    return x * cos_t + jnp.where(lower, fwd, bwd) * sin_t


def _kv_proj_kernel(x_ref, w_ref, cos_ref, sin_ref, cmp_ref, ks_ref, vs_ref, kw_ref, vw_ref, *, tm, nt):
    gw = NSA_GROUPS * NSA_DH
    kv = _dot(x_ref[...], w_ref[...])
    cmp_ref[...] = kv[:, 0:2 * gw]
    cos_t = cos_ref[...]
    sin_t = sin_ref[...]
    t_pos = (pl.program_id(0) % nt) * tm + lax.broadcasted_iota(jnp.int32, (tm, LANE), 0)
    lane = lax.broadcasted_iota(jnp.int32, (tm, LANE), 1)
    onehot = (_blk(t_pos, SEL_BLOCK) == lane).astype(BF16)
    ones_col = (lane == NSA_DH).astype(BF16)
    zeros_half = jnp.zeros((tm, NSA_DH), BF16)
    for pair in range(NSA_GROUPS // 2):
        cols = lambda src: kv[:, src * gw + pair * LANE: src * gw + (pair + 1) * LANE]
        k_sel = _rope128(cols(2), cos_t, sin_t).astype(BF16)
        k_win = _rope128(cols(4), cos_t, sin_t).astype(BF16)
        v_sel = cols(3).astype(BF16)
        v_win = cols(5).astype(BF16)
        for half in range(2):
            g = 2 * pair + half
            sl = slice(half * NSA_DH, (half + 1) * NSA_DH)
            ks_ref[0, g, :, 0:LANE] = onehot
            ks_ref[0, g, :, LANE:LANE + NSA_DH] = k_sel[:, sl]
            ks_ref[0, g, :, LANE + NSA_DH:2 * LANE] = zeros_half
            kw_ref[0, g] = k_win[:, sl]
            for val, dst in ((v_sel, vs_ref), (v_win, vw_ref)):
                dst[0, g] = ones_col
                dst[0, g, :, 0:NSA_DH] = val[:, sl]


def _nsa_proj_kernel(x_ref, w_ref, cos_ref, sin_ref, q_ref, z_ref, t_ref):
    width = NSA_HEADS * NSA_DH
    proj = _dot(x_ref[...], w_ref[...])
    z_ref[...] = proj[:, width:4 * width]
    t_ref[...] = proj[:, 4 * width:]
    cos_t = cos_ref[...]
    sin_t = sin_ref[...]
    scale = NSA_DH ** -0.5 * LOG2_E
    for pair in range(NSA_HEADS // 2):
        x = _rope128(proj[:, pair * LANE:(pair + 1) * LANE], cos_t, sin_t) * scale
        q_ref[0, 2 * pair] = x[:, :NSA_DH].astype(BF16)
        q_ref[0, 2 * pair + 1] = x[:, NSA_DH:].astype(BF16)


def _layer1_proj_kernel(x_ref, wkv_ref, wq_ref, cos_ref, sin_ref, cmp_ref, ks_ref, vs_ref, kw_ref, vw_ref,
                        q_ref, z_ref, t_ref, *, tm, nt):
    _kv_proj_kernel(x_ref, wkv_ref, cos_ref, sin_ref, cmp_ref, ks_ref, vs_ref, kw_ref, vw_ref, tm=tm, nt=nt)
    _nsa_proj_kernel(x_ref, wq_ref, cos_ref, sin_ref, q_ref, z_ref, t_ref)


def _layer1_proj(h16, w_kv, w_all, cos_t, sin_t, b_sz, t_len, tm):
    nt = t_len // tm
    m, d = h16.shape
    gw = NSA_GROUPS * NSA_DH
    width = NSA_HEADS * NSA_DH
    kv_shape = lambda w: jax.ShapeDtypeStruct((b_sz, NSA_GROUPS, t_len, w), BF16)
    kv_spec = lambda w: pl.BlockSpec((1, NSA_GROUPS, tm, w), lambda i: (i // nt, 0, i % nt, 0))
    rows = lambda w: pl.BlockSpec((tm, w), lambda i: (i, 0))
    table = pl.BlockSpec((tm, LANE), lambda i: (i % nt, 0))
    return pl.pallas_call(
        functools.partial(_layer1_proj_kernel, tm=tm, nt=nt),
        grid=(b_sz * nt,),
        in_specs=[rows(d),
                  pl.BlockSpec((d, 6 * gw), lambda i: (0, 0)),
                  pl.BlockSpec((d, 4 * width + LANE), lambda i: (0, 0)),
                  table, table],
        out_specs=[rows(2 * gw), kv_spec(2 * LANE), kv_spec(LANE), kv_spec(NSA_DH), kv_spec(LANE),
                   pl.BlockSpec((1, NSA_HEADS, tm, NSA_DH), lambda i: (i // nt, 0, i % nt, 0)),
                   rows(3 * width), rows(LANE)],
        out_shape=[jax.ShapeDtypeStruct((m, 2 * gw), F32),
                   kv_shape(2 * LANE), kv_shape(LANE), kv_shape(NSA_DH), kv_shape(LANE),
                   jax.ShapeDtypeStruct((b_sz, NSA_HEADS, t_len, NSA_DH), BF16),
                   jax.ShapeDtypeStruct((m, 3 * width), F32),
                   jax.ShapeDtypeStruct((m, LANE), F32)],
        compiler_params=_cparams("parallel"),
        name="layer1_proj_rope_layout",
    )(h16, w_kv, w_all, cos_t, sin_t)


def _compress_kernel(x_ref, pe_ref, w1_ref, w2_ref, cos_ref, sin_ref, o_ref, *, n_chunk):
    is_key = pl.program_id(0) == 0
    chunk = jnp.concatenate([x_ref[pl.ds(tau, n_chunk, stride=CMP_STRIDE), :]
                             for tau in range(CMP_STRIDE)], axis=1)
    a = _dot((chunk + pe_ref[0, 0:1, :]).astype(BF16), w1_ref[0, 0])
    b = _dot((chunk + pe_ref[0, 1:2, :]).astype(BF16), w1_ref[0, 1])
    hid = a + pltpu.roll(b, n_chunk - 1, axis=0)
    act = _silu(hid).astype(BF16)
    for half in range(2):
        y = _dot(act[:, half * CMP_HIDDEN:(half + 1) * CMP_HIDDEN], w2_ref[0])
        y = jnp.where(is_key, _rope128(y, cos_ref[...], sin_ref[...]), y)
        o_ref[0, 0, half] = y[:, :NSA_DH].astype(BF16)


def _compress(kv_raw, pe, w1, w2, cos_c, sin_c, b_sz, t_len):
    n_chunk = t_len // CMP_STRIDE
    pairs = NSA_GROUPS // 2
    flat = CMP_STRIDE * LANE
    return pl.pallas_call(
        functools.partial(_compress_kernel, n_chunk=n_chunk),
        grid=(2, b_sz, pairs),
        in_specs=[pl.BlockSpec((t_len, LANE), lambda s, b, p: (b, s * pairs + p)),
                  pl.BlockSpec((1, 2, flat), lambda s, b, p: (s, 0, 0)),
                  pl.BlockSpec((1, 2, flat, 2 * CMP_HIDDEN), lambda s, b, p: (s, 0, 0, 0)),
                  pl.BlockSpec((1, CMP_HIDDEN, LANE), lambda s, b, p: (s, 0, 0)),
                  pl.BlockSpec((n_chunk, LANE), lambda s, b, p: (0, 0)),
                  pl.BlockSpec((n_chunk, LANE), lambda s, b, p: (0, 0))],
        out_specs=pl.BlockSpec((1, 1, 2, n_chunk, NSA_DH), lambda s, b, p: (s, b, p, 0, 0)),
        out_shape=jax.ShapeDtypeStruct((2, b_sz, NSA_GROUPS, n_chunk, NSA_DH), BF16),
        compiler_params=_cparams("parallel", "parallel", "parallel"),
        name="compress_blocks",
    )(kv_raw, pe, w1, w2, cos_c, sin_c)


def _cmp_kernel(q_ref, k_ref, v_ref, ovt_ref, o_ref, sel_ref, *, n_sel):
    step = pl.program_id(2)
    n_cmp = k_ref.shape[2]
    steps_per_group = LANE * CMP_STRIDE // (Q_BLOCK * CMP_Q_BLOCKS)
    n_var = max(1, n_cmp // LANE)
    for var in range(n_var):
        n_col = min(n_cmp, LANE * (var + 1))
        n_blk = min(n_sel, n_col * CMP_STRIDE // SEL_BLOCK)
        pl.when(step // steps_per_group == var)(functools.partial(
            _cmp_body, step, q_ref, k_ref, v_ref, ovt_ref, o_ref, sel_ref, n_col, n_blk))


def _cmp_body(step, q_ref, k_ref, v_ref, ovt_ref, o_ref, sel_ref, n_cmp, n_sel):
    tq = Q_BLOCK
    rows = NSA_REP * tq
    n_pad = -(-n_sel // SUBLANE) * SUBLANE
    blocks = range(CMP_Q_BLOCKS)
    t0s = [(step * CMP_Q_BLOCKS + x) * tq for x in blocks]
    kc = k_ref[0, 0, 0:n_cmp, :]
    ss = [_dot_nt(q_ref[0, :, x * tq:(x + 1) * tq, :].reshape(rows, NSA_DH), kc) for x in blocks]
    lag = (lax.broadcasted_iota(jnp.int32, (tq, n_cmp), 1) * CMP_STRIDE + (CMP_BLOCK - 1)
           - lax.broadcasted_iota(jnp.int32, (tq, n_cmp), 0))
    ss = [s + jnp.concatenate([jnp.where(lag <= t0, 0.0, NEG_BIG)] * NSA_REP, axis=0)
          for s, t0 in zip(ss, t0s)]
    ms = [jnp.max(s, axis=-1, keepdims=True) for s in ss]
    es = [jnp.exp2(s - m) for s, m in zip(ss, ms)]
    ls = [jnp.sum(e, axis=-1, keepdims=True) for e in es]
    i_col = lax.broadcasted_iota(jnp.int32, (rows, 1), 0) & (tq - 1)
    invs = [jnp.where(t0 + i_col >= CMP_BLOCK - 1, 1.0 / l, 0.0) for t0, l in zip(t0s, ls)]
    ps = [e * inv for e, inv in zip(es, invs)]
    vc = v_ref[0, 0, 0:n_cmp, :]
    for x in blocks:
        o = _dot(ps[x].astype(BF16), vc)
        o_ref[0, :, x * tq:(x + 1) * tq, :] = o.reshape(NSA_REP, tq, NSA_DH).astype(o_ref.dtype)

    ovt = ovt_ref[0:n_pad, 0:n_cmp]
    imps = []
    for p in ps:
        psum = p[0:tq]
        for r in range(1, NSA_REP):
            psum = psum + p[r * tq:(r + 1) * tq]
        ph, plo = _split2(psum)
        imps.append(_dot_nt(ovt, ph) + _dot_nt(ovt, plo))

    i_q = lax.broadcasted_iota(jnp.int32, (n_pad, tq), 1)
    blk = lax.broadcasted_iota(jnp.int32, (n_pad, tq), 0)
    blk_f = blk.astype(F32)
    taken = -float(2 ** 127)
    scores = []
    for t0, imp in zip(t0s, imps):
        t_q = t0 + i_q
        cur = _blk(t_q, SEL_BLOCK)
        forced = (blk == 0) | (blk == cur) | (blk == cur - 1)
        score = jnp.where(blk * SEL_BLOCK <= t_q, jnp.where(forced, taken, imp), -1.0)
        if n_pad > n_sel:
            score = jnp.where(blk < n_sel, score, -2.0)
        scores.append(score)
    for _ in range(min(SEL_TOPK, n_sel) - N_FORCED):
        mxs = [jnp.max(score, axis=0, keepdims=True) for score in scores]
        firsts = [jnp.min(jnp.where(score == mx, blk_f, float(LANE)), axis=0, keepdims=True)
                  for score, mx in zip(scores, mxs)]
        scores = [jnp.where(blk_f == first, taken, score)
                  for score, first in zip(scores, firsts)]
    for x in blocks:
        chosen = jnp.where(scores[x] == taken, 0.0, NEG_BIG)
        if n_pad < LANE:
            chosen = jnp.concatenate([chosen, jnp.full((LANE - n_pad, tq), NEG_BIG, F32)], axis=0)
        sel_ref[0, 0, x * tq:(x + 1) * tq, :] = chosen.T.astype(sel_ref.dtype)


def _cmp_attention(q, k_cmp, v_cmp, overlap_t, t_len):
    b_sz = q.shape[0]
    n_cmp = k_cmp.shape[2]
    tq = Q_BLOCK * CMP_Q_BLOCKS
    nq = t_len // tq
    return pl.pallas_call(
        functools.partial(_cmp_kernel, n_sel=t_len // SEL_BLOCK),
        grid=(b_sz, NSA_GROUPS, nq),
        in_specs=[pl.BlockSpec((1, NSA_REP, tq, NSA_DH), lambda b, g, i: (b, g, i, 0)),
                  pl.BlockSpec((1, 1, n_cmp, NSA_DH), lambda b, g, i: (b, g, 0, 0)),
                  pl.BlockSpec((1, 1, n_cmp, NSA_DH), lambda b, g, i: (b, g, 0, 0)),
                  pl.BlockSpec((LANE, n_cmp), lambda b, g, i: (0, 0))],
        out_specs=[pl.BlockSpec((1, NSA_REP, tq, NSA_DH), lambda b, g, i: (b, g, i, 0)),
                   pl.BlockSpec((1, 1, tq, LANE), lambda b, g, i: (b, g, i, 0))],
        out_shape=[jax.ShapeDtypeStruct((b_sz, NSA_HEADS, t_len, NSA_DH), BF16),
                   jax.ShapeDtypeStruct((b_sz, NSA_GROUPS, t_len, LANE), BF16)],
        compiler_params=_cparams("parallel", "parallel", "parallel"),
        name="cmp_attention_topk",
    )(q, k_cmp, v_cmp, overlap_t)


def _online_softmax_step(s, vt, m_ref, acc_ref):
    reps = s.shape[1] // LANE
    m_prev = m_ref[...]
    m_next = jnp.maximum(m_prev, jnp.max(s, axis=-1, keepdims=True))
    p = jnp.exp2(s - jnp.concatenate([m_next] * reps, axis=1))
    alpha = jnp.exp2(m_prev - m_next)
    acc_ref[...] = alpha * acc_ref[...] + _dot(p.astype(BF16), vt)
    m_ref[...] = m_next


def _sel_kernel(q_ref, k_ref, v_ref, sel_ref, o_ref, qa_ref, m_ref, acc_ref, sa_ref, sb_ref):
    qb = pl.program_id(2)
    tq = SEL_Q_TILE
    tk = SEL_KV_TILE
    rows = NSA_REP * tq
    mask = sel_ref[0, 0]
    for r in range(NSA_REP):
        qa_ref[r * tq:(r + 1) * tq, 0:LANE] = mask
        qa_ref[r * tq:(r + 1) * tq, LANE:LANE + NSA_DH] = q_ref[0, r]
        qa_ref[r * tq:(r + 1) * tq, LANE + NSA_DH:2 * LANE] = jnp.zeros((tq, NSA_DH), BF16)
    m_ref[...] = jnp.full(m_ref.shape, -jnp.inf, F32)
    acc_ref[...] = jnp.zeros(acc_ref.shape, F32)
    last = (qb * tq) // tk

    def scores(j):
        k0 = pl.multiple_of(j * tk, tk)
        return _dot_nt(qa_ref[...], k_ref[0, 0, pl.ds(k0, tk), :])

    def values(j):
        return v_ref[0, 0, pl.ds(pl.multiple_of(j * tk, tk), tk), :]

    def causal_step(s):
        row = lax.broadcasted_iota(jnp.int32, (rows, tk), 0)
        kpos = last * tk + lax.broadcasted_iota(jnp.int32, (rows, tk), 1)
        s = jnp.where(kpos <= qb * tq + (row & (tq - 1)), s, NEG_BIG)
        _online_softmax_step(s, values(last), m_ref, acc_ref)

    sa_ref[...] = scores(0)

    def pair(i, carry):
        sb_ref[...] = scores(2 * i + 1)
        _online_softmax_step(sa_ref[...], values(2 * i), m_ref, acc_ref)
        sa_ref[...] = scores(2 * i + 2)
        _online_softmax_step(sb_ref[...], values(2 * i + 1), m_ref, acc_ref)
        return carry

    lax.fori_loop(0, last // 2, pair, 0)

    @pl.when(last % 2 == 0)
    def _():
        causal_step(sa_ref[...])

    @pl.when(last % 2 == 1)
    def _():
        sb_ref[...] = scores(last)
        _online_softmax_step(sa_ref[...], values(last - 1), m_ref, acc_ref)
        causal_step(sb_ref[...])

    acc = acc_ref[...]
    o = acc[:, 0:NSA_DH] / acc[:, NSA_DH:NSA_DH + 1]
    o_ref[0] = o.reshape(NSA_REP, tq, NSA_DH).astype(o_ref.dtype)


def _sel_attention(q, k_sel, v_sel, sel_mask, t_len):
    b_sz = q.shape[0]
    tq = SEL_Q_TILE
    nq = t_len // tq
    rows = NSA_REP * tq
    return pl.pallas_call(
        _sel_kernel,
        grid=(b_sz, NSA_GROUPS, nq),
        in_specs=[pl.BlockSpec((1, NSA_REP, tq, NSA_DH), lambda b, g, i: (b, g, i, 0)),
                  pl.BlockSpec((1, 1, t_len, 2 * LANE), lambda b, g, i: (b, g, 0, 0)),
                  pl.BlockSpec((1, 1, t_len, LANE), lambda b, g, i: (b, g, 0, 0)),
                  pl.BlockSpec((1, 1, tq, LANE), lambda b, g, i: (b, g, i, 0))],
        out_specs=pl.BlockSpec((1, NSA_REP, tq, NSA_DH), lambda b, g, i: (b, g, i, 0)),
        out_shape=jax.ShapeDtypeStruct((b_sz, NSA_HEADS, t_len, NSA_DH), BF16),
        scratch_shapes=[pltpu.VMEM((rows, 2 * LANE), BF16), pltpu.VMEM((rows, LANE), F32),
                        pltpu.VMEM((rows, LANE), F32), pltpu.VMEM((rows, SEL_KV_TILE), F32),
                        pltpu.VMEM((rows, SEL_KV_TILE), F32)],
        compiler_params=_cparams("parallel", "parallel", "arbitrary"),
        name="sel_attention",
    )(q, k_sel, v_sel, sel_mask)


def _win_body(step, q_ref, k_ref, v_ref, o_ref, interior):
    tq = Q_BLOCK
    span = WINDOW + tq
    rows = NSA_REP * tq
    blocks = range(WIN_Q_BLOCKS)
    qbs = [step * WIN_Q_BLOCKS + x for x in blocks]
    starts = [pl.multiple_of(jnp.maximum(qb * tq - WINDOW, 0), tq) for qb in qbs]
    ss = [_dot_nt(q_ref[0, :, x * tq:(x + 1) * tq, :].reshape(rows, NSA_DH),
                  k_ref[0, 0, pl.ds(starts[x], span), :]) for x in blocks]
    if interior:
        i_row = lax.broadcasted_iota(jnp.int32, (rows, tq), 0) & (tq - 1)
        col = lax.broadcasted_iota(jnp.int32, (rows, tq), 1)
        ss = [jnp.concatenate([jnp.where(col > i_row, s[:, 0:tq], NEG_BIG), s[:, tq:WINDOW],
                               jnp.where(col <= i_row, s[:, WINDOW:span], NEG_BIG)], axis=1) for s in ss]
    else:
        i_row = lax.broadcasted_iota(jnp.int32, (rows, span), 0) & (tq - 1)
        col = lax.broadcasted_iota(jnp.int32, (rows, span), 1)
        masked = []
        for x in blocks:
            t_q = qbs[x] * tq + i_row
            kpos = starts[x] + col
            masked.append(jnp.where((kpos <= t_q) & (kpos > t_q - WINDOW), ss[x], NEG_BIG))
        ss = masked
    ms = [jnp.max(s, axis=-1, keepdims=True) for s in ss]
    ps = [jnp.exp2(s - m).astype(BF16) for s, m in zip(ss, ms)]
    accs = [_dot(ps[x], v_ref[0, 0, pl.ds(starts[x], span), :]) for x in blocks]
    for x in blocks:
        o = accs[x][:, 0:NSA_DH] / accs[x][:, NSA_DH:NSA_DH + 1]
        o_ref[0, :, x * tq:(x + 1) * tq, :] = o.reshape(NSA_REP, tq, NSA_DH).astype(o_ref.dtype)


def _win_kernel(q_ref, k_ref, v_ref, o_ref):
    step = pl.program_id(2)
    edge_steps = -(-(WINDOW // Q_BLOCK) // WIN_Q_BLOCKS)
    pl.when(step < edge_steps)(functools.partial(_win_body, step, q_ref, k_ref, v_ref, o_ref, False))
    pl.when(step >= edge_steps)(functools.partial(_win_body, step, q_ref, k_ref, v_ref, o_ref, True))


def _win_attention(q, k_win, v_win, t_len):
    b_sz = q.shape[0]
    tq = Q_BLOCK * WIN_Q_BLOCKS
    nq = t_len // tq
    return pl.pallas_call(
        _win_kernel,
        grid=(b_sz, NSA_GROUPS, nq),
        in_specs=[pl.BlockSpec((1, NSA_REP, tq, NSA_DH), lambda b, g, i: (b, g, i, 0)),
                  pl.BlockSpec((1, 1, t_len, NSA_DH), lambda b, g, i: (b, g, 0, 0)),
                  pl.BlockSpec((1, 1, t_len, LANE), lambda b, g, i: (b, g, 0, 0))],
        out_specs=pl.BlockSpec((1, NSA_REP, tq, NSA_DH), lambda b, g, i: (b, g, i, 0)),
        out_shape=jax.ShapeDtypeStruct((b_sz, NSA_HEADS, t_len, NSA_DH), BF16),
        compiler_params=_cparams("parallel", "parallel", "parallel"),
        name="win_attention",
    )(q, k_win, v_win)


def _merge_kernel(oc_ref, os_ref, ow_ref, zc_ref, zs_ref, zw_ref, gt_ref, ge_ref,
                  w_ref, x_ref, lw_ref, lb_ref, o_ref):
    gates = _sigmoid(gt_ref[...])
    g2 = jnp.concatenate(_split2(gates), axis=1)
    acc = None
    for n, (o_r, z_r) in enumerate(((oc_ref, zc_ref), (os_ref, zs_ref), (ow_ref, zw_ref))):
        gate = _dot(g2, ge_ref[n])
        o = jnp.concatenate([o_r[0, h] for h in range(NSA_HEADS)], axis=-1).astype(F32)
        term = gate * o * _silu(z_r[...])
        acc = term if acc is None else acc + term
    y = _dot(acc.astype(BF16), w_ref[...])
    r = DEEPNORM_ALPHA * x_ref[...] + y
    o_ref[...] = _layer_norm_rows(r, lw_ref[...], lb_ref[...])


def _merge_out(o_cmp, o_sel, o_win, z, tail, gate_expand, w_out, x_res, ln_w, ln_b, t_len, tm):
    m, d = x_res.shape
    width = NSA_HEADS * NSA_DH
    nt = t_len // tm
    ospec = pl.BlockSpec((1, NSA_HEADS, tm, NSA_DH), lambda i: (i // nt, 0, i % nt, 0))
    zspec = lambda col: pl.BlockSpec((tm, width), lambda i: (i, col))
    fixed2 = lambda i: (0, 0)
    return pl.pallas_call(
        _merge_kernel,
        grid=(m // tm,),
        in_specs=[ospec, ospec, ospec, zspec(0), zspec(1), zspec(2),
                  pl.BlockSpec((tm, LANE), lambda i: (i, 0)),
                  pl.BlockSpec((3, 2 * LANE, width), lambda i: (0, 0, 0)),
                  pl.BlockSpec((width, d), fixed2),
                  pl.BlockSpec((tm, d), lambda i: (i, 0)),
                  pl.BlockSpec((1, d), fixed2), pl.BlockSpec((1, d), fixed2)],
        out_specs=pl.BlockSpec((tm, d), lambda i: (i, 0)),
        out_shape=jax.ShapeDtypeStruct((m, d), F32),
        compiler_params=_cparams("parallel"),
        name="merge_out_proj_layernorm",
    )(o_cmp, o_sel, o_win, z, z, z, tail, gate_expand, w_out, x_res,
      ln_w.reshape(1, d), ln_b.reshape(1, d))


def _pad_cols(w, n):
    return jnp.pad(w, ((0, 0), (0, n - w.shape[1])))


def kernel(x, a_w_in, a_conv_w, a_a_log, a_dt_bias, a_norm_w, a_w_out, a_ln_w, a_ln_b,
           s_w_kv, s_pe_k, s_pe_v, s_w1_k, s_w2_k, s_w1_v, s_w2_v,
           b_w_in, b_w_out, b_ln_w, b_ln_b):
    b_sz, t_len, d = x.shape
    m = b_sz * t_len
    assert t_len % SEL_KV_TILE == 0 and t_len >= WINDOW + Q_BLOCK and d % LANE == 0
    assert t_len % (Q_BLOCK * WIN_Q_BLOCKS) == 0 and t_len % (Q_BLOCK * CMP_Q_BLOCKS) == 0
    assert t_len // SEL_BLOCK <= LANE
    assert a_w_in.shape[0] == 1 and b_w_in.shape[0] == 1
    tm_row = min(ROW_TILE, t_len)

    x2 = x.reshape(m, d)
    x16 = x2.astype(BF16)

    dn_width = DN_HEADS * DN_HEAD
    w_in = a_w_in[0]
    qkv = _gdn_qkv(x16, w_in[:, :3 * dn_width].astype(BF16), a_conv_w[0], t_len, tm_row)
    z_gdn, gates_col = _gdn_zgates(x16, _pad_cols(w_in[:, 3 * dn_width:], dn_width + LANE).astype(BF16),
                                   a_a_log[0], a_dt_bias[0], min(PROJ_ROWS, m))
    nc = t_len // GDN_CHUNK
    gates_row = gates_col[:, :2 * DN_HEADS].reshape(b_sz, nc, GDN_CHUNK, 2 * DN_HEADS).transpose(0, 1, 3, 2)
    o_gdn = _gdn_chunks(qkv, z_gdn, gates_col, gates_row, a_norm_w[0], b_sz, t_len)
    h1, h1_16 = _out_proj_ln(o_gdn, a_w_out[0].astype(BF16), x2, a_ln_w[0], a_ln_b[0], tm_row)

    pos = jnp.arange(t_len, dtype=jnp.int32)
    cos_t, sin_t = _rope_tables(pos)
    nsa_width = NSA_HEADS * NSA_DH
    kv_cmp_raw, k_sel, v_sel, k_win, v_win, q, z_b, tail_b = _layer1_proj(
        h1_16, s_w_kv.astype(BF16), _pad_cols(b_w_in[0], 4 * nsa_width + LANE).astype(BF16),
        cos_t, sin_t, b_sz, t_len, tm_row)

    n_chunk = t_len // CMP_STRIDE
    halves = CMP_BLOCK // CMP_STRIDE
    pe = jnp.stack([s_pe_k, s_pe_v]).astype(F32).reshape(2, halves, CMP_STRIDE, 1, NSA_DH)
    pe = jnp.broadcast_to(pe, (2, halves, CMP_STRIDE, 2, NSA_DH)).reshape(2, halves, CMP_STRIDE * LANE)
    w1 = jnp.stack([s_w1_k, s_w1_v]).reshape(2, halves, CMP_STRIDE, NSA_DH, CMP_HIDDEN)
    w1 = jnp.einsum('ab,shtdk->shtadbk', jnp.eye(2, dtype=w1.dtype), w1)
    w1 = w1.reshape(2, halves, CMP_STRIDE * LANE, 2 * CMP_HIDDEN).astype(BF16)
    w2 = jnp.stack([_pad_cols(s_w2_k, LANE), _pad_cols(s_w2_v, LANE)]).astype(BF16)
    cmp_end = jnp.arange(n_chunk, dtype=jnp.int32) * CMP_STRIDE + CMP_BLOCK - 1
    cos_c, sin_c = _rope_tables(cmp_end)
    kv_cmp = _compress(kv_cmp_raw, pe, w1, w2, cos_c, sin_c, b_sz, t_len)
    k_cmp, v_cmp = kv_cmp[0], kv_cmp[1]

    n_sel = t_len // SEL_BLOCK
    c_start = jnp.arange(n_chunk, dtype=jnp.int32)[:, None] * CMP_STRIDE
    s_start = jnp.arange(n_sel, dtype=jnp.int32)[None, :] * SEL_BLOCK
    ov = jnp.clip(jnp.minimum(c_start + CMP_BLOCK, s_start + SEL_BLOCK) - jnp.maximum(c_start, s_start), 0, None)
    overlap = (ov.astype(F32) / CMP_BLOCK).astype(BF16)
    overlap_t = jnp.pad(overlap.T, ((0, LANE - n_sel), (0, 0)))

    o_cmp, sel_mask = _cmp_attention(q, k_cmp, v_cmp, overlap_t, t_len)
    o_sel = _sel_attention(q, k_sel, v_sel, sel_mask, t_len)
    o_win = _win_attention(q, k_win, v_win, t_len)

    lane_id = (jnp.arange(2 * LANE, dtype=jnp.int32) % LANE)[None, :, None]
    col_head = (jnp.arange(NSA_HEADS * NSA_DH, dtype=jnp.int32) // NSA_DH)[None, None, :]
    branch = jnp.arange(3, dtype=jnp.int32)[:, None, None]
    gate_expand = (lane_id == branch * NSA_HEADS + col_head).astype(BF16)
    out = _merge_out(o_cmp, o_sel, o_win, z_b, tail_b, gate_expand, b_w_out[0].astype(BF16),
                     h1, b_ln_w[0], b_ln_b[0], t_len, tm_row)
    return out.reshape(b_sz, t_len, d)
```
